```python
import math
import jax, jax.numpy as jnp
from jax import lax
import numpy as np

D_MODEL = 2048
BATCH = 2
SEQ = 16384
DEPTH = 1

DA_HEADS = 8
DA_HEAD_DIM = 64
DA_V_DIM = 2 * DA_HEAD_DIM
DA_QK_WIDTH = DA_HEADS * 2 * DA_HEAD_DIM
DA_WIDTH = DA_HEADS * DA_V_DIM
ROPE_THETA = 500000.0
ROPE_DIM = DA_HEAD_DIM // 4
Q_BLOCK = 128
GLA_HEADS = 4
GLA_KEY_DIM = 128
GLA_VAL_DIM = 256
GLA_QK_WIDTH = GLA_HEADS * GLA_KEY_DIM
GLA_WIDTH = GLA_HEADS * GLA_VAL_DIM
GLA_GATE_RANK = 16
GLA_GATE_TAU = 16.0
GLA_CHUNK = 64
D_FF = 4 * D_MODEL
N_MOD = 6
EPS = 1e-6

IN_SIZES = (DA_QK_WIDTH, DA_QK_WIDTH, DA_WIDTH,
            GLA_QK_WIDTH, GLA_QK_WIDTH, GLA_WIDTH, GLA_WIDTH, GLA_GATE_RANK,
            D_MODEL, D_MODEL)
IN_WIDTH = (3 * DA_QK_WIDTH + 2 * GLA_QK_WIDTH + 2 * GLA_WIDTH + GLA_GATE_RANK + 2 * D_MODEL)

kernel_name = "hybrid_diffattn_gla_gated_block"


def _split_points():
    pts, acc = [], 0
    for s in IN_SIZES[:-1]:
        acc += s
        pts.append(acc)
    return pts


def rms_norm(x, g):
    xf = x.astype(jnp.float32)
    y = xf * lax.rsqrt(jnp.mean(xf * xf, axis=-1, keepdims=True) + EPS)
    return (y * g.astype(jnp.float32)).astype(x.dtype)


def partial_rope(t, cos, sin):
    half = ROPE_DIM // 2
    tr, tp = t[..., :ROPE_DIM], t[..., ROPE_DIM:]
    rot = jnp.concatenate([-tr[..., half:], tr[..., :half]], axis=-1)
    return jnp.concatenate([tr * cos + rot * sin, tp], axis=-1)


def diff_attention(q, k, v, positions, q_norm_g, k_norm_g, lq1, lk1, lq2, lk2, subln_g, layer_idx):
    B, S = q.shape[0], q.shape[1]
    f32 = jnp.float32
    q = rms_norm(q.reshape(B, S, DA_HEADS, 2, DA_HEAD_DIM).astype(f32), q_norm_g)
    k = rms_norm(k.reshape(B, S, DA_HEADS, 2, DA_HEAD_DIM).astype(f32), k_norm_g)
    vf = v.reshape(B, S, DA_HEADS, DA_V_DIM).astype(f32)
    inv_freq = ROPE_THETA ** (-jnp.arange(0, ROPE_DIM, 2, dtype=f32) / ROPE_DIM)
    ang = positions.astype(f32)[..., None] * inv_freq
    ang = jnp.concatenate([ang, ang], axis=-1)[:, :, None, None, :]
    cos, sin = jnp.cos(ang), jnp.sin(ang)
    q = partial_rope(q, cos, sin) * (DA_HEAD_DIM ** -0.5)
    k = partial_rope(k, cos, sin)
    lam_init = 0.8 - 0.6 * math.exp(-0.3 * layer_idx)
    lam = (jnp.exp(jnp.sum(lq1.astype(f32) * lk1.astype(f32)))
           - jnp.exp(jnp.sum(lq2.astype(f32) * lk2.astype(f32))) + lam_init)
    nb = S // Q_BLOCK
    q_blocks = q.reshape(B, nb, Q_BLOCK, DA_HEADS, 2, DA_HEAD_DIM).transpose(1, 0, 2, 3, 4, 5)
    key_idx = jnp.arange(S)

    def one_block(args):
        qb, bi = args
        s = jnp.einsum('bqhcd,bkhcd->bhcqk', qb, k)
        q_idx = bi * Q_BLOCK + jnp.arange(Q_BLOCK)
        mask = key_idx[None, :] <= q_idx[:, None]
        p = jax.nn.softmax(jnp.where(mask, s, -jnp.inf), axis=-1)
        w = p[:, :, 0] - lam * p[:, :, 1]
        return jnp.einsum('bhqk,bkhe->bqhe', w, vf)

    o = lax.map(one_block, (q_blocks, jnp.arange(nb)))
    o = o.transpose(1, 0, 2, 3, 4).reshape(B, S, DA_HEADS, DA_V_DIM)
    o = rms_norm(o, subln_g) * (1.0 - lam_init)
    return o.reshape(B, S, DA_WIDTH)


def gated_linear_attention(q, k, v, r, a_low, a_up, a_bias, out_norm_g):
    B, S = q.shape[0], q.shape[1]
    f32 = jnp.float32
    C = GLA_CHUNK
    nc = S // C
    q = q.reshape(B, S, GLA_HEADS, GLA_KEY_DIM).astype(f32) * (GLA_KEY_DIM ** -0.5)
    k = k.reshape(B, S, GLA_HEADS, GLA_KEY_DIM).astype(f32)
    v = v.reshape(B, S, GLA_HEADS, GLA_VAL_DIM).astype(f32)
    log_a = jax.nn.log_sigmoid((a_low @ a_up + a_bias).astype(f32)) / GLA_GATE_TAU
    log_a = log_a.reshape(B, S, GLA_HEADS, GLA_KEY_DIM)

    def to_chunks(t):
        return t.reshape(B, nc, C, *t.shape[2:]).swapaxes(0, 1)

    causal = jnp.tril(jnp.ones((C, C), dtype=bool))

    def step(state, inp):
        qc, kc, vc, lac = inp
        b = jnp.cumsum(lac, axis=1)
        o_inter = jnp.einsum('bihk,bhkv->bihv', qc * jnp.exp(b), state)
        rel = b[:, :, None] - b[:, None]
        decay = jnp.exp(jnp.where(causal[None, :, :, None, None], rel, -jnp.inf))
        att = jnp.einsum('bihk,bjhk,bijhk->bhij', qc, kc, decay)
        o_intra = jnp.einsum('bhij,bjhv->bihv', att, vc)
        b_last = b[:, -1]
        state = (jnp.exp(b_last)[..., None] * state
                 + jnp.einsum('bjhk,bjhv->bhkv', kc * jnp.exp(b_last[:, None] - b), vc))
        return state, o_inter + o_intra

    state0 = jnp.zeros((B, GLA_HEADS, GLA_KEY_DIM, GLA_VAL_DIM), f32)
    _, o = lax.scan(step, state0, (to_chunks(q), to_chunks(k), to_chunks(v), to_chunks(log_a)))
    o = o.swapaxes(0, 1).reshape(B, S, GLA_HEADS, GLA_VAL_DIM)
    o = rms_norm(o, out_norm_g) * jax.nn.silu(r.astype(f32)).reshape(B, S, GLA_HEADS, GLA_VAL_DIM)
    return o.reshape(B, S, GLA_WIDTH)


def setup_inputs(seed: int = 0) -> dict:
    key = jax.random.key(seed)
    ks = jax.random.split(key, 24)
    f32 = jnp.float32
    L, D = DEPTH, D_MODEL

    def nrm(k, shape, scale):
        return jax.random.normal(k, shape, f32) * scale

    def gain(k, shape):
        return 1.0 + 0.02 * jax.random.normal(k, shape, f32)

    x = jax.random.normal(ks[0], (BATCH, SEQ, D), f32)
    c = jax.random.normal(ks[1], (BATCH, D), f32)
    positions = (jnp.arange(SEQ, dtype=jnp.int32)[None, :]
                 + jax.random.randint(ks[2], (BATCH, 1), 0, 1024, dtype=jnp.int32))
    return {
        "x": x,
        "c": c,
        "positions": positions,
        "w_ada": nrm(ks[3], (L, D, N_MOD * D), 0.5 * D ** -0.5),
        "b_ada": nrm(ks[4], (L, N_MOD * D), 0.02),
        "norm1_g": gain(ks[5], (L, D)),
        "w_in": nrm(ks[6], (L, D, IN_WIDTH), D ** -0.5),
        "da_q_norm_g": gain(ks[7], (L, DA_HEAD_DIM)),
        "da_k_norm_g": gain(ks[8], (L, DA_HEAD_DIM)),
        "da_lambda_q1": nrm(ks[9], (L, DA_HEAD_DIM), 0.1),
        "da_lambda_k1": nrm(ks[10], (L, DA_HEAD_DIM), 0.1),
        "da_lambda_q2": nrm(ks[11], (L, DA_HEAD_DIM), 0.1),
        "da_lambda_k2": nrm(ks[12], (L, DA_HEAD_DIM), 0.1),
        "da_subln_g": gain(ks[13], (L, DA_V_DIM)),
        "gla_gate_up": nrm(ks[14], (L, GLA_GATE_RANK, GLA_QK_WIDTH), GLA_GATE_RANK ** -0.5),
        "gla_gate_bias": nrm(ks[15], (L, GLA_QK_WIDTH), 0.1),
        "gla_out_norm_g": gain(ks[16], (L, GLA_VAL_DIM)),
        "w_branch_a": nrm(ks[17], (L, DA_WIDTH, D), DA_WIDTH ** -0.5),
        "w_branch_b": nrm(ks[18], (L, GLA_WIDTH, D), GLA_WIDTH ** -0.5),
        "w_out": nrm(ks[19], (L, D, D), D ** -0.5),
        "norm2_g": gain(ks[20], (L, D)),
        "w_mlp_in": nrm(ks[21], (L, D, D_FF), D ** -0.5),
        "w_mlp_out": nrm(ks[22], (L, D_FF, D), D_FF ** -0.5),
    }


def reference(x, c, positions, w_ada, b_ada, norm1_g, w_in, da_q_norm_g, da_k_norm_g,
              da_lambda_q1, da_lambda_k1, da_lambda_q2, da_lambda_k2, da_subln_g,
              gla_gate_up, gla_gate_bias, gla_out_norm_g, w_branch_a, w_branch_b, w_out,
              norm2_g, w_mlp_in, w_mlp_out):
    h = x
    split_pts = _split_points()
    for l in range(DEPTH):
        mod = jax.nn.silu(c) @ w_ada[l] + b_ada[l]
        shift1, scale1, gate1, shift2, scale2, gate2 = jnp.split(mod[:, None, :], N_MOD, axis=-1)
        u = rms_norm(h, norm1_g[l]) * (1.0 + scale1) + shift1
        proj = u @ w_in[l]
        (da_q, da_k, da_v, g_q, g_k, g_v, g_r, g_a, gate_a, gate_b) = jnp.split(proj, split_pts, axis=-1)
        y_a = diff_attention(da_q, da_k, da_v, positions, da_q_norm_g[l], da_k_norm_g[l],
                             da_lambda_q1[l], da_lambda_k1[l], da_lambda_q2[l], da_lambda_k2[l],
                             da_subln_g[l], l).astype(h.dtype)
        y_b = gated_linear_attention(g_q, g_k, g_v, g_r, g_a, gla_gate_up[l], gla_gate_bias[l],
                                     gla_out_norm_g[l]).astype(h.dtype)
        merged = (jax.nn.sigmoid(gate_a) * (y_a @ w_branch_a[l])
                  + jax.nn.sigmoid(gate_b) * (y_b @ w_branch_b[l]))
        h = h + gate1 * (merged @ w_out[l])
        u2 = rms_norm(h, norm2_g[l]) * (1.0 + scale2) + shift2
        hid = jnp.square(jax.nn.relu(u2 @ w_mlp_in[l]))
        h = h + gate2 * (hid @ w_mlp_out[l])
    return h
```

```python
import functools
import math

import jax
import jax.numpy as jnp
import numpy as np
from jax import lax
from jax.experimental import pallas as pl
from jax.experimental.pallas import tpu as pltpu

F32 = jnp.float32
BF16 = jnp.bfloat16

D_MODEL = 2048
DA_HEADS = 8
DA_HEAD_DIM = 64
DA_V_DIM = 2 * DA_HEAD_DIM
DA_QK_WIDTH = DA_HEADS * 2 * DA_HEAD_DIM
DA_WIDTH = DA_HEADS * DA_V_DIM
ROPE_THETA = 500000.0
ROPE_DIM = DA_HEAD_DIM // 4
GLA_HEADS = 4
GLA_KEY_DIM = 128
GLA_VAL_DIM = 256
GLA_QK_WIDTH = GLA_HEADS * GLA_KEY_DIM
GLA_WIDTH = GLA_HEADS * GLA_VAL_DIM
GLA_GATE_RANK = 16
GLA_GATE_TAU = 16.0
GLA_CHUNK = 64
D_FF = 4 * D_MODEL
N_MOD = 6
EPS = 1e-6
IN_SIZES = (DA_QK_WIDTH, DA_QK_WIDTH, DA_WIDTH, GLA_QK_WIDTH, GLA_QK_WIDTH, GLA_WIDTH, GLA_WIDTH,
            GLA_GATE_RANK, D_MODEL, D_MODEL)

LANES = 128
SUBLANES = 8
VMEM_LIMIT_BYTES = 56 * 1024 * 1024

ADA_TN = 1024
NORM_TM = 512
PROJ_TM = 512
ACT_TN = 1024
ATT_T = PROJ_TM
GLA_TM = 512
MERGE_TM = 256
MLP_TM = 512
MLP_TF = 1024

GLA_LEVELS = (32, 16, 8, 4, 2, 1)


def _dot(a, b):
    return jnp.dot(a, b, preferred_element_type=F32)


def _dot_nt(a, b):
    return lax.dot_general(a, b, (((1,), (1,)), ((), ())), preferred_element_type=F32)


def _dot_tn(a, b):
    return lax.dot_general(a, b, (((0,), (0,)), ((), ())), preferred_element_type=F32)


def _split_bf16(x):
    hi = x.astype(BF16)
    lo = (x - hi.astype(F32)).astype(BF16)
    return hi, lo


def _params(*semantics):
    return pltpu.CompilerParams(dimension_semantics=semantics, vmem_limit_bytes=VMEM_LIMIT_BYTES)


def _resident(shape):
    nd = len(shape)
    return pl.BlockSpec(shape, lambda *_: (0,) * nd, pipeline_mode=pl.Buffered(1))


def _ada_kernel(c_ref, w_ref, b_ref, o_ref):
    c = c_ref[...]
    sc = c * jax.nn.sigmoid(c)
    o_ref[...] = _dot(sc.astype(BF16), w_ref[...].astype(BF16)) + b_ref[...]


def _ada(c, w_ada, b_ada):
    bsz, d = c.shape
    n = w_ada.shape[1]
    c8 = jnp.zeros((SUBLANES, d), F32).at[:bsz].set(c)
    out = pl.pallas_call(
        _ada_kernel,
        grid=(n // ADA_TN,),
        in_specs=[pl.BlockSpec((SUBLANES, d), lambda j: (0, 0)),
                  pl.BlockSpec((d, ADA_TN), lambda j: (0, j)),
                  pl.BlockSpec((1, ADA_TN), lambda j: (0, j))],
        out_specs=pl.BlockSpec((SUBLANES, ADA_TN), lambda j: (0, j)),
        out_shape=jax.ShapeDtypeStruct((SUBLANES, n), F32),
        compiler_params=_params("arbitrary"),
        name="ada",
    )(c8, w_ada, b_ada.reshape(1, n))
    return out[:bsz].reshape(bsz, N_MOD, d)


def _modulated_norm(x, gain, mod, shift_idx, scale_idx):
    ms = jnp.mean(x * x, axis=-1, keepdims=True)
    y = x * lax.rsqrt(ms + EPS) * gain
    return y * (1.0 + mod[scale_idx:scale_idx + 1]) + mod[shift_idx:shift_idx + 1]


def _normmod_kernel(x_ref, g_ref, mod_ref, o_ref):
    o_ref[...] = _modulated_norm(x_ref[...], g_ref[...], mod_ref[0], 0, 1).astype(BF16)


def _normmod(x2, gain, mod, seq):
    t, d = x2.shape
    tm = min(NORM_TM, seq)
    tpb = seq // tm
    return pl.pallas_call(
        _normmod_kernel,
        grid=(t // tm,),
        in_specs=[pl.BlockSpec((tm, d), lambda i: (i, 0)),
                  pl.BlockSpec((1, d), lambda i: (0, 0)),
                  pl.BlockSpec((1, N_MOD, d), lambda i: (i // tpb, 0, 0))],
        out_specs=pl.BlockSpec((tm, d), lambda i: (i, 0)),
        out_shape=jax.ShapeDtypeStruct((t, d), BF16),
        compiler_params=_params("parallel"),
        name="normmod",
    )(x2, gain.reshape(1, d), mod)


def _qkv_kernel(u_ref, w_ref, pos_ref, freq_ref, sa_ref, sb_ref, gq_ref, gk_ref, grp_ref,
                qt_ref, k_ref, vt_ref):
    u = u_ref[...]
    ang = pos_ref[...].astype(F32) * freq_ref[...]
    cs = jnp.cos(ang)
    sn = jnp.sin(ang)
    sin_a = sn * sa_ref[...]
    sin_b = sn * sb_ref[...]
    grp = grp_ref[...]

    def norm_rope(x, gain):
        hi, lo = _split_bf16(x * x)
        ss = _dot(hi, grp) + _dot(lo, grp)
        xn = x * lax.rsqrt(ss * (1.0 / DA_HEAD_DIM) + EPS) * gain
        half = ROPE_DIM // 2
        return (xn * cs + pltpu.roll(xn, LANES - half, 1) * sin_a + pltpu.roll(xn, half, 1) * sin_b)

    acc = _dot(u, w_ref[:, 0:DA_QK_WIDTH])
    gq = gq_ref[...]
    for h in range(DA_HEADS):
        xq = norm_rope(acc[:, h * LANES:(h + 1) * LANES], gq) * (DA_HEAD_DIM ** -0.5)
        qt_ref[0, h, 0] = xq.T.astype(BF16)
    acc = _dot(u, w_ref[:, DA_QK_WIDTH:2 * DA_QK_WIDTH])
    gk = gk_ref[...]
    for h in range(DA_HEADS):
        k_ref[0, h] = norm_rope(acc[:, h * LANES:(h + 1) * LANES], gk).astype(BF16)
    acc = _dot(u, w_ref[:, 2 * DA_QK_WIDTH:2 * DA_QK_WIDTH + DA_WIDTH])
    for h in range(DA_HEADS):
        vt_ref[0, h, 0] = acc[:, h * LANES:(h + 1) * LANES].T.astype(BF16)


def _rope_lane_tables():
    lane = np.arange(LANES) % DA_HEAD_DIM
    inv_freq = ROPE_THETA ** (-jnp.arange(0, ROPE_DIM, 2, dtype=F32) / ROPE_DIM)
    half = ROPE_DIM // 2
    freq = jnp.where(lane < ROPE_DIM, inv_freq[lane % half], 0.0).astype(F32).reshape(1, LANES)
    sign_a = np.where(lane < half, -1.0, 0.0).astype(np.float32).reshape(1, LANES)
    mask_b = np.where((lane >= half) & (lane < ROPE_DIM), 1.0, 0.0).astype(np.float32).reshape(1, LANES)
    group = (np.arange(LANES)[:, None] // DA_HEAD_DIM == np.arange(LANES)[None, :] // DA_HEAD_DIM)
    return freq, jnp.asarray(sign_a), jnp.asarray(mask_b), jnp.asarray(group, dtype=BF16)


def _qkv(u, w_qkv, positions, q_gain, k_gain, bsz, seq):
    t, d = u.shape
    tm = min(PROJ_TM, seq)
    tpb = seq // tm
    freq, sign_a, mask_b, group = _rope_lane_tables()
    lane_vec = lambda v: jnp.tile(v.astype(F32), LANES // DA_HEAD_DIM).reshape(1, LANES)
    vec_spec = pl.BlockSpec((1, LANES), lambda i: (0, 0))
    t_shape = jax.ShapeDtypeStruct((bsz, DA_HEADS, tpb, LANES, tm), BF16)
    t_spec = pl.BlockSpec((1, DA_HEADS, 1, LANES, tm), lambda i: (i // tpb, 0, i % tpb, 0, 0))
    return pl.pallas_call(
        _qkv_kernel,
        grid=(t // tm,),
        in_specs=[pl.BlockSpec((tm, d), lambda i: (i, 0)),
                  _resident(w_qkv.shape),
                  pl.BlockSpec((tm, 1), lambda i: (i, 0)),
                  vec_spec, vec_spec, vec_spec, vec_spec, vec_spec,
                  pl.BlockSpec((LANES, LANES), lambda i: (0, 0))],
        out_specs=[t_spec,
                   pl.BlockSpec((1, DA_HEADS, tm, LANES), lambda i: (i // tpb, 0, i % tpb, 0)),
                   t_spec],
        out_shape=[t_shape, jax.ShapeDtypeStruct((bsz, DA_HEADS, seq, LANES), BF16), t_shape],
        compiler_params=_params("parallel"),
        name="qkv",
    )(u, w_qkv, positions.reshape(t, 1), freq, sign_a, mask_b, lane_vec(q_gain), lane_vec(k_gain), group)


def _glaqk_kernel(u_ref, w_ref, up_ref, bias_ref, q_ref, k_ref, la_ref):
    acc = _dot(u_ref[...], w_ref[...])
    q_ref[...] = (acc[:, 0:GLA_QK_WIDTH] * (GLA_KEY_DIM ** -0.5)).astype(BF16)
    k_ref[...] = acc[:, GLA_QK_WIDTH:2 * GLA_QK_WIDTH].astype(BF16)
    low = acc[:, 2 * GLA_QK_WIDTH:]
    low_hi, low_lo = _split_bf16(low)
    up_hi, up_lo = _split_bf16(up_ref[...])
    z = _dot(low_hi, up_hi) + _dot(low_lo, up_hi) + _dot(low_hi, up_lo) + bias_ref[...]
    log_sig = jnp.minimum(z, 0.0) - jnp.log1p(jnp.exp(-jnp.abs(z)))
    la_ref[...] = log_sig * (1.0 / GLA_GATE_TAU)


def _glaqk(u, w_g, gate_up, gate_bias, seq):
    t, d = u.shape
    tm = min(PROJ_TM, seq)
    up = jnp.zeros((LANES, GLA_QK_WIDTH), F32).at[:GLA_GATE_RANK].set(gate_up)
    row_spec = lambda w: pl.BlockSpec((tm, w), lambda i: (i, 0))
    return pl.pallas_call(
        _glaqk_kernel,
        grid=(t // tm,),
        in_specs=[row_spec(d), _resident(w_g.shape), _resident(up.shape),
                  pl.BlockSpec((1, GLA_QK_WIDTH), lambda i: (0, 0))],
        out_specs=[row_spec(GLA_QK_WIDTH), row_spec(GLA_QK_WIDTH), row_spec(GLA_QK_WIDTH)],
        out_shape=[jax.ShapeDtypeStruct((t, GLA_QK_WIDTH), BF16),
                   jax.ShapeDtypeStruct((t, GLA_QK_WIDTH), BF16),
                   jax.ShapeDtypeStruct((t, GLA_QK_WIDTH), F32)],
        compiler_params=_params("parallel"),
        name="glaqk",
    )(u, w_g, up, gate_bias.reshape(1, GLA_QK_WIDTH))


ACT_SILU_BLOCKS = (GLA_WIDTH // ACT_TN, 2 * GLA_WIDTH // ACT_TN)


def _act_kernel(u_ref, w_ref, o_ref):
    j = pl.program_id(0)
    acc = _dot(u_ref[...], w_ref[...])
    sig = jax.nn.sigmoid(acc)
    out = jnp.where(j < ACT_SILU_BLOCKS[0], acc, jnp.where(j < ACT_SILU_BLOCKS[1], acc * sig, sig))
    o_ref[...] = out.astype(BF16)


def _act(u, w_act, seq):
    t, d = u.shape
    n = w_act.shape[1]
    tm = min(PROJ_TM, seq)
    return pl.pallas_call(
        _act_kernel,
        grid=(n // ACT_TN, t // tm),
        in_specs=[pl.BlockSpec((tm, d), lambda j, i: (i, 0)),
                  pl.BlockSpec((d, ACT_TN), lambda j, i: (0, j))],
        out_specs=pl.BlockSpec((tm, ACT_TN), lambda j, i: (i, j)),
        out_shape=jax.ShapeDtypeStruct((t, n), BF16),
        compiler_params=_params("arbitrary", "arbitrary"),
        name="act",
    )(u, w_act)


def _attn_kernel(qt_ref, k_ref, vt_ref, lq1_ref, lk1_ref, lq2_ref, lk2_ref, subg_ref, o_ref,
                 acc_ref, m_ref, l_ref, *, tile, lam_init):
    i = pl.program_id(2)
    qt = qt_ref[0, 0, 0]
    row = lax.broadcasted_iota(jnp.int32, qt.shape, 0)
    zero = jnp.zeros_like(qt)
    qt_comp = (jnp.where(row < DA_HEAD_DIM, qt, zero), jnp.where(row >= DA_HEAD_DIM, qt, zero))
    acc_ref[...] = jnp.zeros_like(acc_ref)
    l_ref[...] = jnp.zeros_like(l_ref)
    m_ref[...] = jnp.full_like(m_ref, -jnp.inf)

    def block(j, masked):
        kb = k_ref[0, 0, pl.ds(pl.multiple_of(j * tile, tile), tile), :]
        vtb = vt_ref[0, 0, j]
        if masked:
            key = j * tile + lax.broadcasted_iota(jnp.int32, (tile, tile), 0)
            qry = i * tile + lax.broadcasted_iota(jnp.int32, (tile, tile), 1)
            keep = key <= qry
        for c in range(2):
            s = _dot(kb, qt_comp[c])
            if masked:
                s = jnp.where(keep, s, -jnp.inf)
            m_old = m_ref[c]
            m_new = jnp.maximum(m_old, jnp.max(s, axis=0, keepdims=True))
            alpha = jnp.exp(m_old - m_new)
            p = jnp.exp(s - m_new)
            l_ref[c] = alpha * l_ref[c] + jnp.sum(p, axis=0, keepdims=True)
            acc_ref[c] = alpha * acc_ref[c] + _dot(vtb, p.astype(BF16))
            m_ref[c] = m_new

    def full_block(j, carry):
        block(j, False)
        return carry

    lax.fori_loop(0, i, full_block, 0)
    block(i, True)

    s1 = jnp.sum(lq1_ref[...] * lk1_ref[...], axis=1, keepdims=True)
    s2 = jnp.sum(lq2_ref[...] * lk2_ref[...], axis=1, keepdims=True)
    lam = jnp.exp(s1) - jnp.exp(s2) + lam_init
    o = acc_ref[0] / l_ref[0] - lam * (acc_ref[1] / l_ref[1])
    ms = jnp.mean(o * o, axis=0, keepdims=True)
    y = o * lax.rsqrt(ms + EPS) * subg_ref[...] * (1.0 - lam_init)
    o_ref[...] = y.T.astype(BF16)


def _attention(qt, k, vt, lq1, lk1, lq2, lk2, subln_g, layer_idx):
    bsz, heads, nblk, _, tile = qt.shape
    seq = nblk * tile
    lam_init = 0.8 - 0.6 * math.exp(-0.3 * layer_idx)
    vec = lambda v: v.astype(F32).reshape(1, DA_HEAD_DIM)
    vec_spec = pl.BlockSpec((1, DA_HEAD_DIM), lambda b, h, i: (0, 0))
    return pl.pallas_call(
        functools.partial(_attn_kernel, tile=tile, lam_init=lam_init),
        grid=(bsz, heads, nblk),
        in_specs=[pl.BlockSpec((1, 1, 1, LANES, tile), lambda b, h, i: (b, h, i, 0, 0)),
                  pl.BlockSpec((1, 1, seq, LANES), lambda b, h, i: (b, h, 0, 0)),
                  pl.BlockSpec((1, 1, nblk, LANES, tile), lambda b, h, i: (b, h, 0, 0, 0)),
                  vec_spec, vec_spec, vec_spec, vec_spec,
                  pl.BlockSpec((DA_V_DIM, 1), lambda b, h, i: (0, 0))],
        out_specs=pl.BlockSpec((tile, LANES), lambda b, h, i: (b * nblk + i, h)),
        out_shape=jax.ShapeDtypeStruct((bsz * seq, DA_WIDTH), BF16),
        scratch_shapes=[pltpu.VMEM((2, DA_V_DIM, tile), F32),
                        pltpu.VMEM((2, 1, tile), F32),
                        pltpu.VMEM((2, 1, tile), F32)],
        compiler_params=_params("parallel", "parallel", "arbitrary"),
        name="attn",
    )(qt, k, vt, vec(lq1), vec(lk1), vec(lq2), vec(lk2), subln_g.astype(F32).reshape(DA_V_DIM, 1))


def _gla_decay_matrix():
    c = GLA_CHUNK
    t = np.arange(c)[:, None]
    s = np.arange(c)[None, :]
    blocks = [(s <= t)]
    for h in GLA_LEVELS:
        r = (t // (2 * h)) * (2 * h) + h
        upper = (t >= r) & (s > r) & (s <= t)
        lower = (t < r) & (s > t) & (s <= r)
        blocks.append(upper | lower)
    return jnp.asarray(np.concatenate(blocks, axis=0), dtype=BF16)


def _gla_kernel(q_ref, k_ref, v_ref, la_ref, sr_ref, fm_ref, g_ref, o_ref, st_ref, *, nchunk):
    @pl.when(pl.program_id(1) == 0)
    def _():
        st_ref[...] = jnp.zeros_like(st_ref)

    c = GLA_CHUNK
    ii = lax.broadcasted_iota(jnp.int32, (c, c), 0)
    jj = lax.broadcasted_iota(jnp.int32, (c, c), 1)
    eye = ii == jj
    level_masks = []
    for h in GLA_LEVELS:
        shift = int(math.log2(2 * h))
        same = (ii >> shift) == (jj >> shift)
        level_masks.append(same & ((ii & (2 * h - 1)) >= h) & ((jj & (2 * h - 1)) < h))
    fm = fm_ref[...]
    gain = g_ref[...]

    def chunk(n, carry):
        rows = pl.ds(pl.multiple_of(n * c, c), c)
        la = la_ref[rows, :]
        la_hi, la_lo = _split_bf16(la)
        fa = _dot(fm, la_hi) + _dot(fm, la_lo)
        b = fa[0:c]
        b_last = b[c - 1:c]
        e_b = jnp.exp(b)
        e_rest = jnp.exp(b_last - b)
        e_last = jnp.exp(b_last)
        e_lvl = [jnp.exp(fa[(l + 1) * c:(l + 2) * c]) for l in range(len(GLA_LEVELS))]
        for hd in range(GLA_HEADS):
            ks = slice(hd * GLA_KEY_DIM, (hd + 1) * GLA_KEY_DIM)
            vs = slice(hd * GLA_VAL_DIM, (hd + 1) * GLA_VAL_DIM)
            qb = q_ref[rows, ks]
            kb = k_ref[rows, ks]
            q = qb.astype(F32)
            k = kb.astype(F32)
            v = v_ref[rows, vs]
            state = st_ref[hd]
            out = _dot_nt((q * e_b[:, ks]).astype(BF16), state.astype(BF16))
            att = jnp.where(eye, _dot_nt(qb, kb), 0.0)
            for l in range(len(GLA_LEVELS)):
                e = e_lvl[l][:, ks]
                att = jnp.where(level_masks[l], _dot_nt((q * e).astype(BF16), (k * e).astype(BF16)), att)
            out = out + _dot(att.astype(BF16), v)
            ms = jnp.mean(out * out, axis=-1, keepdims=True)
            y = out * lax.rsqrt(ms + EPS) * gain * sr_ref[rows, vs].astype(F32)
            o_ref[rows, vs] = y.astype(BF16)
            st_ref[hd] = state * e_last[:, ks] + _dot_tn(v, (k * e_rest[:, ks]).astype(BF16))
        return carry

    lax.fori_loop(0, nchunk, chunk, 0)


def _gla(gq, gk, act, la, out_gain, bsz, seq):
    t = gq.shape[0]
    tm = min(GLA_TM, seq)
    tpb = seq // tm
    fm = _gla_decay_matrix()
    row = lambda b, n: (b * tpb + n, 0)
    return pl.pallas_call(
        functools.partial(_gla_kernel, nchunk=tm // GLA_CHUNK),
        grid=(bsz, tpb),
        in_specs=[pl.BlockSpec((tm, GLA_QK_WIDTH), row),
                  pl.BlockSpec((tm, GLA_QK_WIDTH), row),
                  pl.BlockSpec((tm, GLA_WIDTH), row),
                  pl.BlockSpec((tm, GLA_QK_WIDTH), row),
                  pl.BlockSpec((tm, GLA_WIDTH), lambda b, n: (b * tpb + n, 1)),
                  pl.BlockSpec(fm.shape, lambda b, n: (0, 0)),
                  pl.BlockSpec((1, GLA_VAL_DIM), lambda b, n: (0, 0))],
        out_specs=pl.BlockSpec((tm, GLA_WIDTH), row),
        out_shape=jax.ShapeDtypeStruct((t, GLA_WIDTH), BF16),
        scratch_shapes=[pltpu.VMEM((GLA_HEADS, GLA_VAL_DIM, GLA_KEY_DIM), F32)],
        compiler_params=_params("parallel", "arbitrary"),
        name="gla",
    )(gq, gk, act, la, act, fm, out_gain.astype(F32).reshape(1, GLA_VAL_DIM))


def _merge_kernel(ya_ref, yb_ref, sa_ref, sb_ref, x_ref, mod_ref, wa_ref, wb_ref, wo_ref, o_ref):
    ta = _dot(ya_ref[...], wa_ref[...])
    tb = _dot(yb_ref[...], wb_ref[...])
    merged = sa_ref[...].astype(F32) * ta + sb_ref[...].astype(F32) * tb
    delta = _dot(merged.astype(BF16), wo_ref[...])
    o_ref[...] = x_ref[...] + mod_ref[0][2:3] * delta


def _merge(ya, yb, act, x2, mod, w_a, w_b, w_o, seq):
    t, d = x2.shape
    tm = min(MERGE_TM, seq)
    tpb = seq // tm
    gate_blk = 2 * GLA_WIDTH // d
    return pl.pallas_call(
        _merge_kernel,
        grid=(t // tm,),
        in_specs=[pl.BlockSpec((tm, DA_WIDTH), lambda i: (i, 0)),
                  pl.BlockSpec((tm, GLA_WIDTH), lambda i: (i, 0)),
                  pl.BlockSpec((tm, d), lambda i: (i, gate_blk)),
                  pl.BlockSpec((tm, d), lambda i: (i, gate_blk + 1)),
                  pl.BlockSpec((tm, d), lambda i: (i, 0)),
                  pl.BlockSpec((1, N_MOD, d), lambda i: (i // tpb, 0, 0)),
                  _resident(w_a.shape), _resident(w_b.shape), _resident(w_o.shape)],
        out_specs=pl.BlockSpec((tm, d), lambda i: (i, 0)),
        out_shape=jax.ShapeDtypeStruct((t, d), F32),
        compiler_params=_params("parallel"),
        name="merge",
    )(ya, yb, act, act, x2, mod, w_a, w_b, w_o)


def _mlp_kernel(h_ref, g_ref, mod_ref, w1_ref, w2_ref, o_ref, u_ref, acc_ref):
    kf = pl.program_id(1)

    @pl.when(kf == 0)
    def _():
        u_ref[...] = _modulated_norm(h_ref[...], g_ref[...], mod_ref[0], 3, 4).astype(BF16)
        acc_ref[...] = jnp.zeros_like(acc_ref)

    hid = jnp.square(jnp.maximum(_dot(u_ref[...], w1_ref[...]), 0.0))
    acc_ref[...] += _dot(hid.astype(BF16), w2_ref[...])

    @pl.when(kf == pl.num_programs(1) - 1)
    def _():
        o_ref[...] = h_ref[...] + mod_ref[0][5:6] * acc_ref[...]


def _mlp(h, gain, mod, w1, w2, seq):
    t, d = h.shape
    tm = min(MLP_TM, seq)
    tpb = seq // tm
    ff = w1.shape[1]
    return pl.pallas_call(
        _mlp_kernel,
        grid=(t // tm, ff // MLP_TF),
        in_specs=[pl.BlockSpec((tm, d), lambda i, kf: (i, 0)),
                  pl.BlockSpec((1, d), lambda i, kf: (0, 0)),
                  pl.BlockSpec((1, N_MOD, d), lambda i, kf: (i // tpb, 0, 0)),
                  pl.BlockSpec((d, MLP_TF), lambda i, kf: (0, kf)),
                  pl.BlockSpec((MLP_TF, d), lambda i, kf: (kf, 0))],
        out_specs=pl.BlockSpec((tm, d), lambda i, kf: (i, 0)),
        out_shape=jax.ShapeDtypeStruct((t, d), F32),
        scratch_shapes=[pltpu.VMEM((tm, d), BF16), pltpu.VMEM((tm, d), F32)],
        compiler_params=_params("parallel", "arbitrary"),
        name="mlp",
    )(h, gain.reshape(1, d), mod, w1, w2)


def _layer(h2, c, positions, bsz, seq, layer_idx, w_ada, b_ada, norm1_g, w_in, da_q_norm_g, da_k_norm_g,
           lq1, lk1, lq2, lk2, da_subln_g, gla_gate_up, gla_gate_bias, gla_out_norm_g,
           w_branch_a, w_branch_b, w_out, norm2_g, w_mlp_in, w_mlp_out):
    d = h2.shape[1]
    assert seq % min(PROJ_TM, seq) == 0 and seq % GLA_CHUNK == 0
    offs = np.concatenate([[0], np.cumsum(IN_SIZES)])
    col = lambda a, b: w_in[:, offs[a]:offs[b]].astype(BF16)
    w_qkv = col(0, 3)
    w_g = jnp.concatenate([col(3, 5), col(7, 8), jnp.zeros((d, LANES - GLA_GATE_RANK), BF16)], axis=1)
    w_act = jnp.concatenate([col(5, 7), col(8, 10)], axis=1)

    mod = _ada(c, w_ada, b_ada)
    u = _normmod(h2, norm1_g, mod, seq)
    qt, k, vt = _qkv(u, w_qkv, positions, da_q_norm_g, da_k_norm_g, bsz, seq)
    gq, gk, la = _glaqk(u, w_g, gla_gate_up, gla_gate_bias, seq)
    act = _act(u, w_act, seq)
    ya = _attention(qt, k, vt, lq1, lk1, lq2, lk2, da_subln_g, layer_idx)
    yb = _gla(gq, gk, act, la, gla_out_norm_g, bsz, seq)
    h1 = _merge(ya, yb, act, h2, mod, w_branch_a.astype(BF16), w_branch_b.astype(BF16),
                w_out.astype(BF16), seq)
    return _mlp(h1, norm2_g, mod, w_mlp_in.astype(BF16), w_mlp_out.astype(BF16), seq)


def kernel(x, c, positions, w_ada, b_ada, norm1_g, w_in, da_q_norm_g, da_k_norm_g, da_lambda_q1,
           da_lambda_k1, da_lambda_q2, da_lambda_k2, da_subln_g, gla_gate_up, gla_gate_bias,
           gla_out_norm_g, w_branch_a, w_branch_b, w_out, norm2_g, w_mlp_in, w_mlp_out):
    bsz, seq, d = x.shape
    h = x.reshape(bsz * seq, d)
    for l in range(w_ada.shape[0]):
        h = _layer(h, c, positions, bsz, seq, l, w_ada[l], b_ada[l], norm1_g[l], w_in[l], da_q_norm_g[l],
                   da_k_norm_g[l], da_lambda_q1[l], da_lambda_k1[l], da_lambda_q2[l], da_lambda_k2[l],
                   da_subln_g[l], gla_gate_up[l], gla_gate_bias[l], gla_out_norm_g[l], w_branch_a[l],
                   w_branch_b[l], w_out[l], norm2_g[l], w_mlp_in[l], w_mlp_out[l])
    return h.reshape(bsz, seq, d)
```

```python
import functools
import math

import jax
import jax.numpy as jnp
import numpy as np
from jax import lax
from jax.experimental import pallas as pl
from jax.experimental.pallas import tpu as pltpu

F32 = jnp.float32
BF16 = jnp.bfloat16

D_MODEL = 2048
DA_HEADS = 8
DA_HEAD_DIM = 64
DA_V_DIM = 2 * DA_HEAD_DIM
DA_QK_WIDTH = DA_HEADS * 2 * DA_HEAD_DIM
DA_WIDTH = DA_HEADS * DA_V_DIM
ROPE_THETA = 500000.0
ROPE_DIM = DA_HEAD_DIM // 4
GLA_HEADS = 4
GLA_KEY_DIM = 128
GLA_VAL_DIM = 256
GLA_QK_WIDTH = GLA_HEADS * GLA_KEY_DIM
GLA_WIDTH = GLA_HEADS * GLA_VAL_DIM
GLA_GATE_RANK = 16
GLA_GATE_TAU = 16.0
GLA_CHUNK = 64
D_FF = 4 * D_MODEL
N_MOD = 6
EPS = 1e-6
IN_SIZES = (DA_QK_WIDTH, DA_QK_WIDTH, DA_WIDTH, GLA_QK_WIDTH, GLA_QK_WIDTH, GLA_WIDTH, GLA_WIDTH,
            GLA_GATE_RANK, D_MODEL, D_MODEL)

LANES = 128
SUBLANES = 8
VMEM_LIMIT_BYTES = 56 * 1024 * 1024

ADA_TN = 1024
NORM_TM = 512
PROJ_TM = 512
ACT_TN = 1024
ATT_TK = PROJ_TM
ATT_TQ = 2 * ATT_TK
ATT_QC = 256
ATT_VROWS = DA_V_DIM + 16
GLA_TM = 512
MERGE_TM = 256
MLP_TM = 512
MLP_TF = 1024

GLA_LEVELS = (32, 16, 8, 4, 2, 1)


def _dot(a, b):
    return jnp.dot(a, b, preferred_element_type=F32)


def _dot_nt(a, b):
    return lax.dot_general(a, b, (((1,), (1,)), ((), ())), preferred_element_type=F32)


def _dot_tn(a, b):
    return lax.dot_general(a, b, (((0,), (0,)), ((), ())), preferred_element_type=F32)


def _split_bf16(x):
    hi = x.astype(BF16)
    lo = (x - hi.astype(F32)).astype(BF16)
    return hi, lo


def _params(*semantics):
    return pltpu.CompilerParams(dimension_semantics=semantics, vmem_limit_bytes=VMEM_LIMIT_BYTES)


def _resident(shape):
    nd = len(shape)
    return pl.BlockSpec(shape, lambda *_: (0,) * nd, pipeline_mode=pl.Buffered(1))


def _ada_kernel(c_ref, w_ref, b_ref, o_ref):
    c = c_ref[...]
    sc = c * jax.nn.sigmoid(c)
    o_ref[...] = _dot(sc.astype(BF16), w_ref[...].astype(BF16)) + b_ref[...]


def _ada(c, w_ada, b_ada):
    bsz, d = c.shape
    n = w_ada.shape[1]
    c8 = jnp.zeros((SUBLANES, d), F32).at[:bsz].set(c)
    out = pl.pallas_call(
        _ada_kernel,
        grid=(n // ADA_TN,),
        in_specs=[pl.BlockSpec((SUBLANES, d), lambda j: (0, 0)),
                  pl.BlockSpec((d, ADA_TN), lambda j: (0, j)),
                  pl.BlockSpec((1, ADA_TN), lambda j: (0, j))],
        out_specs=pl.BlockSpec((SUBLANES, ADA_TN), lambda j: (0, j)),
        out_shape=jax.ShapeDtypeStruct((SUBLANES, n), F32),
        compiler_params=_params("arbitrary"),
        name="ada",
    )(c8, w_ada, b_ada.reshape(1, n))
    return out[:bsz].reshape(bsz, N_MOD, d)


def _modulated_norm(x, gain, mod, shift_idx, scale_idx):
    ms = jnp.mean(x * x, axis=-1, keepdims=True)
    y = x * lax.rsqrt(ms + EPS) * gain
    return y * (1.0 + mod[scale_idx:scale_idx + 1]) + mod[shift_idx:shift_idx + 1]


def _normmod_kernel(x_ref, g_ref, mod_ref, o_ref):
    o_ref[...] = _modulated_norm(x_ref[...], g_ref[...], mod_ref[0], 0, 1).astype(BF16)


def _normmod(x2, gain, mod, seq):
    t, d = x2.shape
    tm = min(NORM_TM, seq)
    tpb = seq // tm
    return pl.pallas_call(
        _normmod_kernel,
        grid=(t // tm,),
        in_specs=[pl.BlockSpec((tm, d), lambda i: (i, 0)),
                  pl.BlockSpec((1, d), lambda i: (0, 0)),
                  pl.BlockSpec((1, N_MOD, d), lambda i: (i // tpb, 0, 0))],
        out_specs=pl.BlockSpec((tm, d), lambda i: (i, 0)),
        out_shape=jax.ShapeDtypeStruct((t, d), BF16),
        compiler_params=_params("parallel"),
        name="normmod",
    )(x2, gain.reshape(1, d), mod)


def _qkv_kernel(u_ref, w_ref, pos_ref, freq_ref, sa_ref, sb_ref, gq_ref, gk_ref, grp_ref,
                qt_ref, k_ref, vt_ref):
    u = u_ref[...]
    ang = pos_ref[...].astype(F32) * freq_ref[...]
    cs = jnp.cos(ang)
    sn = jnp.sin(ang)
    sin_a = sn * sa_ref[...]
    sin_b = sn * sb_ref[...]
    grp = grp_ref[...]

    def norm_rope(x, gain):
        hi, lo = _split_bf16(x * x)
        ss = _dot(hi, grp) + _dot(lo, grp)
        xn = x * lax.rsqrt(ss * (1.0 / DA_HEAD_DIM) + EPS) * gain
        half = ROPE_DIM // 2
        return (xn * cs + pltpu.roll(xn, LANES - half, 1) * sin_a + pltpu.roll(xn, half, 1) * sin_b)

    q_scale = (DA_HEAD_DIM ** -0.5) * math.log2(math.e)
    acc = _dot(u, w_ref[:, 0:DA_QK_WIDTH])
    gq = gq_ref[...]
    for h in range(DA_HEADS):
        xq = norm_rope(acc[:, h * LANES:(h + 1) * LANES], gq) * q_scale
        qt_ref[0, h, 0] = xq.T.astype(BF16)
    acc = _dot(u, w_ref[:, DA_QK_WIDTH:2 * DA_QK_WIDTH])
    gk = gk_ref[...]
    for h in range(DA_HEADS):
        k_ref[0, h] = norm_rope(acc[:, h * LANES:(h + 1) * LANES], gk).astype(BF16)
    acc = _dot(u, w_ref[:, 2 * DA_QK_WIDTH:2 * DA_QK_WIDTH + DA_WIDTH])
    tm = u.shape[0]
    ones_row = (lax.broadcasted_iota(jnp.int32, (ATT_VROWS - DA_V_DIM, tm), 0) == 0).astype(BF16)
    for h in range(DA_HEADS):
        vt_ref[0, h, 0, 0:DA_V_DIM, :] = acc[:, h * LANES:(h + 1) * LANES].T.astype(BF16)
        vt_ref[0, h, 0, DA_V_DIM:ATT_VROWS, :] = ones_row


def _rope_lane_tables():
    lane = np.arange(LANES) % DA_HEAD_DIM
    inv_freq = ROPE_THETA ** (-jnp.arange(0, ROPE_DIM, 2, dtype=F32) / ROPE_DIM)
    half = ROPE_DIM // 2
    freq = jnp.where(lane < ROPE_DIM, inv_freq[lane % half], 0.0).astype(F32).reshape(1, LANES)
    sign_a = np.where(lane < half, -1.0, 0.0).astype(np.float32).reshape(1, LANES)
    mask_b = np.where((lane >= half) & (lane < ROPE_DIM), 1.0, 0.0).astype(np.float32).reshape(1, LANES)
    group = (np.arange(LANES)[:, None] // DA_HEAD_DIM == np.arange(LANES)[None, :] // DA_HEAD_DIM)
    return freq, jnp.asarray(sign_a), jnp.asarray(mask_b), jnp.asarray(group, dtype=BF16)


def _qkv(u, w_qkv, positions, q_gain, k_gain, bsz, seq):
    t, d = u.shape
    tm = min(PROJ_TM, seq)
    tpb = seq // tm
    freq, sign_a, mask_b, group = _rope_lane_tables()
    lane_vec = lambda v: jnp.tile(v.astype(F32), LANES // DA_HEAD_DIM).reshape(1, LANES)
    vec_spec = pl.BlockSpec((1, LANES), lambda i: (0, 0))
    assert tm == ATT_TK and seq % ATT_TQ == 0
    per_q = ATT_TQ // tm
    qt_shape = jax.ShapeDtypeStruct((bsz, DA_HEADS, seq // ATT_TQ, LANES, ATT_TQ), BF16)
    qt_spec = pl.BlockSpec((1, DA_HEADS, 1, LANES, tm),
                           lambda i: (i // tpb, 0, (i % tpb) // per_q, 0, (i % tpb) % per_q))
    vt_shape = jax.ShapeDtypeStruct((bsz, DA_HEADS, tpb, ATT_VROWS, tm), BF16)
    vt_spec = pl.BlockSpec((1, DA_HEADS, 1, ATT_VROWS, tm), lambda i: (i // tpb, 0, i % tpb, 0, 0))
    return pl.pallas_call(
        _qkv_kernel,
        grid=(t // tm,),
        in_specs=[pl.BlockSpec((tm, d), lambda i: (i, 0)),
                  _resident(w_qkv.shape),
                  pl.BlockSpec((tm, 1), lambda i: (i, 0)),
                  vec_spec, vec_spec, vec_spec, vec_spec, vec_spec,
                  pl.BlockSpec((LANES, LANES), lambda i: (0, 0))],
        out_specs=[qt_spec,
                   pl.BlockSpec((1, DA_HEADS, tm, LANES), lambda i: (i // tpb, 0, i % tpb, 0)),
                   vt_spec],
        out_shape=[qt_shape, jax.ShapeDtypeStruct((bsz, DA_HEADS, seq, LANES), BF16), vt_shape],
        compiler_params=_params("parallel"),
        name="qkv",
    )(u, w_qkv, positions.reshape(t, 1), freq, sign_a, mask_b, lane_vec(q_gain), lane_vec(k_gain), group)


def _glaqk_kernel(u_ref, w_ref, up_ref, bias_ref, q_ref, k_ref, la_ref):
    acc = _dot(u_ref[...], w_ref[...])
    q_ref[...] = (acc[:, 0:GLA_QK_WIDTH] * (GLA_KEY_DIM ** -0.5)).astype(BF16)
    k_ref[...] = acc[:, GLA_QK_WIDTH:2 * GLA_QK_WIDTH].astype(BF16)
    low = acc[:, 2 * GLA_QK_WIDTH:]
    low_hi, low_lo = _split_bf16(low)
    up_hi, up_lo = _split_bf16(up_ref[...])
    z = _dot(low_hi, up_hi) + _dot(low_lo, up_hi) + _dot(low_hi, up_lo) + bias_ref[...]
    log_sig = jnp.minimum(z, 0.0) - jnp.log1p(jnp.exp(-jnp.abs(z)))
    la_ref[...] = log_sig * (1.0 / GLA_GATE_TAU)


def _glaqk(u, w_g, gate_up, gate_bias, seq):
    t, d = u.shape
    tm = min(PROJ_TM, seq)
    up = jnp.zeros((LANES, GLA_QK_WIDTH), F32).at[:GLA_GATE_RANK].set(gate_up)
    row_spec = lambda w: pl.BlockSpec((tm, w), lambda i: (i, 0))
    return pl.pallas_call(
        _glaqk_kernel,
        grid=(t // tm,),
        in_specs=[row_spec(d), _resident(w_g.shape), _resident(up.shape),
                  pl.BlockSpec((1, GLA_QK_WIDTH), lambda i: (0, 0))],
        out_specs=[row_spec(GLA_QK_WIDTH), row_spec(GLA_QK_WIDTH), row_spec(GLA_QK_WIDTH)],
        out_shape=[jax.ShapeDtypeStruct((t, GLA_QK_WIDTH), BF16),
                   jax.ShapeDtypeStruct((t, GLA_QK_WIDTH), BF16),
                   jax.ShapeDtypeStruct((t, GLA_QK_WIDTH), F32)],
        compiler_params=_params("parallel"),
        name="glaqk",
    )(u, w_g, up, gate_bias.reshape(1, GLA_QK_WIDTH))


ACT_SILU_BLOCKS = (GLA_WIDTH // ACT_TN, 2 * GLA_WIDTH // ACT_TN)


def _act_kernel(u_ref, w_ref, o_ref):
    j = pl.program_id(0)
    acc = _dot(u_ref[...], w_ref[...])
    sig = jax.nn.sigmoid(acc)
    out = jnp.where(j < ACT_SILU_BLOCKS[0], acc, jnp.where(j < ACT_SILU_BLOCKS[1], acc * sig, sig))
    o_ref[...] = out.astype(BF16)


def _act(u, w_act, seq):
    t, d = u.shape
    n = w_act.shape[1]
    tm = min(PROJ_TM, seq)
    return pl.pallas_call(
        _act_kernel,
        grid=(n // ACT_TN, t // tm),
        in_specs=[pl.BlockSpec((tm, d), lambda j, i: (i, 0)),
                  pl.BlockSpec((d, ACT_TN), lambda j, i: (0, j))],
        out_specs=pl.BlockSpec((tm, ACT_TN), lambda j, i: (i, j)),
        out_shape=jax.ShapeDtypeStruct((t, n), BF16),
        compiler_params=_params("arbitrary", "arbitrary"),
        name="act",
    )(u, w_act)


_FULL, _MASK, _SKIP = "full", "mask", "skip"


def _attn_kernel(qt_ref, k_ref, vt_ref, lq1_ref, lk1_ref, lq2_ref, lk2_ref, subg_ref, o_ref,
                 qc_ref, s_ref, cm_ref, acc_ref, m_ref, *, lam_init):
    i = pl.program_id(2)
    tq, tk, qc_w = ATT_TQ, ATT_TK, ATT_QC
    nqc = tq // qc_w
    qt = qt_ref[0, 0, 0]
    row = lax.broadcasted_iota(jnp.int32, qt.shape, 0)
    zero = jnp.zeros_like(qt)
    qc_ref[0] = jnp.where(row < DA_HEAD_DIM, qt, zero)
    qc_ref[1] = jnp.where(row >= DA_HEAD_DIM, qt, zero)
    acc_ref[...] = jnp.zeros_like(acc_ref)
    m_ref[...] = jnp.full_like(m_ref, -jnp.inf)

    def scores(t, slot, modes):
        kb = k_ref[0, 0, pl.ds(pl.multiple_of(t * tk, tk), tk), :]
        for c in range(2):
            for q in range(nqc):
                if modes[q] == _SKIP:
                    continue
                cols = slice(q * qc_w, (q + 1) * qc_w)
                s = _dot(kb, qc_ref[c, :, cols])
                if modes[q] == _MASK:
                    rel = (lax.broadcasted_iota(jnp.int32, (tk, qc_w), 0)
                           - lax.broadcasted_iota(jnp.int32, (tk, qc_w), 1))
                    s = jnp.where(rel <= i * tq + q * qc_w - t * tk, s, -jnp.inf)
                s_ref[slot, c, :, cols] = s
                cm_ref[slot, c, :, cols] = jnp.max(s, axis=0, keepdims=True)

    def accumulate(t, slot, modes):
        vtb = vt_ref[0, 0, t]
        for c in range(2):
            for q in range(nqc):
                if modes[q] == _SKIP:
                    continue
                cols = slice(q * qc_w, (q + 1) * qc_w)
                m_old = m_ref[c, :, cols]
                m_new = jnp.maximum(m_old, cm_ref[slot, c, :, cols])
                alpha = jnp.exp2(m_old - m_new)
                p = jnp.exp2(s_ref[slot, c, :, cols] - m_new)
                acc_ref[c, :, cols] = alpha * acc_ref[c, :, cols] + _dot(vtb, p.astype(BF16))
                m_ref[c, :, cols] = m_new

    half = tk // qc_w
    even_modes = (_MASK,) * half + (_FULL,) * (nqc - half)
    full_modes = (_FULL,) * nqc
    last_modes = (_SKIP,) * half + (_MASK,) * (nqc - half)

    scores(0, 0, even_modes)

    def pair(p, carry):
        scores(2 * p + 1, 1, full_modes)
        accumulate(2 * p, 0, full_modes)
        scores(2 * p + 2, 0, even_modes)
        accumulate(2 * p + 1, 1, full_modes)
        return carry

    lax.fori_loop(0, i, pair, 0)
    scores(2 * i + 1, 1, last_modes)
    accumulate(2 * i, 0, full_modes)
    accumulate(2 * i + 1, 1, last_modes)

    s1 = jnp.sum(lq1_ref[...] * lk1_ref[...], axis=1, keepdims=True)
    s2 = jnp.sum(lq2_ref[...] * lk2_ref[...], axis=1, keepdims=True)
    lam = jnp.exp(s1) - jnp.exp(s2) + lam_init
    num0, den0 = acc_ref[0, 0:DA_V_DIM, :], acc_ref[0, DA_V_DIM:DA_V_DIM + 1, :]
    num1, den1 = acc_ref[1, 0:DA_V_DIM, :], acc_ref[1, DA_V_DIM:DA_V_DIM + 1, :]
    o = num0 / den0 - lam * (num1 / den1)
    ms = jnp.mean(o * o, axis=0, keepdims=True)
    y = o * lax.rsqrt(ms + EPS) * subg_ref[...] * (1.0 - lam_init)
    o_ref[...] = y.T.astype(BF16)


def _attention(qt, k, vt, lq1, lk1, lq2, lk2, subln_g, layer_idx):
    bsz, heads, nq, _, tq = qt.shape
    nkb, vrows, tk = vt.shape[2:]
    seq = nq * tq
    assert (tq, tk, vrows) == (ATT_TQ, ATT_TK, ATT_VROWS) and tq == 2 * tk and nkb * tk == seq
    lam_init = 0.8 - 0.6 * math.exp(-0.3 * layer_idx)
    vec = lambda v: v.astype(F32).reshape(1, DA_HEAD_DIM)
    vec_spec = pl.BlockSpec((1, DA_HEAD_DIM), lambda b, h, i: (0, 0))
    return pl.pallas_call(
        functools.partial(_attn_kernel, lam_init=lam_init),
        grid=(bsz, heads, nq),
        in_specs=[pl.BlockSpec((1, 1, 1, LANES, tq), lambda b, h, i: (b, h, i, 0, 0)),
                  pl.BlockSpec((1, 1, seq, LANES), lambda b, h, i: (b, h, 0, 0)),
                  pl.BlockSpec((1, 1, nkb, vrows, tk), lambda b, h, i: (b, h, 0, 0, 0)),
                  vec_spec, vec_spec, vec_spec, vec_spec,
                  pl.BlockSpec((DA_V_DIM, 1), lambda b, h, i: (0, 0))],
        out_specs=pl.BlockSpec((tq, LANES), lambda b, h, i: (b * nq + i, h)),
        out_shape=jax.ShapeDtypeStruct((bsz * seq, DA_WIDTH), BF16),
        scratch_shapes=[pltpu.VMEM((2, LANES, tq), BF16),
                        pltpu.VMEM((2, 2, tk, tq), F32),
                        pltpu.VMEM((2, 2, 1, tq), F32),
                        pltpu.VMEM((2, vrows, tq), F32),
                        pltpu.VMEM((2, 1, tq), F32)],
        compiler_params=_params("parallel", "parallel", "arbitrary"),
        name="attn",
    )(qt, k, vt, vec(lq1), vec(lk1), vec(lq2), vec(lk2), subln_g.astype(F32).reshape(DA_V_DIM, 1))


def _gla_decay_matrix():
    c = GLA_CHUNK
    t = np.arange(c)[:, None]
    s = np.arange(c)[None, :]
    blocks = [(s <= t)]
    for h in GLA_LEVELS:
        r = (t // (2 * h)) * (2 * h) + h
        upper = (t >= r) & (s > r) & (s <= t)
        lower = (t < r) & (s > t) & (s <= r)
        blocks.append(upper | lower)
    return jnp.asarray(np.concatenate(blocks, axis=0), dtype=BF16)


def _gla_kernel(q_ref, k_ref, v_ref, la_ref, sr_ref, fm_ref, g_ref, o_ref, st_ref, *, nchunk):
    @pl.when(pl.program_id(1) == 0)
    def _():
        st_ref[...] = jnp.zeros_like(st_ref)

    c = GLA_CHUNK
    ii = lax.broadcasted_iota(jnp.int32, (c, c), 0)
    jj = lax.broadcasted_iota(jnp.int32, (c, c), 1)
    eye = ii == jj
    level_masks = []
    for h in GLA_LEVELS:
        shift = int(math.log2(2 * h))
        same = (ii >> shift) == (jj >> shift)
        level_masks.append(same & ((ii & (2 * h - 1)) >= h) & ((jj & (2 * h - 1)) < h))
    fm = fm_ref[...]
    gain = g_ref[...]

    def chunk(n, carry):
        rows = pl.ds(pl.multiple_of(n * c, c), c)
        la = la_ref[rows, :]
        la_hi, la_lo = _split_bf16(la)
        fa = _dot(fm, la_hi) + _dot(fm, la_lo)
        b = fa[0:c]
        b_last = b[c - 1:c]
        e_b = jnp.exp(b)
        e_rest = jnp.exp(b_last - b)
        e_last = jnp.exp(b_last)
        e_lvl = [jnp.exp(fa[(l + 1) * c:(l + 2) * c]) for l in range(len(GLA_LEVELS))]
        for hd in range(GLA_HEADS):
            ks = slice(hd * GLA_KEY_DIM, (hd + 1) * GLA_KEY_DIM)
            vs = slice(hd * GLA_VAL_DIM, (hd + 1) * GLA_VAL_DIM)
            qb = q_ref[rows, ks]
            kb = k_ref[rows, ks]
            q = qb.astype(F32)
            k = kb.astype(F32)
            v = v_ref[rows, vs]
            state = st_ref[hd]
            out = _dot_nt((q * e_b[:, ks]).astype(BF16), state.astype(BF16))
            att = jnp.where(eye, _dot_nt(qb, kb), 0.0)
            for l in range(len(GLA_LEVELS)):
                e = e_lvl[l][:, ks]
                att = jnp.where(level_masks[l], _dot_nt((q * e).astype(BF16), (k * e).astype(BF16)), att)
            out = out + _dot(att.astype(BF16), v)
            ms = jnp.mean(out * out, axis=-1, keepdims=True)
            y = out * lax.rsqrt(ms + EPS) * gain * sr_ref[rows, vs].astype(F32)
            o_ref[rows, vs] = y.astype(BF16)
            st_ref[hd] = state * e_last[:, ks] + _dot_tn(v, (k * e_rest[:, ks]).astype(BF16))
        return carry

    lax.fori_loop(0, nchunk, chunk, 0)


def _gla(gq, gk, act, la, out_gain, bsz, seq):
    t = gq.shape[0]
    tm = min(GLA_TM, seq)
    tpb = seq // tm
    fm = _gla_decay_matrix()
    row = lambda b, n: (b * tpb + n, 0)
    return pl.pallas_call(
        functools.partial(_gla_kernel, nchunk=tm // GLA_CHUNK),
        grid=(bsz, tpb),
        in_specs=[pl.BlockSpec((tm, GLA_QK_WIDTH), row),
                  pl.BlockSpec((tm, GLA_QK_WIDTH), row),
                  pl.BlockSpec((tm, GLA_WIDTH), row),
                  pl.BlockSpec((tm, GLA_QK_WIDTH), row),
                  pl.BlockSpec((tm, GLA_WIDTH), lambda b, n: (b * tpb + n, 1)),
                  pl.BlockSpec(fm.shape, lambda b, n: (0, 0)),
                  pl.BlockSpec((1, GLA_VAL_DIM), lambda b, n: (0, 0))],
        out_specs=pl.BlockSpec((tm, GLA_WIDTH), row),
        out_shape=jax.ShapeDtypeStruct((t, GLA_WIDTH), BF16),
        scratch_shapes=[pltpu.VMEM((GLA_HEADS, GLA_VAL_DIM, GLA_KEY_DIM), F32)],
        compiler_params=_params("parallel", "arbitrary"),
        name="gla",
    )(gq, gk, act, la, act, fm, out_gain.astype(F32).reshape(1, GLA_VAL_DIM))


def _merge_kernel(ya_ref, yb_ref, sa_ref, sb_ref, x_ref, mod_ref, wa_ref, wb_ref, wo_ref, o_ref):
    ta = _dot(ya_ref[...], wa_ref[...])
    tb = _dot(yb_ref[...], wb_ref[...])
    merged = sa_ref[...].astype(F32) * ta + sb_ref[...].astype(F32) * tb
    delta = _dot(merged.astype(BF16), wo_ref[...])
    o_ref[...] = x_ref[...] + mod_ref[0][2:3] * delta


def _merge(ya, yb, act, x2, mod, w_a, w_b, w_o, seq):
    t, d = x2.shape
    tm = min(MERGE_TM, seq)
    tpb = seq // tm
    gate_blk = 2 * GLA_WIDTH // d
    return pl.pallas_call(
        _merge_kernel,
        grid=(t // tm,),
        in_specs=[pl.BlockSpec((tm, DA_WIDTH), lambda i: (i, 0)),
                  pl.BlockSpec((tm, GLA_WIDTH), lambda i: (i, 0)),
                  pl.BlockSpec((tm, d), lambda i: (i, gate_blk)),
                  pl.BlockSpec((tm, d), lambda i: (i, gate_blk + 1)),
                  pl.BlockSpec((tm, d), lambda i: (i, 0)),
                  pl.BlockSpec((1, N_MOD, d), lambda i: (i // tpb, 0, 0)),
                  _resident(w_a.shape), _resident(w_b.shape), _resident(w_o.shape)],
        out_specs=pl.BlockSpec((tm, d), lambda i: (i, 0)),
        out_shape=jax.ShapeDtypeStruct((t, d), F32),
        compiler_params=_params("parallel"),
        name="merge",
    )(ya, yb, act, act, x2, mod, w_a, w_b, w_o)


def _mlp_kernel(h_ref, g_ref, mod_ref, w1_ref, w2_ref, o_ref, u_ref, acc_ref):
    kf = pl.program_id(1)

    @pl.when(kf == 0)
    def _():
        u_ref[...] = _modulated_norm(h_ref[...], g_ref[...], mod_ref[0], 3, 4).astype(BF16)
        acc_ref[...] = jnp.zeros_like(acc_ref)

    hid = jnp.square(jnp.maximum(_dot(u_ref[...], w1_ref[...]), 0.0))
    acc_ref[...] += _dot(hid.astype(BF16), w2_ref[...])

    @pl.when(kf == pl.num_programs(1) - 1)
    def _():
        o_ref[...] = h_ref[...] + mod_ref[0][5:6] * acc_ref[...]


def _mlp(h, gain, mod, w1, w2, seq):
    t, d = h.shape
    tm = min(MLP_TM, seq)
    tpb = seq // tm
    ff = w1.shape[1]
    return pl.pallas_call(
        _mlp_kernel,
        grid=(t // tm, ff // MLP_TF),
        in_specs=[pl.BlockSpec((tm, d), lambda i, kf: (i, 0)),
                  pl.BlockSpec((1, d), lambda i, kf: (0, 0)),
                  pl.BlockSpec((1, N_MOD, d), lambda i, kf: (i // tpb, 0, 0)),
                  pl.BlockSpec((d, MLP_TF), lambda i, kf: (0, kf)),
                  pl.BlockSpec((MLP_TF, d), lambda i, kf: (kf, 0))],
        out_specs=pl.BlockSpec((tm, d), lambda i, kf: (i, 0)),
        out_shape=jax.ShapeDtypeStruct((t, d), F32),
        scratch_shapes=[pltpu.VMEM((tm, d), BF16), pltpu.VMEM((tm, d), F32)],
        compiler_params=_params("parallel", "arbitrary"),
        name="mlp",
    )(h, gain.reshape(1, d), mod, w1, w2)


def _layer(h2, c, positions, bsz, seq, layer_idx, w_ada, b_ada, norm1_g, w_in, da_q_norm_g, da_k_norm_g,
           lq1, lk1, lq2, lk2, da_subln_g, gla_gate_up, gla_gate_bias, gla_out_norm_g,
           w_branch_a, w_branch_b, w_out, norm2_g, w_mlp_in, w_mlp_out):
    d = h2.shape[1]
    assert seq % min(PROJ_TM, seq) == 0 and seq % GLA_CHUNK == 0
    offs = np.concatenate([[0], np.cumsum(IN_SIZES)])
    col = lambda a, b: w_in[:, offs[a]:offs[b]].astype(BF16)
    w_qkv = col(0, 3)
    w_g = jnp.concatenate([col(3, 5), col(7, 8), jnp.zeros((d, LANES - GLA_GATE_RANK), BF16)], axis=1)
    w_act = jnp.concatenate([col(5, 7), col(8, 10)], axis=1)

    mod = _ada(c, w_ada, b_ada)
    u = _normmod(h2, norm1_g, mod, seq)
    qt, k, vt = _qkv(u, w_qkv, positions, da_q_norm_g, da_k_norm_g, bsz, seq)
    gq, gk, la = _glaqk(u, w_g, gla_gate_up, gla_gate_bias, seq)
    act = _act(u, w_act, seq)
    ya = _attention(qt, k, vt, lq1, lk1, lq2, lk2, da_subln_g, layer_idx)
    yb = _gla(gq, gk, act, la, gla_out_norm_g, bsz, seq)
    h1 = _merge(ya, yb, act, h2, mod, w_branch_a.astype(BF16), w_branch_b.astype(BF16),
                w_out.astype(BF16), seq)
    return _mlp(h1, norm2_g, mod, w_mlp_in.astype(BF16), w_mlp_out.astype(BF16), seq)


def kernel(x, c, positions, w_ada, b_ada, norm1_g, w_in, da_q_norm_g, da_k_norm_g, da_lambda_q1,
           da_lambda_k1, da_lambda_q2, da_lambda_k2, da_subln_g, gla_gate_up, gla_gate_bias,
           gla_out_norm_g, w_branch_a, w_branch_b, w_out, norm2_g, w_mlp_in, w_mlp_out):
    bsz, seq, d = x.shape
    h = x.reshape(bsz * seq, d)
    for l in range(w_ada.shape[0]):
        h = _layer(h, c, positions, bsz, seq, l, w_ada[l], b_ada[l], norm1_g[l], w_in[l], da_q_norm_g[l],
                   da_k_norm_g[l], da_lambda_q1[l], da_lambda_k1[l], da_lambda_q2[l], da_lambda_k2[l],
                   da_subln_g[l], gla_gate_up[l], gla_gate_bias[l], gla_out_norm_g[l], w_branch_a[l],
                   w_branch_b[l], w_out[l], norm2_g[l], w_mlp_in[l], w_mlp_out[l])
    return h.reshape(bsz, seq, d)
```

```python
import functools
import math

import jax
import jax.numpy as jnp
import numpy as np
from jax import lax
from jax.experimental import pallas as pl
from jax.experimental.pallas import tpu as pltpu

F32 = jnp.float32
BF16 = jnp.bfloat16

D_MODEL = 2048
DA_HEADS = 8
DA_HEAD_DIM = 64
DA_V_DIM = 2 * DA_HEAD_DIM
DA_QK_WIDTH = DA_HEADS * 2 * DA_HEAD_DIM
DA_WIDTH = DA_HEADS * DA_V_DIM
ROPE_THETA = 500000.0
ROPE_DIM = DA_HEAD_DIM // 4
GLA_HEADS = 4
GLA_KEY_DIM = 128
GLA_VAL_DIM = 256
GLA_QK_WIDTH = GLA_HEADS * GLA_KEY_DIM
GLA_WIDTH = GLA_HEADS * GLA_VAL_DIM
GLA_GATE_RANK = 16
GLA_GATE_TAU = 16.0
GLA_CHUNK = 64
D_FF = 4 * D_MODEL
N_MOD = 6
EPS = 1e-6
IN_SIZES = (DA_QK_WIDTH, DA_QK_WIDTH, DA_WIDTH, GLA_QK_WIDTH, GLA_QK_WIDTH, GLA_WIDTH, GLA_WIDTH,
            GLA_GATE_RANK, D_MODEL, D_MODEL)

LANES = 128
SUBLANES = 8
VMEM_LIMIT_BYTES = 56 * 1024 * 1024

ADA_TN = 1024
NORM_TM = 512
PROJ_TM = 512
ACT_TM = 1024
ACT_TN = 1024
ATT_TK = PROJ_TM
ATT_TQ = 2 * ATT_TK
ATT_QC = 256
ATT_VROWS = DA_V_DIM + 16
GLA_TM = 512
MERGE_TM = 256
MLP_TM = 512
MLP_TF = 1024

GLA_LEVELS = (32, 16, 8, 4, 2, 1)


def _dot(a, b):
    return jnp.dot(a, b, preferred_element_type=F32)


def _dot_nt(a, b):
    return lax.dot_general(a, b, (((1,), (1,)), ((), ())), preferred_element_type=F32)


def _dot_tn(a, b):
    return lax.dot_general(a, b, (((0,), (0,)), ((), ())), preferred_element_type=F32)


def _split_bf16(x):
    hi = x.astype(BF16)
    lo = (x - hi.astype(F32)).astype(BF16)
    return hi, lo


def _params(*semantics):
    return pltpu.CompilerParams(dimension_semantics=semantics, vmem_limit_bytes=VMEM_LIMIT_BYTES)


def _resident(shape):
    nd = len(shape)
    return pl.BlockSpec(shape, lambda *_: (0,) * nd, pipeline_mode=pl.Buffered(1))


def _ada_kernel(c_ref, w_ref, b_ref, o_ref):
    c = c_ref[...]
    sc = c * jax.nn.sigmoid(c)
    o_ref[...] = _dot(sc.astype(BF16), w_ref[...].astype(BF16)) + b_ref[...]


def _ada(c, w_ada, b_ada):
    bsz, d = c.shape
    n = w_ada.shape[1]
    c8 = jnp.zeros((SUBLANES, d), F32).at[:bsz].set(c)
    out = pl.pallas_call(
        _ada_kernel,
        grid=(n // ADA_TN,),
        in_specs=[pl.BlockSpec((SUBLANES, d), lambda j: (0, 0)),
                  pl.BlockSpec((d, ADA_TN), lambda j: (0, j)),
                  pl.BlockSpec((1, ADA_TN), lambda j: (0, j))],
        out_specs=pl.BlockSpec((SUBLANES, ADA_TN), lambda j: (0, j)),
        out_shape=jax.ShapeDtypeStruct((SUBLANES, n), F32),
        compiler_params=_params("arbitrary"),
        name="ada",
    )(c8, w_ada, b_ada.reshape(1, n))
    return out[:bsz].reshape(bsz, N_MOD, d)


def _modulated_norm(x, gain, mod, shift_idx, scale_idx):
    ms = jnp.mean(x * x, axis=-1, keepdims=True)
    y = x * lax.rsqrt(ms + EPS) * gain
    return y * (1.0 + mod[scale_idx:scale_idx + 1]) + mod[shift_idx:shift_idx + 1]


def _normmod_kernel(x_ref, g_ref, mod_ref, o_ref):
    o_ref[...] = _modulated_norm(x_ref[...], g_ref[...], mod_ref[0], 0, 1).astype(BF16)


def _normmod(x2, gain, mod, seq):
    t, d = x2.shape
    tm = min(NORM_TM, seq)
    tpb = seq // tm
    return pl.pallas_call(
        _normmod_kernel,
        grid=(t // tm,),
        in_specs=[pl.BlockSpec((tm, d), lambda i: (i, 0)),
                  pl.BlockSpec((1, d), lambda i: (0, 0)),
                  pl.BlockSpec((1, N_MOD, d), lambda i: (i // tpb, 0, 0))],
        out_specs=pl.BlockSpec((tm, d), lambda i: (i, 0)),
        out_shape=jax.ShapeDtypeStruct((t, d), BF16),
        compiler_params=_params("parallel"),
        name="normmod",
    )(x2, gain.reshape(1, d), mod)


def _qkv_kernel(u_ref, w_ref, pos_ref, freq_ref, sa_ref, sb_ref, gq_ref, gk_ref, grp_ref,
                qt_ref, k_ref, vt_ref):
    u = u_ref[...]
    ang = pos_ref[...].astype(F32) * freq_ref[...]
    cs = jnp.cos(ang)
    sn = jnp.sin(ang)
    sin_a = sn * sa_ref[...]
    sin_b = sn * sb_ref[...]
    grp = grp_ref[...]

    def norm_rope(x, gain):
        hi, lo = _split_bf16(x * x)
        ss = _dot(hi, grp) + _dot(lo, grp)
        xn = x * lax.rsqrt(ss * (1.0 / DA_HEAD_DIM) + EPS) * gain
        half = ROPE_DIM // 2
        return (xn * cs + pltpu.roll(xn, LANES - half, 1) * sin_a + pltpu.roll(xn, half, 1) * sin_b)

    q_scale = (DA_HEAD_DIM ** -0.5) * math.log2(math.e)
    acc = _dot(u, w_ref[:, 0:DA_QK_WIDTH])
    gq = gq_ref[...]
    for h in range(DA_HEADS):
        xq = norm_rope(acc[:, h * LANES:(h + 1) * LANES], gq) * q_scale
        qt_ref[0, h, 0] = xq.T.astype(BF16)
    acc = _dot(u, w_ref[:, DA_QK_WIDTH:2 * DA_QK_WIDTH])
    gk = gk_ref[...]
    for h in range(DA_HEADS):
        k_ref[0, h] = norm_rope(acc[:, h * LANES:(h + 1) * LANES], gk).astype(BF16)
    acc = _dot(u, w_ref[:, 2 * DA_QK_WIDTH:2 * DA_QK_WIDTH + DA_WIDTH])
    tm = u.shape[0]
    ones_row = (lax.broadcasted_iota(jnp.int32, (ATT_VROWS - DA_V_DIM, tm), 0) == 0).astype(BF16)
    for h in range(DA_HEADS):
        vt_ref[0, h, 0, 0:DA_V_DIM, :] = acc[:, h * LANES:(h + 1) * LANES].T.astype(BF16)
        vt_ref[0, h, 0, DA_V_DIM:ATT_VROWS, :] = ones_row


def _rope_lane_tables():
    lane = np.arange(LANES) % DA_HEAD_DIM
    inv_freq = ROPE_THETA ** (-jnp.arange(0, ROPE_DIM, 2, dtype=F32) / ROPE_DIM)
    half = ROPE_DIM // 2
    freq = jnp.where(lane < ROPE_DIM, inv_freq[lane % half], 0.0).astype(F32).reshape(1, LANES)
    sign_a = np.where(lane < half, -1.0, 0.0).astype(np.float32).reshape(1, LANES)
    mask_b = np.where((lane >= half) & (lane < ROPE_DIM), 1.0, 0.0).astype(np.float32).reshape(1, LANES)
    group = (np.arange(LANES)[:, None] // DA_HEAD_DIM == np.arange(LANES)[None, :] // DA_HEAD_DIM)
    return freq, jnp.asarray(sign_a), jnp.asarray(mask_b), jnp.asarray(group, dtype=BF16)


def _qkv(u, w_qkv, positions, q_gain, k_gain, bsz, seq):
    t, d = u.shape
    tm = min(PROJ_TM, seq)
    tpb = seq // tm
    freq, sign_a, mask_b, group = _rope_lane_tables()
    lane_vec = lambda v: jnp.tile(v.astype(F32), LANES // DA_HEAD_DIM).reshape(1, LANES)
    vec_spec = pl.BlockSpec((1, LANES), lambda i: (0, 0))
    assert tm == ATT_TK and seq % ATT_TQ == 0
    per_q = ATT_TQ // tm
    qt_shape = jax.ShapeDtypeStruct((bsz, DA_HEADS, seq // ATT_TQ, LANES, ATT_TQ), BF16)
    qt_spec = pl.BlockSpec((1, DA_HEADS, 1, LANES, tm),
                           lambda i: (i // tpb, 0, (i % tpb) // per_q, 0, (i % tpb) % per_q))
    vt_shape = jax.ShapeDtypeStruct((bsz, DA_HEADS, tpb, ATT_VROWS, tm), BF16)
    vt_spec = pl.BlockSpec((1, DA_HEADS, 1, ATT_VROWS, tm), lambda i: (i // tpb, 0, i % tpb, 0, 0))
    return pl.pallas_call(
        _qkv_kernel,
        grid=(t // tm,),
        in_specs=[pl.BlockSpec((tm, d), lambda i: (i, 0)),
                  _resident(w_qkv.shape),
                  pl.BlockSpec((tm, 1), lambda i: (i, 0)),
                  vec_spec, vec_spec, vec_spec, vec_spec, vec_spec,
                  pl.BlockSpec((LANES, LANES), lambda i: (0, 0))],
        out_specs=[qt_spec,
                   pl.BlockSpec((1, DA_HEADS, tm, LANES), lambda i: (i // tpb, 0, i % tpb, 0)),
                   vt_spec],
        out_shape=[qt_shape, jax.ShapeDtypeStruct((bsz, DA_HEADS, seq, LANES), BF16), vt_shape],
        compiler_params=_params("parallel"),
        name="qkv",
    )(u, w_qkv, positions.reshape(t, 1), freq, sign_a, mask_b, lane_vec(q_gain), lane_vec(k_gain), group)


def _glaqk_kernel(u_ref, w_ref, up_ref, bias_ref, q_ref, k_ref, la_ref):
    acc = _dot(u_ref[...], w_ref[...])
    q_ref[...] = (acc[:, 0:GLA_QK_WIDTH] * (GLA_KEY_DIM ** -0.5)).astype(BF16)
    k_ref[...] = acc[:, GLA_QK_WIDTH:2 * GLA_QK_WIDTH].astype(BF16)
    low = acc[:, 2 * GLA_QK_WIDTH:]
    low_hi, low_lo = _split_bf16(low)
    up_hi, up_lo = _split_bf16(up_ref[...])
    z = _dot(low_hi, up_hi) + _dot(low_lo, up_hi) + _dot(low_hi, up_lo) + bias_ref[...]
    log_sig = jnp.minimum(z, 0.0) - jnp.log1p(jnp.exp(-jnp.abs(z)))
    la_ref[...] = log_sig * (1.0 / GLA_GATE_TAU)


def _glaqk(u, w_g, gate_up, gate_bias, seq):
    t, d = u.shape
    tm = min(PROJ_TM, seq)
    up = jnp.zeros((LANES, GLA_QK_WIDTH), F32).at[:GLA_GATE_RANK].set(gate_up)
    row_spec = lambda w: pl.BlockSpec((tm, w), lambda i: (i, 0))
    return pl.pallas_call(
        _glaqk_kernel,
        grid=(t // tm,),
        in_specs=[row_spec(d), _resident(w_g.shape), _resident(up.shape),
                  pl.BlockSpec((1, GLA_QK_WIDTH), lambda i: (0, 0))],
        out_specs=[row_spec(GLA_QK_WIDTH), row_spec(GLA_QK_WIDTH), row_spec(GLA_QK_WIDTH)],
        out_shape=[jax.ShapeDtypeStruct((t, GLA_QK_WIDTH), BF16),
                   jax.ShapeDtypeStruct((t, GLA_QK_WIDTH), BF16),
                   jax.ShapeDtypeStruct((t, GLA_QK_WIDTH), F32)],
        compiler_params=_params("parallel"),
        name="glaqk",
    )(u, w_g, up, gate_bias.reshape(1, GLA_QK_WIDTH))


ACT_SILU_BLOCKS = (GLA_WIDTH // ACT_TN, 2 * GLA_WIDTH // ACT_TN)


def _act_kernel(u_ref, w_ref, o_ref):
    j = pl.program_id(0)
    acc = _dot(u_ref[...], w_ref[...])
    sig = jax.nn.sigmoid(acc)
    out = jnp.where(j < ACT_SILU_BLOCKS[0], acc, jnp.where(j < ACT_SILU_BLOCKS[1], acc * sig, sig))
    o_ref[...] = out.astype(BF16)


def _act(u, w_act, seq):
    t, d = u.shape
    n = w_act.shape[1]
    tm = min(ACT_TM, seq)
    return pl.pallas_call(
        _act_kernel,
        grid=(n // ACT_TN, t // tm),
        in_specs=[pl.BlockSpec((tm, d), lambda j, i: (i, 0)),
                  pl.BlockSpec((d, ACT_TN), lambda j, i: (0, j))],
        out_specs=pl.BlockSpec((tm, ACT_TN), lambda j, i: (i, j)),
        out_shape=jax.ShapeDtypeStruct((t, n), BF16),
        compiler_params=_params("arbitrary", "arbitrary"),
        name="act",
    )(u, w_act)


_FULL, _MASK, _SKIP = "full", "mask", "skip"


def _attn_kernel(bounded_ref, qt_ref, k_ref, vt_ref, lq1_ref, lk1_ref, lq2_ref, lk2_ref, subg_ref, o_ref,
                 qc_ref, s_ref, p_ref, cm_ref, acc_ref, m_ref, *, lam_init):
    i = pl.program_id(2)
    tq, tk, qc_w = ATT_TQ, ATT_TK, ATT_QC
    nqc = tq // qc_w
    qt = qt_ref[0, 0, 0]
    row = lax.broadcasted_iota(jnp.int32, qt.shape, 0)
    zero = jnp.zeros_like(qt)
    qc_ref[0] = jnp.where(row < DA_HEAD_DIM, qt, zero)
    qc_ref[1] = jnp.where(row >= DA_HEAD_DIM, qt, zero)
    acc_ref[...] = jnp.zeros_like(acc_ref)
    m_ref[...] = jnp.full_like(m_ref, -jnp.inf)

    def scores(t, slot, modes):
        kb = k_ref[0, 0, pl.ds(pl.multiple_of(t * tk, tk), tk), :]
        for c in range(2):
            for q in range(nqc):
                if modes[q] == _SKIP:
                    continue
                cols = slice(q * qc_w, (q + 1) * qc_w)
                s = _dot(kb, qc_ref[c, :, cols])
                if modes[q] == _MASK:
                    rel = (lax.broadcasted_iota(jnp.int32, (tk, qc_w), 0)
                           - lax.broadcasted_iota(jnp.int32, (tk, qc_w), 1))
                    s = jnp.where(rel <= i * tq + q * qc_w - t * tk, s, -jnp.inf)
                s_ref[slot, c, :, cols] = s
                cm_ref[slot, c, :, cols] = jnp.max(s, axis=0, keepdims=True)

    def accumulate(t, slot, modes):
        vtb = vt_ref[0, 0, t]
        for c in range(2):
            for q in range(nqc):
                if modes[q] == _SKIP:
                    continue
                cols = slice(q * qc_w, (q + 1) * qc_w)
                m_old = m_ref[c, :, cols]
                m_new = jnp.maximum(m_old, cm_ref[slot, c, :, cols])
                alpha = jnp.exp2(m_old - m_new)
                p = jnp.exp2(s_ref[slot, c, :, cols] - m_new)
                acc_ref[c, :, cols] = alpha * acc_ref[c, :, cols] + _dot(vtb, p.astype(BF16))
                m_ref[c, :, cols] = m_new

    half = tk // qc_w
    even_modes = (_MASK,) * half + (_FULL,) * (nqc - half)
    full_modes = (_FULL,) * nqc
    last_modes = (_SKIP,) * half + (_MASK,) * (nqc - half)

    def probs(t, slot, modes):
        kb = k_ref[0, 0, pl.ds(pl.multiple_of(t * tk, tk), tk), :]
        for c in range(2):
            for q in range(nqc):
                if modes[q] == _SKIP:
                    continue
                cols = slice(q * qc_w, (q + 1) * qc_w)
                s = _dot(kb, qc_ref[c, :, cols])
                if modes[q] == _MASK:
                    rel = (lax.broadcasted_iota(jnp.int32, (tk, qc_w), 0)
                           - lax.broadcasted_iota(jnp.int32, (tk, qc_w), 1))
                    s = jnp.where(rel <= i * tq + q * qc_w - t * tk, s, -jnp.inf)
                p_ref[slot, c, :, cols] = jnp.exp2(s).astype(BF16)

    def weigh(t, slot, modes):
        vtb = vt_ref[0, 0, t]
        for c in range(2):
            for q in range(nqc):
                if modes[q] == _SKIP:
                    continue
                cols = slice(q * qc_w, (q + 1) * qc_w)
                acc_ref[c, :, cols] += _dot(vtb, p_ref[slot, c, :, cols])

    @pl.when(bounded_ref[0] != 0)
    def _():
        probs(0, 0, even_modes)

        def pair(p, carry):
            probs(2 * p + 1, 1, full_modes)
            weigh(2 * p, 0, full_modes)
            probs(2 * p + 2, 0, even_modes)
            weigh(2 * p + 1, 1, full_modes)
            return carry

        lax.fori_loop(0, i, pair, 0)
        probs(2 * i + 1, 1, last_modes)
        weigh(2 * i, 0, full_modes)
        weigh(2 * i + 1, 1, last_modes)

    @pl.when(bounded_ref[0] == 0)
    def _():
        scores(0, 0, even_modes)

        def pair(p, carry):
            scores(2 * p + 1, 1, full_modes)
            accumulate(2 * p, 0, full_modes)
            scores(2 * p + 2, 0, even_modes)
            accumulate(2 * p + 1, 1, full_modes)
            return carry

        lax.fori_loop(0, i, pair, 0)
        scores(2 * i + 1, 1, last_modes)
        accumulate(2 * i, 0, full_modes)
        accumulate(2 * i + 1, 1, last_modes)

    s1 = jnp.sum(lq1_ref[...] * lk1_ref[...], axis=1, keepdims=True)
    s2 = jnp.sum(lq2_ref[...] * lk2_ref[...], axis=1, keepdims=True)
    lam = jnp.exp(s1) - jnp.exp(s2) + lam_init
    num0, den0 = acc_ref[0, 0:DA_V_DIM, :], acc_ref[0, DA_V_DIM:DA_V_DIM + 1, :]
    num1, den1 = acc_ref[1, 0:DA_V_DIM, :], acc_ref[1, DA_V_DIM:DA_V_DIM + 1, :]
    o = num0 / den0 - lam * (num1 / den1)
    ms = jnp.mean(o * o, axis=0, keepdims=True)
    y = o * lax.rsqrt(ms + EPS) * subg_ref[...] * (1.0 - lam_init)
    o_ref[...] = y.T.astype(BF16)


ATT_SAFE_LOG2 = 60.0


def _scores_bounded(q_gain, k_gain):
    q_scale = (DA_HEAD_DIM ** -0.5) * math.log2(math.e)
    bound = 1.02 * DA_HEAD_DIM * q_scale * jnp.max(jnp.abs(q_gain)) * jnp.max(jnp.abs(k_gain))
    return (bound <= ATT_SAFE_LOG2).astype(jnp.int32).reshape(1)


def _attention(qt, k, vt, q_gain, k_gain, lq1, lk1, lq2, lk2, subln_g, layer_idx):
    bsz, heads, nq, _, tq = qt.shape
    nkb, vrows, tk = vt.shape[2:]
    seq = nq * tq
    assert (tq, tk, vrows) == (ATT_TQ, ATT_TK, ATT_VROWS) and tq == 2 * tk and nkb * tk == seq
    lam_init = 0.8 - 0.6 * math.exp(-0.3 * layer_idx)
    vec = lambda v: v.astype(F32).reshape(1, DA_HEAD_DIM)
    vec_spec = pl.BlockSpec((1, DA_HEAD_DIM), lambda b, h, i: (0, 0))
    return pl.pallas_call(
        functools.partial(_attn_kernel, lam_init=lam_init),
        grid=(bsz, heads, nq),
        in_specs=[pl.BlockSpec(memory_space=pltpu.SMEM),
                  pl.BlockSpec((1, 1, 1, LANES, tq), lambda b, h, i: (b, h, i, 0, 0)),
                  pl.BlockSpec((1, 1, seq, LANES), lambda b, h, i: (b, h, 0, 0)),
                  pl.BlockSpec((1, 1, nkb, vrows, tk), lambda b, h, i: (b, h, 0, 0, 0)),
                  vec_spec, vec_spec, vec_spec, vec_spec,
                  pl.BlockSpec((DA_V_DIM, 1), lambda b, h, i: (0, 0))],
        out_specs=pl.BlockSpec((tq, LANES), lambda b, h, i: (b * nq + i, h)),
        out_shape=jax.ShapeDtypeStruct((bsz * seq, DA_WIDTH), BF16),
        scratch_shapes=[pltpu.VMEM((2, LANES, tq), BF16),
                        pltpu.VMEM((2, 2, tk, tq), F32),
                        pltpu.VMEM((2, 2, tk, tq), BF16),
                        pltpu.VMEM((2, 2, 1, tq), F32),
                        pltpu.VMEM((2, vrows, tq), F32),
                        pltpu.VMEM((2, 1, tq), F32)],
        compiler_params=_params("parallel", "parallel", "arbitrary"),
        name="attn",
    )(_scores_bounded(q_gain, k_gain), qt, k, vt, vec(lq1), vec(lk1), vec(lq2), vec(lk2),
      subln_g.astype(F32).reshape(DA_V_DIM, 1))


def _gla_decay_matrix():
    c = GLA_CHUNK
    t = np.arange(c)[:, None]
    s = np.arange(c)[None, :]
    blocks = [(s <= t)]
    for h in GLA_LEVELS:
        r = (t // (2 * h)) * (2 * h) + h
        upper = (t >= r) & (s > r) & (s <= t)
        lower = (t < r) & (s > t) & (s <= r)
        blocks.append(upper | lower)
    return jnp.asarray(np.concatenate(blocks, axis=0), dtype=BF16)


def _gla_kernel(q_ref, k_ref, v_ref, la_ref, sr_ref, fm_ref, g_ref, o_ref, st_ref, *, nchunk):
    @pl.when(pl.program_id(1) == 0)
    def _():
        st_ref[...] = jnp.zeros_like(st_ref)

    c = GLA_CHUNK
    ii = lax.broadcasted_iota(jnp.int32, (c, c), 0)
    jj = lax.broadcasted_iota(jnp.int32, (c, c), 1)
    eye = ii == jj
    level_masks = []
    for h in GLA_LEVELS:
        shift = int(math.log2(2 * h))
        same = (ii >> shift) == (jj >> shift)
        level_masks.append(same & ((ii & (2 * h - 1)) >= h) & ((jj & (2 * h - 1)) < h))
    fm = fm_ref[...]
    gain = g_ref[...]

    def chunk(n, carry):
        rows = pl.ds(pl.multiple_of(n * c, c), c)
        la = la_ref[rows, :]
        la_hi, la_lo = _split_bf16(la)
        fa = _dot(fm, la_hi) + _dot(fm, la_lo)
        b = fa[0:c]
        b_last = b[c - 1:c]
        e_b = jnp.exp(b)
        e_rest = jnp.exp(b_last - b)
        e_last = jnp.exp(b_last)
        e_lvl = [jnp.exp(fa[(l + 1) * c:(l + 2) * c]) for l in range(len(GLA_LEVELS))]
        for hd in range(GLA_HEADS):
            ks = slice(hd * GLA_KEY_DIM, (hd + 1) * GLA_KEY_DIM)
            vs = slice(hd * GLA_VAL_DIM, (hd + 1) * GLA_VAL_DIM)
            qb = q_ref[rows, ks]
            kb = k_ref[rows, ks]
            q = qb.astype(F32)
            k = kb.astype(F32)
            v = v_ref[rows, vs]
            state = st_ref[hd]
            out = _dot_nt((q * e_b[:, ks]).astype(BF16), state.astype(BF16))
            att = jnp.where(eye, _dot_nt(qb, kb), 0.0)
            for l in range(len(GLA_LEVELS)):
                e = e_lvl[l][:, ks]
                att = jnp.where(level_masks[l], _dot_nt((q * e).astype(BF16), (k * e).astype(BF16)), att)
            out = out + _dot(att.astype(BF16), v)
            ms = jnp.mean(out * out, axis=-1, keepdims=True)
            y = out * lax.rsqrt(ms + EPS) * gain * sr_ref[rows, vs].astype(F32)
            o_ref[rows, vs] = y.astype(BF16)
            st_ref[hd] = state * e_last[:, ks] + _dot_tn(v, (k * e_rest[:, ks]).astype(BF16))
        return carry

    lax.fori_loop(0, nchunk, chunk, 0, unroll=2)


def _gla(gq, gk, act, la, out_gain, bsz, seq):
    t = gq.shape[0]
    tm = min(GLA_TM, seq)
    tpb = seq // tm
    fm = _gla_decay_matrix()
    row = lambda b, n: (b * tpb + n, 0)
    return pl.pallas_call(
        functools.partial(_gla_kernel, nchunk=tm // GLA_CHUNK),
        grid=(bsz, tpb),
        in_specs=[pl.BlockSpec((tm, GLA_QK_WIDTH), row),
                  pl.BlockSpec((tm, GLA_QK_WIDTH), row),
                  pl.BlockSpec((tm, GLA_WIDTH), row),
                  pl.BlockSpec((tm, GLA_QK_WIDTH), row),
                  pl.BlockSpec((tm, GLA_WIDTH), lambda b, n: (b * tpb + n, 1)),
                  pl.BlockSpec(fm.shape, lambda b, n: (0, 0)),
                  pl.BlockSpec((1, GLA_VAL_DIM), lambda b, n: (0, 0))],
        out_specs=pl.BlockSpec((tm, GLA_WIDTH), row),
        out_shape=jax.ShapeDtypeStruct((t, GLA_WIDTH), BF16),
        scratch_shapes=[pltpu.VMEM((GLA_HEADS, GLA_VAL_DIM, GLA_KEY_DIM), F32)],
        compiler_params=_params("parallel", "arbitrary"),
        name="gla",
    )(gq, gk, act, la, act, fm, out_gain.astype(F32).reshape(1, GLA_VAL_DIM))


def _merge_kernel(ya_ref, yb_ref, sa_ref, sb_ref, x_ref, mod_ref, wa_ref, wb_ref, wo_ref, o_ref):
    ta = _dot(ya_ref[...], wa_ref[...])
    tb = _dot(yb_ref[...], wb_ref[...])
    merged = sa_ref[...].astype(F32) * ta + sb_ref[...].astype(F32) * tb
    delta = _dot(merged.astype(BF16), wo_ref[...])
    o_ref[...] = x_ref[...] + mod_ref[0][2:3] * delta


def _merge(ya, yb, act, x2, mod, w_a, w_b, w_o, seq):
    t, d = x2.shape
    tm = min(MERGE_TM, seq)
    tpb = seq // tm
    gate_blk = 2 * GLA_WIDTH // d
    return pl.pallas_call(
        _merge_kernel,
        grid=(t // tm,),
        in_specs=[pl.BlockSpec((tm, DA_WIDTH), lambda i: (i, 0)),
                  pl.BlockSpec((tm, GLA_WIDTH), lambda i: (i, 0)),
                  pl.BlockSpec((tm, d), lambda i: (i, gate_blk)),
                  pl.BlockSpec((tm, d), lambda i: (i, gate_blk + 1)),
                  pl.BlockSpec((tm, d), lambda i: (i, 0)),
                  pl.BlockSpec((1, N_MOD, d), lambda i: (i // tpb, 0, 0)),
                  _resident(w_a.shape), _resident(w_b.shape), _resident(w_o.shape)],
        out_specs=pl.BlockSpec((tm, d), lambda i: (i, 0)),
        out_shape=jax.ShapeDtypeStruct((t, d), F32),
        compiler_params=_params("parallel"),
        name="merge",
    )(ya, yb, act, act, x2, mod, w_a, w_b, w_o)


def _mlp_kernel(h_ref, g_ref, mod_ref, w1_ref, w2_ref, o_ref, u_ref, acc_ref):
    kf = pl.program_id(1)

    @pl.when(kf == 0)
    def _():
        u_ref[...] = _modulated_norm(h_ref[...], g_ref[...], mod_ref[0], 3, 4).astype(BF16)
        acc_ref[...] = jnp.zeros_like(acc_ref)

    hid = jnp.square(jnp.maximum(_dot(u_ref[...], w1_ref[...]), 0.0))
    acc_ref[...] += _dot(hid.astype(BF16), w2_ref[...])

    @pl.when(kf == pl.num_programs(1) - 1)
    def _():
        o_ref[...] = h_ref[...] + mod_ref[0][5:6] * acc_ref[...]


def _mlp(h, gain, mod, w1, w2, seq):
    t, d = h.shape
    tm = min(MLP_TM, seq)
    tpb = seq // tm
    ff = w1.shape[1]
    return pl.pallas_call(
        _mlp_kernel,
        grid=(t // tm, ff // MLP_TF),
        in_specs=[pl.BlockSpec((tm, d), lambda i, kf: (i, 0)),
                  pl.BlockSpec((1, d), lambda i, kf: (0, 0)),
                  pl.BlockSpec((1, N_MOD, d), lambda i, kf: (i // tpb, 0, 0)),
                  pl.BlockSpec((d, MLP_TF), lambda i, kf: (0, kf)),
                  pl.BlockSpec((MLP_TF, d), lambda i, kf: (kf, 0))],
        out_specs=pl.BlockSpec((tm, d), lambda i, kf: (i, 0)),
        out_shape=jax.ShapeDtypeStruct((t, d), F32),
        scratch_shapes=[pltpu.VMEM((tm, d), BF16), pltpu.VMEM((tm, d), F32)],
        compiler_params=_params("parallel", "arbitrary"),
        name="mlp",
    )(h, gain.reshape(1, d), mod, w1, w2)


def _layer(h2, c, positions, bsz, seq, layer_idx, w_ada, b_ada, norm1_g, w_in, da_q_norm_g, da_k_norm_g,
           lq1, lk1, lq2, lk2, da_subln_g, gla_gate_up, gla_gate_bias, gla_out_norm_g,
           w_branch_a, w_branch_b, w_out, norm2_g, w_mlp_in, w_mlp_out):
    d = h2.shape[1]
    assert seq % min(PROJ_TM, seq) == 0 and seq % GLA_CHUNK == 0
    offs = np.concatenate([[0], np.cumsum(IN_SIZES)])
    col = lambda a, b: w_in[:, offs[a]:offs[b]].astype(BF16)
    w_qkv = col(0, 3)
    w_g = jnp.concatenate([col(3, 5), col(7, 8), jnp.zeros((d, LANES - GLA_GATE_RANK), BF16)], axis=1)
    w_act = jnp.concatenate([col(5, 7), col(8, 10)], axis=1)

    mod = _ada(c, w_ada, b_ada)
    u = _normmod(h2, norm1_g, mod, seq)
    qt, k, vt = _qkv(u, w_qkv, positions, da_q_norm_g, da_k_norm_g, bsz, seq)
    gq, gk, la = _glaqk(u, w_g, gla_gate_up, gla_gate_bias, seq)
    act = _act(u, w_act, seq)
    ya = _attention(qt, k, vt, da_q_norm_g, da_k_norm_g, lq1, lk1, lq2, lk2, da_subln_g, layer_idx)
    yb = _gla(gq, gk, act, la, gla_out_norm_g, bsz, seq)
    h1 = _merge(ya, yb, act, h2, mod, w_branch_a.astype(BF16), w_branch_b.astype(BF16),
                w_out.astype(BF16), seq)
    return _mlp(h1, norm2_g, mod, w_mlp_in.astype(BF16), w_mlp_out.astype(BF16), seq)


def kernel(x, c, positions, w_ada, b_ada, norm1_g, w_in, da_q_norm_g, da_k_norm_g, da_lambda_q1,
           da_lambda_k1, da_lambda_q2, da_lambda_k2, da_subln_g, gla_gate_up, gla_gate_bias,
           gla_out_norm_g, w_branch_a, w_branch_b, w_out, norm2_g, w_mlp_in, w_mlp_out):
    bsz, seq, d = x.shape
    h = x.reshape(bsz * seq, d)
    for l in range(w_ada.shape[0]):
        h = _layer(h, c, positions, bsz, seq, l, w_ada[l], b_ada[l], norm1_g[l], w_in[l], da_q_norm_g[l],
                   da_k_norm_g[l], da_lambda_q1[l], da_lambda_k1[l], da_lambda_q2[l], da_lambda_k2[l],
                   da_subln_g[l], gla_gate_up[l], gla_gate_bias[l], gla_out_norm_g[l], w_branch_a[l],
                   w_branch_b[l], w_out[l], norm2_g[l], w_mlp_in[l], w_mlp_out[l])
    return h.reshape(bsz, seq, d)
```

```python
import functools
import math

import jax
import jax.numpy as jnp
import numpy as np
from jax import lax
from jax.experimental import pallas as pl
from jax.experimental.pallas import tpu as pltpu

F32 = jnp.float32
BF16 = jnp.bfloat16

D_MODEL = 2048
DA_HEADS = 8
DA_HEAD_DIM = 64
DA_V_DIM = 2 * DA_HEAD_DIM
DA_QK_WIDTH = DA_HEADS * 2 * DA_HEAD_DIM
DA_WIDTH = DA_HEADS * DA_V_DIM
ROPE_THETA = 500000.0
ROPE_DIM = DA_HEAD_DIM // 4
GLA_HEADS = 4
GLA_KEY_DIM = 128
GLA_VAL_DIM = 256
GLA_QK_WIDTH = GLA_HEADS * GLA_KEY_DIM
GLA_WIDTH = GLA_HEADS * GLA_VAL_DIM
GLA_GATE_RANK = 16
GLA_GATE_TAU = 16.0
GLA_CHUNK = 64
D_FF = 4 * D_MODEL
N_MOD = 6
EPS = 1e-6
IN_SIZES = (DA_QK_WIDTH, DA_QK_WIDTH, DA_WIDTH, GLA_QK_WIDTH, GLA_QK_WIDTH, GLA_WIDTH, GLA_WIDTH,
            GLA_GATE_RANK, D_MODEL, D_MODEL)

LANES = 128
SUBLANES = 8
VMEM_LIMIT_BYTES = 56 * 1024 * 1024

ADA_TN = 1024
NORM_TM = 512
PROJ_TM = 512
ACT_TM = 1024
ACT_TN = 1024
ATT_TK = PROJ_TM
ATT_TQ = 2 * ATT_TK
ATT_QC = 256
ATT_VROWS = DA_V_DIM + 16
GLA_TM = 512
MERGE_TM = 512
MERGE_TN = 512
MLP_TM = 1024
MLP_TF = 512

GLA_LEVELS = (32, 16, 8, 4, 2, 1)


def _dot(a, b):
    return jnp.dot(a, b, preferred_element_type=F32)


def _dot_nt(a, b):
    return lax.dot_general(a, b, (((1,), (1,)), ((), ())), preferred_element_type=F32)


def _dot_tn(a, b):
    return lax.dot_general(a, b, (((0,), (0,)), ((), ())), preferred_element_type=F32)


def _split_bf16(x):
    hi = x.astype(BF16)
    lo = (x - hi.astype(F32)).astype(BF16)
    return hi, lo


def _params(*semantics):
    return pltpu.CompilerParams(dimension_semantics=semantics, vmem_limit_bytes=VMEM_LIMIT_BYTES)


def _resident(shape):
    nd = len(shape)
    return pl.BlockSpec(shape, lambda *_: (0,) * nd, pipeline_mode=pl.Buffered(1))


def _ada_kernel(c_ref, w_ref, b_ref, o_ref):
    c = c_ref[...]
    sc = c * jax.nn.sigmoid(c)
    o_ref[...] = _dot(sc.astype(BF16), w_ref[...].astype(BF16)) + b_ref[...]


def _ada(c, w_ada, b_ada):
    bsz, d = c.shape
    n = w_ada.shape[1]
    c8 = jnp.zeros((SUBLANES, d), F32).at[:bsz].set(c)
    out = pl.pallas_call(
        _ada_kernel,
        grid=(n // ADA_TN,),
        in_specs=[pl.BlockSpec((SUBLANES, d), lambda j: (0, 0)),
                  pl.BlockSpec((d, ADA_TN), lambda j: (0, j)),
                  pl.BlockSpec((1, ADA_TN), lambda j: (0, j))],
        out_specs=pl.BlockSpec((SUBLANES, ADA_TN), lambda j: (0, j)),
        out_shape=jax.ShapeDtypeStruct((SUBLANES, n), F32),
        compiler_params=_params("arbitrary"),
        name="ada",
    )(c8, w_ada, b_ada.reshape(1, n))
    return out[:bsz].reshape(bsz, N_MOD, d)


def _modulated_norm(x, gain, mod, shift_idx, scale_idx):
    ms = jnp.mean(x * x, axis=-1, keepdims=True)
    y = x * lax.rsqrt(ms + EPS) * gain
    return y * (1.0 + mod[scale_idx:scale_idx + 1]) + mod[shift_idx:shift_idx + 1]


def _normmod_kernel(x_ref, g_ref, mod_ref, o_ref):
    o_ref[...] = _modulated_norm(x_ref[...], g_ref[...], mod_ref[0], 0, 1).astype(BF16)


def _normmod(x2, gain, mod, seq):
    t, d = x2.shape
    tm = min(NORM_TM, seq)
    tpb = seq // tm
    return pl.pallas_call(
        _normmod_kernel,
        grid=(t // tm,),
        in_specs=[pl.BlockSpec((tm, d), lambda i: (i, 0)),
                  pl.BlockSpec((1, d), lambda i: (0, 0)),
                  pl.BlockSpec((1, N_MOD, d), lambda i: (i // tpb, 0, 0))],
        out_specs=pl.BlockSpec((tm, d), lambda i: (i, 0)),
        out_shape=jax.ShapeDtypeStruct((t, d), BF16),
        compiler_params=_params("parallel"),
        name="normmod",
    )(x2, gain.reshape(1, d), mod)


def _qkv_kernel(u_ref, w_ref, pos_ref, freq_ref, sa_ref, sb_ref, gq_ref, gk_ref, grp_ref,
                qt_ref, k_ref, vt_ref):
    u = u_ref[...]
    ang = pos_ref[...].astype(F32) * freq_ref[...]
    cs = jnp.cos(ang)
    sn = jnp.sin(ang)
    sin_a = sn * sa_ref[...]
    sin_b = sn * sb_ref[...]
    grp = grp_ref[...]

    def norm_rope(x, gain):
        hi, lo = _split_bf16(x * x)
        ss = _dot(hi, grp) + _dot(lo, grp)
        xn = x * lax.rsqrt(ss * (1.0 / DA_HEAD_DIM) + EPS) * gain
        half = ROPE_DIM // 2
        return (xn * cs + pltpu.roll(xn, LANES - half, 1) * sin_a + pltpu.roll(xn, half, 1) * sin_b)

    q_scale = (DA_HEAD_DIM ** -0.5) * math.log2(math.e)
    acc = _dot(u, w_ref[:, 0:DA_QK_WIDTH])
    gq = gq_ref[...]
    for h in range(DA_HEADS):
        xq = norm_rope(acc[:, h * LANES:(h + 1) * LANES], gq) * q_scale
        qt_ref[0, h, 0] = xq.T.astype(BF16)
    acc = _dot(u, w_ref[:, DA_QK_WIDTH:2 * DA_QK_WIDTH])
    gk = gk_ref[...]
    for h in range(DA_HEADS):
        k_ref[0, h] = norm_rope(acc[:, h * LANES:(h + 1) * LANES], gk).astype(BF16)
    acc = _dot(u, w_ref[:, 2 * DA_QK_WIDTH:2 * DA_QK_WIDTH + DA_WIDTH])
    tm = u.shape[0]
    ones_row = (lax.broadcasted_iota(jnp.int32, (ATT_VROWS - DA_V_DIM, tm), 0) == 0).astype(BF16)
    for h in range(DA_HEADS):
        vt_ref[0, h, 0, 0:DA_V_DIM, :] = acc[:, h * LANES:(h + 1) * LANES].T.astype(BF16)
        vt_ref[0, h, 0, DA_V_DIM:ATT_VROWS, :] = ones_row


def _rope_lane_tables():
    lane = np.arange(LANES) % DA_HEAD_DIM
    inv_freq = ROPE_THETA ** (-jnp.arange(0, ROPE_DIM, 2, dtype=F32) / ROPE_DIM)
    half = ROPE_DIM // 2
    freq = jnp.where(lane < ROPE_DIM, inv_freq[lane % half], 0.0).astype(F32).reshape(1, LANES)
    sign_a = np.where(lane < half, -1.0, 0.0).astype(np.float32).reshape(1, LANES)
    mask_b = np.where((lane >= half) & (lane < ROPE_DIM), 1.0, 0.0).astype(np.float32).reshape(1, LANES)
    group = (np.arange(LANES)[:, None] // DA_HEAD_DIM == np.arange(LANES)[None, :] // DA_HEAD_DIM)
    return freq, jnp.asarray(sign_a), jnp.asarray(mask_b), jnp.asarray(group, dtype=BF16)


def _qkv(u, w_qkv, positions, q_gain, k_gain, bsz, seq):
    t, d = u.shape
    tm = min(PROJ_TM, seq)
    tpb = seq // tm
    freq, sign_a, mask_b, group = _rope_lane_tables()
    lane_vec = lambda v: jnp.tile(v.astype(F32), LANES // DA_HEAD_DIM).reshape(1, LANES)
    vec_spec = pl.BlockSpec((1, LANES), lambda i: (0, 0))
    assert tm == ATT_TK and seq % ATT_TQ == 0
    per_q = ATT_TQ // tm
    qt_shape = jax.ShapeDtypeStruct((bsz, DA_HEADS, seq // ATT_TQ, LANES, ATT_TQ), BF16)
    qt_spec = pl.BlockSpec((1, DA_HEADS, 1, LANES, tm),
                           lambda i: (i // tpb, 0, (i % tpb) // per_q, 0, (i % tpb) % per_q))
    vt_shape = jax.ShapeDtypeStruct((bsz, DA_HEADS, tpb, ATT_VROWS, tm), BF16)
    vt_spec = pl.BlockSpec((1, DA_HEADS, 1, ATT_VROWS, tm), lambda i: (i // tpb, 0, i % tpb, 0, 0))
    return pl.pallas_call(
        _qkv_kernel,
        grid=(t // tm,),
        in_specs=[pl.BlockSpec((tm, d), lambda i: (i, 0)),
                  _resident(w_qkv.shape),
                  pl.BlockSpec((tm, 1), lambda i: (i, 0)),
                  vec_spec, vec_spec, vec_spec, vec_spec, vec_spec,
                  pl.BlockSpec((LANES, LANES), lambda i: (0, 0))],
        out_specs=[qt_spec,
                   pl.BlockSpec((1, DA_HEADS, tm, LANES), lambda i: (i // tpb, 0, i % tpb, 0)),
                   vt_spec],
        out_shape=[qt_shape, jax.ShapeDtypeStruct((bsz, DA_HEADS, seq, LANES), BF16), vt_shape],
        compiler_params=_params("parallel"),
        name="qkv",
    )(u, w_qkv, positions.reshape(t, 1), freq, sign_a, mask_b, lane_vec(q_gain), lane_vec(k_gain), group)


def _glaqk_kernel(u_ref, w_ref, up_ref, bias_ref, q_ref, k_ref, la_ref):
    acc = _dot(u_ref[...], w_ref[...])
    q_ref[...] = (acc[:, 0:GLA_QK_WIDTH] * (GLA_KEY_DIM ** -0.5)).astype(BF16)
    k_ref[...] = acc[:, GLA_QK_WIDTH:2 * GLA_QK_WIDTH].astype(BF16)
    low = acc[:, 2 * GLA_QK_WIDTH:]
    low_hi, low_lo = _split_bf16(low)
    up_hi, up_lo = _split_bf16(up_ref[...])
    z = _dot(low_hi, up_hi) + _dot(low_lo, up_hi) + _dot(low_hi, up_lo) + bias_ref[...]
    log_sig = jnp.minimum(z, 0.0) - jnp.log1p(jnp.exp(-jnp.abs(z)))
    la_ref[...] = log_sig * (1.0 / GLA_GATE_TAU)


def _glaqk(u, w_g, gate_up, gate_bias, seq):
    t, d = u.shape
    tm = min(PROJ_TM, seq)
    up = jnp.zeros((LANES, GLA_QK_WIDTH), F32).at[:GLA_GATE_RANK].set(gate_up)
    row_spec = lambda w: pl.BlockSpec((tm, w), lambda i: (i, 0))
    return pl.pallas_call(
        _glaqk_kernel,
        grid=(t // tm,),
        in_specs=[row_spec(d), _resident(w_g.shape), _resident(up.shape),
                  pl.BlockSpec((1, GLA_QK_WIDTH), lambda i: (0, 0))],
        out_specs=[row_spec(GLA_QK_WIDTH), row_spec(GLA_QK_WIDTH), row_spec(GLA_QK_WIDTH)],
        out_shape=[jax.ShapeDtypeStruct((t, GLA_QK_WIDTH), BF16),
                   jax.ShapeDtypeStruct((t, GLA_QK_WIDTH), BF16),
                   jax.ShapeDtypeStruct((t, GLA_QK_WIDTH), F32)],
        compiler_params=_params("parallel"),
        name="glaqk",
    )(u, w_g, up, gate_bias.reshape(1, GLA_QK_WIDTH))


ACT_SILU_BLOCKS = (GLA_WIDTH // ACT_TN, 2 * GLA_WIDTH // ACT_TN)


def _act_kernel(u_ref, w_ref, o_ref):
    j = pl.program_id(0)
    acc = _dot(u_ref[...], w_ref[...])
    sig = jax.nn.sigmoid(acc)
    out = jnp.where(j < ACT_SILU_BLOCKS[0], acc, jnp.where(j < ACT_SILU_BLOCKS[1], acc * sig, sig))
    o_ref[...] = out.astype(BF16)


def _act(u, w_act, seq):
    t, d = u.shape
    n = w_act.shape[1]
    tm = min(ACT_TM, seq)
    return pl.pallas_call(
        _act_kernel,
        grid=(n // ACT_TN, t // tm),
        in_specs=[pl.BlockSpec((tm, d), lambda j, i: (i, 0)),
                  pl.BlockSpec((d, ACT_TN), lambda j, i: (0, j))],
        out_specs=pl.BlockSpec((tm, ACT_TN), lambda j, i: (i, j)),
        out_shape=jax.ShapeDtypeStruct((t, n), BF16),
        compiler_params=_params("arbitrary", "arbitrary"),
        name="act",
    )(u, w_act)


_FULL, _MASK, _SKIP = "full", "mask", "skip"


def _attn_kernel(bounded_ref, qt_ref, k_ref, vt_ref, lq1_ref, lk1_ref, lq2_ref, lk2_ref, subg_ref, o_ref,
                 qc_ref, s_ref, p_ref, cm_ref, acc_ref, m_ref, *, lam_init):
    i = pl.program_id(2)
    tq, tk, qc_w = ATT_TQ, ATT_TK, ATT_QC
    nqc = tq // qc_w
    qt = qt_ref[0, 0, 0]
    row = lax.broadcasted_iota(jnp.int32, qt.shape, 0)
    zero = jnp.zeros_like(qt)
    qc_ref[0] = jnp.where(row < DA_HEAD_DIM, qt, zero)
    qc_ref[1] = jnp.where(row >= DA_HEAD_DIM, qt, zero)
    acc_ref[...] = jnp.zeros_like(acc_ref)
    m_ref[...] = jnp.full_like(m_ref, -jnp.inf)

    def scores(t, slot, modes):
        kb = k_ref[0, 0, pl.ds(pl.multiple_of(t * tk, tk), tk), :]
        for c in range(2):
            for q in range(nqc):
                if modes[q] == _SKIP:
                    continue
                cols = slice(q * qc_w, (q + 1) * qc_w)
                s = _dot(kb, qc_ref[c, :, cols])
                if modes[q] == _MASK:
                    rel = (lax.broadcasted_iota(jnp.int32, (tk, qc_w), 0)
                           - lax.broadcasted_iota(jnp.int32, (tk, qc_w), 1))
                    s = jnp.where(rel <= i * tq + q * qc_w - t * tk, s, -jnp.inf)
                s_ref[slot, c, :, cols] = s
                cm_ref[slot, c, :, cols] = jnp.max(s, axis=0, keepdims=True)

    def accumulate(t, slot, modes):
        vtb = vt_ref[0, 0, t]
        for c in range(2):
            for q in range(nqc):
                if modes[q] == _SKIP:
                    continue
                cols = slice(q * qc_w, (q + 1) * qc_w)
                m_old = m_ref[c, :, cols]
                m_new = jnp.maximum(m_old, cm_ref[slot, c, :, cols])
                alpha = jnp.exp2(m_old - m_new)
                p = jnp.exp2(s_ref[slot, c, :, cols] - m_new)
                acc_ref[c, :, cols] = alpha * acc_ref[c, :, cols] + _dot(vtb, p.astype(BF16))
                m_ref[c, :, cols] = m_new

    half = tk // qc_w
    even_modes = (_MASK,) * half + (_FULL,) * (nqc - half)
    full_modes = (_FULL,) * nqc
    last_modes = (_SKIP,) * half + (_MASK,) * (nqc - half)

    def probs(t, slot, modes):
        kb = k_ref[0, 0, pl.ds(pl.multiple_of(t * tk, tk), tk), :]
        for c in range(2):
            for q in range(nqc):
                if modes[q] == _SKIP:
                    continue
                cols = slice(q * qc_w, (q + 1) * qc_w)
                s = _dot(kb, qc_ref[c, :, cols])
                if modes[q] == _MASK:
                    rel = (lax.broadcasted_iota(jnp.int32, (tk, qc_w), 0)
                           - lax.broadcasted_iota(jnp.int32, (tk, qc_w), 1))
                    s = jnp.where(rel <= i * tq + q * qc_w - t * tk, s, -jnp.inf)
                p_ref[slot, c, :, cols] = jnp.exp2(s).astype(BF16)

    def weigh(t, slot, modes):
        vtb = vt_ref[0, 0, t]
        for c in range(2):
            for q in range(nqc):
                if modes[q] == _SKIP:
                    continue
                cols = slice(q * qc_w, (q + 1) * qc_w)
                acc_ref[c, :, cols] += _dot(vtb, p_ref[slot, c, :, cols])

    @pl.when(bounded_ref[0] != 0)
    def _():
        probs(0, 0, even_modes)

        def pair(p, carry):
            probs(2 * p + 1, 1, full_modes)
            weigh(2 * p, 0, full_modes)
            probs(2 * p + 2, 0, even_modes)
            weigh(2 * p + 1, 1, full_modes)
            return carry

        lax.fori_loop(0, i, pair, 0)
        probs(2 * i + 1, 1, last_modes)
        weigh(2 * i, 0, full_modes)
        weigh(2 * i + 1, 1, last_modes)

    @pl.when(bounded_ref[0] == 0)
    def _():
        scores(0, 0, even_modes)

        def pair(p, carry):
            scores(2 * p + 1, 1, full_modes)
            accumulate(2 * p, 0, full_modes)
            scores(2 * p + 2, 0, even_modes)
            accumulate(2 * p + 1, 1, full_modes)
            return carry

        lax.fori_loop(0, i, pair, 0)
        scores(2 * i + 1, 1, last_modes)
        accumulate(2 * i, 0, full_modes)
        accumulate(2 * i + 1, 1, last_modes)

    s1 = jnp.sum(lq1_ref[...] * lk1_ref[...], axis=1, keepdims=True)
    s2 = jnp.sum(lq2_ref[...] * lk2_ref[...], axis=1, keepdims=True)
    lam = jnp.exp(s1) - jnp.exp(s2) + lam_init
    num0, den0 = acc_ref[0, 0:DA_V_DIM, :], acc_ref[0, DA_V_DIM:DA_V_DIM + 1, :]
    num1, den1 = acc_ref[1, 0:DA_V_DIM, :], acc_ref[1, DA_V_DIM:DA_V_DIM + 1, :]
    o = num0 / den0 - lam * (num1 / den1)
    ms = jnp.mean(o * o, axis=0, keepdims=True)
    y = o * lax.rsqrt(ms + EPS) * subg_ref[...] * (1.0 - lam_init)
    o_ref[...] = y.T.astype(BF16)


ATT_SAFE_LOG2 = 60.0


def _scores_bounded(q_gain, k_gain):
    q_scale = (DA_HEAD_DIM ** -0.5) * math.log2(math.e)
    bound = 1.02 * DA_HEAD_DIM * q_scale * jnp.max(jnp.abs(q_gain)) * jnp.max(jnp.abs(k_gain))
    return (bound <= ATT_SAFE_LOG2).astype(jnp.int32).reshape(1)


def _attention(qt, k, vt, q_gain, k_gain, lq1, lk1, lq2, lk2, subln_g, layer_idx):
    bsz, heads, nq, _, tq = qt.shape
    nkb, vrows, tk = vt.shape[2:]
    seq = nq * tq
    assert (tq, tk, vrows) == (ATT_TQ, ATT_TK, ATT_VROWS) and tq == 2 * tk and nkb * tk == seq
    lam_init = 0.8 - 0.6 * math.exp(-0.3 * layer_idx)
    vec = lambda v: v.astype(F32).reshape(1, DA_HEAD_DIM)
    vec_spec = pl.BlockSpec((1, DA_HEAD_DIM), lambda b, h, i: (0, 0))
    return pl.pallas_call(
        functools.partial(_attn_kernel, lam_init=lam_init),
        grid=(bsz, heads, nq),
        in_specs=[pl.BlockSpec(memory_space=pltpu.SMEM),
                  pl.BlockSpec((1, 1, 1, LANES, tq), lambda b, h, i: (b, h, i, 0, 0)),
                  pl.BlockSpec((1, 1, seq, LANES), lambda b, h, i: (b, h, 0, 0)),
                  pl.BlockSpec((1, 1, nkb, vrows, tk), lambda b, h, i: (b, h, 0, 0, 0)),
                  vec_spec, vec_spec, vec_spec, vec_spec,
                  pl.BlockSpec((DA_V_DIM, 1), lambda b, h, i: (0, 0))],
        out_specs=pl.BlockSpec((tq, LANES), lambda b, h, i: (b * nq + i, h)),
        out_shape=jax.ShapeDtypeStruct((bsz * seq, DA_WIDTH), BF16),
        scratch_shapes=[pltpu.VMEM((2, LANES, tq), BF16),
                        pltpu.VMEM((2, 2, tk, tq), F32),
                        pltpu.VMEM((2, 2, tk, tq), BF16),
                        pltpu.VMEM((2, 2, 1, tq), F32),
                        pltpu.VMEM((2, vrows, tq), F32),
                        pltpu.VMEM((2, 1, tq), F32)],
        compiler_params=_params("parallel", "parallel", "arbitrary"),
        name="attn",
    )(_scores_bounded(q_gain, k_gain), qt, k, vt, vec(lq1), vec(lk1), vec(lq2), vec(lk2),
      subln_g.astype(F32).reshape(DA_V_DIM, 1))


def _gla_decay_matrix():
    c = GLA_CHUNK
    t = np.arange(c)[:, None]
    s = np.arange(c)[None, :]
    blocks = [(s <= t)]
    for h in GLA_LEVELS:
        r = (t // (2 * h)) * (2 * h) + h
        upper = (t >= r) & (s > r) & (s <= t)
        lower = (t < r) & (s > t) & (s <= r)
        blocks.append(upper | lower)
    return jnp.asarray(np.concatenate(blocks, axis=0), dtype=BF16)


def _gla_kernel(q_ref, k_ref, v_ref, la_ref, sr_ref, fm_ref, g_ref, o_ref, st_ref, *, nchunk):
    @pl.when(pl.program_id(1) == 0)
    def _():
        st_ref[...] = jnp.zeros_like(st_ref)

    c = GLA_CHUNK
    ii = lax.broadcasted_iota(jnp.int32, (c, c), 0)
    jj = lax.broadcasted_iota(jnp.int32, (c, c), 1)
    eye = ii == jj
    level_masks = []
    for h in GLA_LEVELS:
        shift = int(math.log2(2 * h))
        same = (ii >> shift) == (jj >> shift)
        level_masks.append(same & ((ii & (2 * h - 1)) >= h) & ((jj & (2 * h - 1)) < h))
    fm = fm_ref[...]
    gain = g_ref[...]

    def chunk(n, carry):
        rows = pl.ds(pl.multiple_of(n * c, c), c)
        la = la_ref[rows, :]
        la_hi, la_lo = _split_bf16(la)
        fa = _dot(fm, la_hi) + _dot(fm, la_lo)
        b = fa[0:c]
        b_last = b[c - 1:c]
        e_b = jnp.exp(b)
        e_rest = jnp.exp(b_last - b)
        e_last = jnp.exp(b_last)
        e_lvl = [jnp.exp(fa[(l + 1) * c:(l + 2) * c]) for l in range(len(GLA_LEVELS))]
        for hd in range(GLA_HEADS):
            ks = slice(hd * GLA_KEY_DIM, (hd + 1) * GLA_KEY_DIM)
            vs = slice(hd * GLA_VAL_DIM, (hd + 1) * GLA_VAL_DIM)
            qb = q_ref[rows, ks]
            kb = k_ref[rows, ks]
            q = qb.astype(F32)
            k = kb.astype(F32)
            v = v_ref[rows, vs]
            state = st_ref[hd]
            out = _dot_nt((q * e_b[:, ks]).astype(BF16), state.astype(BF16))
            att = jnp.where(eye, _dot_nt(qb, kb), 0.0)
            for l in range(len(GLA_LEVELS)):
                e = e_lvl[l][:, ks]
                att = jnp.where(level_masks[l], _dot_nt((q * e).astype(BF16), (k * e).astype(BF16)), att)
            out = out + _dot(att.astype(BF16), v)
            ms = jnp.mean(out * out, axis=-1, keepdims=True)
            y = out * lax.rsqrt(ms + EPS) * gain * sr_ref[rows, vs].astype(F32)
            o_ref[rows, vs] = y.astype(BF16)
            st_ref[hd] = state * e_last[:, ks] + _dot_tn(v, (k * e_rest[:, ks]).astype(BF16))
        return carry

    lax.fori_loop(0, nchunk, chunk, 0, unroll=2)


def _gla(gq, gk, act, la, out_gain, bsz, seq):
    t = gq.shape[0]
    tm = min(GLA_TM, seq)
    tpb = seq // tm
    fm = _gla_decay_matrix()
    row = lambda b, n: (b * tpb + n, 0)
    return pl.pallas_call(
        functools.partial(_gla_kernel, nchunk=tm // GLA_CHUNK),
        grid=(bsz, tpb),
        in_specs=[pl.BlockSpec((tm, GLA_QK_WIDTH), row),
                  pl.BlockSpec((tm, GLA_QK_WIDTH), row),
                  pl.BlockSpec((tm, GLA_WIDTH), row),
                  pl.BlockSpec((tm, GLA_QK_WIDTH), row),
                  pl.BlockSpec((tm, GLA_WIDTH), lambda b, n: (b * tpb + n, 1)),
                  pl.BlockSpec(fm.shape, lambda b, n: (0, 0)),
                  pl.BlockSpec((1, GLA_VAL_DIM), lambda b, n: (0, 0))],
        out_specs=pl.BlockSpec((tm, GLA_WIDTH), row),
        out_shape=jax.ShapeDtypeStruct((t, GLA_WIDTH), BF16),
        scratch_shapes=[pltpu.VMEM((GLA_HEADS, GLA_VAL_DIM, GLA_KEY_DIM), F32)],
        compiler_params=_params("parallel", "arbitrary"),
        name="gla",
    )(gq, gk, act, la, act, fm, out_gain.astype(F32).reshape(1, GLA_VAL_DIM))


def _merge_kernel(ya_ref, yb_ref, sa_ref, sb_ref, x_ref, mod_ref, wa_ref, wb_ref, wo_ref, o_ref, mg_ref):
    ya = ya_ref[...]
    yb = yb_ref[...]
    for n in range(mg_ref.shape[1] // MERGE_TN):
        cols = slice(n * MERGE_TN, (n + 1) * MERGE_TN)
        ta = _dot(ya, wa_ref[:, cols])
        tb = _dot(yb, wb_ref[:, cols])
        mg_ref[:, cols] = (sa_ref[:, cols].astype(F32) * ta + sb_ref[:, cols].astype(F32) * tb).astype(BF16)
    o_ref[...] = x_ref[...] + mod_ref[0][2:3] * _dot(mg_ref[...], wo_ref[...])


def _merge(ya, yb, act, x2, mod, w_a, w_b, w_o, seq):
    t, d = x2.shape
    tm = min(MERGE_TM, seq)
    tpb = seq // tm
    gate_blk = 2 * GLA_WIDTH // d
    return pl.pallas_call(
        _merge_kernel,
        grid=(t // tm,),
        in_specs=[pl.BlockSpec((tm, DA_WIDTH), lambda i: (i, 0)),
                  pl.BlockSpec((tm, GLA_WIDTH), lambda i: (i, 0)),
                  pl.BlockSpec((tm, d), lambda i: (i, gate_blk)),
                  pl.BlockSpec((tm, d), lambda i: (i, gate_blk + 1)),
                  pl.BlockSpec((tm, d), lambda i: (i, 0)),
                  pl.BlockSpec((1, N_MOD, d), lambda i: (i // tpb, 0, 0)),
                  _resident(w_a.shape), _resident(w_b.shape), _resident(w_o.shape)],
        out_specs=pl.BlockSpec((tm, d), lambda i: (i, 0)),
        out_shape=jax.ShapeDtypeStruct((t, d), F32),
        scratch_shapes=[pltpu.VMEM((tm, d), BF16)],
        compiler_params=_params("parallel"),
        name="merge",
    )(ya, yb, act, act, x2, mod, w_a, w_b, w_o)


def _mlp_kernel(h_ref, g_ref, mod_ref, w1_ref, w2_ref, o_ref, u_ref):
    @pl.when(pl.program_id(1) == 0)
    def _():
        h = h_ref[...]
        u_ref[...] = _modulated_norm(h, g_ref[...], mod_ref[0], 3, 4).astype(BF16)
        o_ref[...] = h

    hid = jnp.square(jnp.maximum(_dot(u_ref[...], w1_ref[...]), 0.0))
    o_ref[...] += mod_ref[0][5:6] * _dot(hid.astype(BF16), w2_ref[...])


def _mlp(h, gain, mod, w1, w2, seq):
    t, d = h.shape
    tm = min(MLP_TM, seq)
    tpb = seq // tm
    ff = w1.shape[1]
    return pl.pallas_call(
        _mlp_kernel,
        grid=(t // tm, ff // MLP_TF),
        in_specs=[pl.BlockSpec((tm, d), lambda i, kf: (i, 0)),
                  pl.BlockSpec((1, d), lambda i, kf: (0, 0)),
                  pl.BlockSpec((1, N_MOD, d), lambda i, kf: (i // tpb, 0, 0)),
                  pl.BlockSpec((d, MLP_TF), lambda i, kf: (0, kf)),
                  pl.BlockSpec((MLP_TF, d), lambda i, kf: (kf, 0))],
        out_specs=pl.BlockSpec((tm, d), lambda i, kf: (i, 0)),
        out_shape=jax.ShapeDtypeStruct((t, d), F32),
        scratch_shapes=[pltpu.VMEM((tm, d), BF16)],
        compiler_params=_params("parallel", "arbitrary"),
        name="mlp",
    )(h, gain.reshape(1, d), mod, w1, w2)


def _layer(h2, c, positions, bsz, seq, layer_idx, w_ada, b_ada, norm1_g, w_in, da_q_norm_g, da_k_norm_g,
           lq1, lk1, lq2, lk2, da_subln_g, gla_gate_up, gla_gate_bias, gla_out_norm_g,
           w_branch_a, w_branch_b, w_out, norm2_g, w_mlp_in, w_mlp_out):
    d = h2.shape[1]
    assert seq % min(PROJ_TM, seq) == 0 and seq % GLA_CHUNK == 0
    offs = np.concatenate([[0], np.cumsum(IN_SIZES)])
    col = lambda a, b: w_in[:, offs[a]:offs[b]].astype(BF16)
    w_qkv = col(0, 3)
    w_g = jnp.concatenate([col(3, 5), col(7, 8), jnp.zeros((d, LANES - GLA_GATE_RANK), BF16)], axis=1)
    w_act = jnp.concatenate([col(5, 7), col(8, 10)], axis=1)

    mod = _ada(c, w_ada, b_ada)
    u = _normmod(h2, norm1_g, mod, seq)
    qt, k, vt = _qkv(u, w_qkv, positions, da_q_norm_g, da_k_norm_g, bsz, seq)
    gq, gk, la = _glaqk(u, w_g, gla_gate_up, gla_gate_bias, seq)
    act = _act(u, w_act, seq)
    ya = _attention(qt, k, vt, da_q_norm_g, da_k_norm_g, lq1, lk1, lq2, lk2, da_subln_g, layer_idx)
    yb = _gla(gq, gk, act, la, gla_out_norm_g, bsz, seq)
    h1 = _merge(ya, yb, act, h2, mod, w_branch_a.astype(BF16), w_branch_b.astype(BF16),
                w_out.astype(BF16), seq)
    return _mlp(h1, norm2_g, mod, w_mlp_in.astype(BF16), w_mlp_out.astype(BF16), seq)


def kernel(x, c, positions, w_ada, b_ada, norm1_g, w_in, da_q_norm_g, da_k_norm_g, da_lambda_q1,
           da_lambda_k1, da_lambda_q2, da_lambda_k2, da_subln_g, gla_gate_up, gla_gate_bias,
           gla_out_norm_g, w_branch_a, w_branch_b, w_out, norm2_g, w_mlp_in, w_mlp_out):
    bsz, seq, d = x.shape
    h = x.reshape(bsz * seq, d)
    for l in range(w_ada.shape[0]):
        h = _layer(h, c, positions, bsz, seq, l, w_ada[l], b_ada[l], norm1_g[l], w_in[l], da_q_norm_g[l],
                   da_k_norm_g[l], da_lambda_q1[l], da_lambda_k1[l], da_lambda_q2[l], da_lambda_k2[l],
                   da_subln_g[l], gla_gate_up[l], gla_gate_bias[l], gla_out_norm_g[l], w_branch_a[l],
                   w_branch_b[l], w_out[l], norm2_g[l], w_mlp_in[l], w_mlp_out[l])
    return h.reshape(bsz, seq, d)
```

```python
import functools
import math

import jax
import jax.numpy as jnp
import numpy as np
from jax import lax
from jax.experimental import pallas as pl
from jax.experimental.pallas import tpu as pltpu

F32 = jnp.float32
BF16 = jnp.bfloat16

D_MODEL = 2048
DA_HEADS = 8
DA_HEAD_DIM = 64
DA_V_DIM = 2 * DA_HEAD_DIM
DA_QK_WIDTH = DA_HEADS * 2 * DA_HEAD_DIM
DA_WIDTH = DA_HEADS * DA_V_DIM
ROPE_THETA = 500000.0
ROPE_DIM = DA_HEAD_DIM // 4
GLA_HEADS = 4
GLA_KEY_DIM = 128
GLA_VAL_DIM = 256
GLA_QK_WIDTH = GLA_HEADS * GLA_KEY_DIM
GLA_WIDTH = GLA_HEADS * GLA_VAL_DIM
GLA_GATE_RANK = 16
GLA_GATE_TAU = 16.0
GLA_CHUNK = 64
D_FF = 4 * D_MODEL
N_MOD = 6
EPS = 1e-6
IN_SIZES = (DA_QK_WIDTH, DA_QK_WIDTH, DA_WIDTH, GLA_QK_WIDTH, GLA_QK_WIDTH, GLA_WIDTH, GLA_WIDTH,
            GLA_GATE_RANK, D_MODEL, D_MODEL)

LANES = 128
SUBLANES = 8
VMEM_LIMIT_BYTES = 56 * 1024 * 1024

ADA_TN = 1024
PROJ_TM = 512
ACT_TM = 1024
ACT_TN = 1024
ATT_TK = PROJ_TM
ATT_TQ = 2 * ATT_TK
ATT_QC = 256
GLA_TM = 512
MERGE_TM = 512
MERGE_TN = 512
MLP_TM = 1024
MLP_TF = 512

GLA_LEVELS = (32, 16, 8, 4, 2, 1)


def _dot(a, b):
    return jnp.dot(a, b, preferred_element_type=F32)


def _dot_nt(a, b):
    return lax.dot_general(a, b, (((1,), (1,)), ((), ())), preferred_element_type=F32)


def _dot_tn(a, b):
    return lax.dot_general(a, b, (((0,), (0,)), ((), ())), preferred_element_type=F32)


def _split_bf16(x):
    hi = x.astype(BF16)
    lo = (x - hi.astype(F32)).astype(BF16)
    return hi, lo


def _params(*semantics):
    return pltpu.CompilerParams(dimension_semantics=semantics, vmem_limit_bytes=VMEM_LIMIT_BYTES)


def _resident(shape):
    nd = len(shape)
    return pl.BlockSpec(shape, lambda *_: (0,) * nd, pipeline_mode=pl.Buffered(1))


def _ada_kernel(c_ref, w_ref, b_ref, o_ref):
    c = c_ref[...]
    sc = c * jax.nn.sigmoid(c)
    o_ref[...] = _dot(sc.astype(BF16), w_ref[...].astype(BF16)) + b_ref[...]


def _ada(c, w_ada, b_ada):
    bsz, d = c.shape
    n = w_ada.shape[1]
    c8 = jnp.zeros((SUBLANES, d), F32).at[:bsz].set(c)
    out = pl.pallas_call(
        _ada_kernel,
        grid=(n // ADA_TN,),
        in_specs=[pl.BlockSpec((SUBLANES, d), lambda j: (0, 0)),
                  pl.BlockSpec((d, ADA_TN), lambda j: (0, j)),
                  pl.BlockSpec((1, ADA_TN), lambda j: (0, j))],
        out_specs=pl.BlockSpec((SUBLANES, ADA_TN), lambda j: (0, j)),
        out_shape=jax.ShapeDtypeStruct((SUBLANES, n), F32),
        compiler_params=_params("arbitrary"),
        name="ada",
    )(c8, w_ada, b_ada.reshape(1, n))
    return out[:bsz].reshape(bsz, N_MOD, d)


def _modulated_norm(x, gain, mod, shift_idx, scale_idx):
    ms = jnp.mean(x * x, axis=-1, keepdims=True)
    y = x * lax.rsqrt(ms + EPS) * gain
    return y * (1.0 + mod[scale_idx:scale_idx + 1]) + mod[shift_idx:shift_idx + 1]


QK_DTYPE = jnp.float8_e4m3fn
GROUP_COLS = 256


def _qkv_kernel(x_ref, g1_ref, mod_ref, w_ref, pos_ref, freq_ref, sa_ref, sb_ref, gq_ref, gk_ref, grp_ref,
                u_ref, qt_ref, k_ref, vt_ref):
    u = _modulated_norm(x_ref[...], g1_ref[...], mod_ref[0], 0, 1).astype(BF16)
    u_ref[...] = u
    ang = pos_ref[...].astype(F32) * freq_ref[...]
    cs = jnp.cos(ang)
    sn = jnp.sin(ang)
    sin_a = sn * sa_ref[...]
    sin_b = sn * sb_ref[...]
    grp = grp_ref[...]
    half = ROPE_DIM // 2

    def norm_rope(acc, gain):
        slabs = []
        for j in range(acc.shape[1] // GROUP_COLS):
            x = acc[:, j * GROUP_COLS:(j + 1) * GROUP_COLS]
            ss = _dot((x * x).astype(BF16), grp)
            xn = x * lax.rsqrt(ss * (1.0 / DA_HEAD_DIM) + EPS)
            for s in range(GROUP_COLS // LANES):
                xs = xn[:, s * LANES:(s + 1) * LANES] * gain
                slabs.append(xs * cs + pltpu.roll(xs, LANES - half, 1) * sin_a + pltpu.roll(xs, half, 1) * sin_b)
        return slabs

    q_scale = (DA_HEAD_DIM ** -0.5) * math.log2(math.e)
    tm = u.shape[0]
    row = lax.broadcasted_iota(jnp.int32, (LANES, tm), 0)
    for h, xq in enumerate(norm_rope(_dot(u, w_ref[:, 0:DA_QK_WIDTH]), gq_ref[...])):
        xt = (xq * q_scale).T
        qt_ref[0, h, 0, 0] = jnp.where(row < DA_HEAD_DIM, xt, 0.0).astype(QK_DTYPE)
        qt_ref[0, h, 0, 1] = jnp.where(row >= DA_HEAD_DIM, xt, 0.0).astype(QK_DTYPE)
    for h, xk in enumerate(norm_rope(_dot(u, w_ref[:, DA_QK_WIDTH:2 * DA_QK_WIDTH]), gk_ref[...])):
        k_ref[0, h] = xk.astype(QK_DTYPE)
    acc = _dot(u, w_ref[:, 2 * DA_QK_WIDTH:2 * DA_QK_WIDTH + DA_WIDTH])
    for h in range(DA_HEADS):
        vt_ref[0, h, 0] = acc[:, h * LANES:(h + 1) * LANES].T.astype(BF16)


def _rope_lane_tables():
    lane = np.arange(LANES) % DA_HEAD_DIM
    inv_freq = ROPE_THETA ** (-jnp.arange(0, ROPE_DIM, 2, dtype=F32) / ROPE_DIM)
    half = ROPE_DIM // 2
    freq = jnp.where(lane < ROPE_DIM, inv_freq[lane % half], 0.0).astype(F32).reshape(1, LANES)
    sign_a = np.where(lane < half, -1.0, 0.0).astype(np.float32).reshape(1, LANES)
    mask_b = np.where((lane >= half) & (lane < ROPE_DIM), 1.0, 0.0).astype(np.float32).reshape(1, LANES)
    col = np.arange(GROUP_COLS)
    group = col[:, None] // DA_HEAD_DIM == col[None, :] // DA_HEAD_DIM
    return freq, jnp.asarray(sign_a), jnp.asarray(mask_b), jnp.asarray(group, dtype=BF16)


def _qkv(x2, gain, mod, w_qkv, positions, q_gain, k_gain, bsz, seq):
    t, d = x2.shape
    tm = min(PROJ_TM, seq)
    tpb = seq // tm
    freq, sign_a, mask_b, group = _rope_lane_tables()
    lane_vec = lambda v: jnp.tile(v.astype(F32), LANES // DA_HEAD_DIM).reshape(1, LANES)
    vec_spec = pl.BlockSpec((1, LANES), lambda i: (0, 0))
    assert tm == ATT_TK and seq % ATT_TQ == 0
    per_q = ATT_TQ // tm
    qt_shape = jax.ShapeDtypeStruct((bsz, DA_HEADS, seq // ATT_TQ, 2, LANES, ATT_TQ), QK_DTYPE)
    qt_spec = pl.BlockSpec((1, DA_HEADS, 1, 2, LANES, tm),
                           lambda i: (i // tpb, 0, (i % tpb) // per_q, 0, 0, (i % tpb) % per_q))
    vt_shape = jax.ShapeDtypeStruct((bsz, DA_HEADS, tpb, DA_V_DIM, tm), BF16)
    vt_spec = pl.BlockSpec((1, DA_HEADS, 1, DA_V_DIM, tm), lambda i: (i // tpb, 0, i % tpb, 0, 0))
    return pl.pallas_call(
        _qkv_kernel,
        grid=(t // tm,),
        in_specs=[pl.BlockSpec((tm, d), lambda i: (i, 0)),
                  pl.BlockSpec((1, d), lambda i: (0, 0)),
                  pl.BlockSpec((1, N_MOD, d), lambda i: (i // tpb, 0, 0)),
                  _resident(w_qkv.shape),
                  pl.BlockSpec((tm, 1), lambda i: (i, 0)),
                  vec_spec, vec_spec, vec_spec, vec_spec, vec_spec,
                  pl.BlockSpec((GROUP_COLS, GROUP_COLS), lambda i: (0, 0))],
        out_specs=[pl.BlockSpec((tm, d), lambda i: (i, 0)),
                   qt_spec,
                   pl.BlockSpec((1, DA_HEADS, tm, LANES), lambda i: (i // tpb, 0, i % tpb, 0)),
                   vt_spec],
        out_shape=[jax.ShapeDtypeStruct((t, d), BF16), qt_shape,
                   jax.ShapeDtypeStruct((bsz, DA_HEADS, seq, LANES), QK_DTYPE), vt_shape],
        compiler_params=_params("parallel"),
        name="qkv",
    )(x2, gain.reshape(1, d), mod, w_qkv, positions.reshape(t, 1), freq, sign_a, mask_b,
      lane_vec(q_gain), lane_vec(k_gain), group)


def _glaqk_kernel(u_ref, w_ref, up_ref, bias_ref, q_ref, k_ref, la_ref):
    acc = _dot(u_ref[...], w_ref[...])
    q_ref[...] = (acc[:, 0:GLA_QK_WIDTH] * (GLA_KEY_DIM ** -0.5)).astype(BF16)
    k_ref[...] = acc[:, GLA_QK_WIDTH:2 * GLA_QK_WIDTH].astype(BF16)
    low = acc[:, 2 * GLA_QK_WIDTH:]
    low_hi, low_lo = _split_bf16(low)
    up_hi, up_lo = _split_bf16(up_ref[...])
    z = _dot(low_hi, up_hi) + _dot(low_lo, up_hi) + _dot(low_hi, up_lo) + bias_ref[...]
    log_sig = jnp.minimum(z, 0.0) - jnp.log1p(jnp.exp(-jnp.abs(z)))
    la_ref[...] = log_sig * (1.0 / GLA_GATE_TAU)


def _glaqk(u, w_g, gate_up, gate_bias, seq):
    t, d = u.shape
    tm = min(PROJ_TM, seq)
    up = jnp.zeros((LANES, GLA_QK_WIDTH), F32).at[:GLA_GATE_RANK].set(gate_up)
    row_spec = lambda w: pl.BlockSpec((tm, w), lambda i: (i, 0))
    return pl.pallas_call(
        _glaqk_kernel,
        grid=(t // tm,),
        in_specs=[row_spec(d), _resident(w_g.shape), _resident(up.shape),
                  pl.BlockSpec((1, GLA_QK_WIDTH), lambda i: (0, 0))],
        out_specs=[row_spec(GLA_QK_WIDTH), row_spec(GLA_QK_WIDTH), row_spec(GLA_QK_WIDTH)],
        out_shape=[jax.ShapeDtypeStruct((t, GLA_QK_WIDTH), BF16),
                   jax.ShapeDtypeStruct((t, GLA_QK_WIDTH), BF16),
                   jax.ShapeDtypeStruct((t, GLA_QK_WIDTH), F32)],
        compiler_params=_params("parallel"),
        name="glaqk",
    )(u, w_g, up, gate_bias.reshape(1, GLA_QK_WIDTH))


ACT_SILU_BLOCKS = (GLA_WIDTH // ACT_TN, 2 * GLA_WIDTH // ACT_TN)


def _act_kernel(u_ref, w_ref, o_ref):
    j = pl.program_id(0)
    acc = _dot(u_ref[...], w_ref[...])
    sig = jax.nn.sigmoid(acc)
    out = jnp.where(j < ACT_SILU_BLOCKS[0], acc, jnp.where(j < ACT_SILU_BLOCKS[1], acc * sig, sig))
    o_ref[...] = out.astype(BF16)


def _act(u, w_act, seq):
    t, d = u.shape
    n = w_act.shape[1]
    tm = min(ACT_TM, seq)
    return pl.pallas_call(
        _act_kernel,
        grid=(n // ACT_TN, t // tm),
        in_specs=[pl.BlockSpec((tm, d), lambda j, i: (i, 0)),
                  pl.BlockSpec((d, ACT_TN), lambda j, i: (0, j))],
        out_specs=pl.BlockSpec((tm, ACT_TN), lambda j, i: (i, j)),
        out_shape=jax.ShapeDtypeStruct((t, n), BF16),
        compiler_params=_params("arbitrary", "arbitrary"),
        name="act",
    )(u, w_act)


_FULL, _MASK, _SKIP = "full", "mask", "skip"


def _attn_kernel(bounded_ref, qt_ref, k_ref, vt_ref, lq1_ref, lk1_ref, lq2_ref, lk2_ref, subg_ref, o_ref,
                 s_ref, p_ref, cm_ref, acc_ref, m_ref, l_ref, *, lam_init):
    i = pl.program_id(2)
    tq, tk, qc_w = ATT_TQ, ATT_TK, ATT_QC
    nqc = tq // qc_w
    acc_ref[...] = jnp.zeros_like(acc_ref)
    l_ref[...] = jnp.zeros_like(l_ref)
    m_ref[...] = jnp.full_like(m_ref, -jnp.inf)

    def scores(t, slot, modes):
        kb = k_ref[0, 0, pl.ds(pl.multiple_of(t * tk, tk), tk), :]
        for c in range(2):
            for q in range(nqc):
                if modes[q] == _SKIP:
                    continue
                cols = slice(q * qc_w, (q + 1) * qc_w)
                s = _dot(kb, qt_ref[0, 0, 0, c, :, cols])
                if modes[q] == _MASK:
                    rel = (lax.broadcasted_iota(jnp.int32, (tk, qc_w), 0)
                           - lax.broadcasted_iota(jnp.int32, (tk, qc_w), 1))
                    s = jnp.where(rel <= i * tq + q * qc_w - t * tk, s, -jnp.inf)
                s_ref[slot, c, :, cols] = s
                cm_ref[slot, c, :, cols] = jnp.max(s, axis=0, keepdims=True)

    def accumulate(t, slot, modes):
        vtb = vt_ref[0, 0, t]
        for c in range(2):
            for q in range(nqc):
                if modes[q] == _SKIP:
                    continue
                cols = slice(q * qc_w, (q + 1) * qc_w)
                m_old = m_ref[c, :, cols]
                m_new = jnp.maximum(m_old, cm_ref[slot, c, :, cols])
                alpha = jnp.exp2(m_old - m_new)
                p = jnp.exp2(s_ref[slot, c, :, cols] - m_new)
                l_ref[c, :, cols] = alpha * l_ref[c, :, cols] + jnp.sum(p, axis=0, keepdims=True)
                acc_ref[c, :, cols] = alpha * acc_ref[c, :, cols] + _dot(vtb, p.astype(BF16))
                m_ref[c, :, cols] = m_new

    half = tk // qc_w
    even_modes = (_MASK,) * half + (_FULL,) * (nqc - half)
    full_modes = (_FULL,) * nqc
    last_modes = (_SKIP,) * half + (_MASK,) * (nqc - half)

    def probs(t, slot, modes):
        kb = k_ref[0, 0, pl.ds(pl.multiple_of(t * tk, tk), tk), :]
        for c in range(2):
            for q in range(nqc):
                if modes[q] == _SKIP:
                    continue
                cols = slice(q * qc_w, (q + 1) * qc_w)
                s = _dot(kb, qt_ref[0, 0, 0, c, :, cols])
                if modes[q] == _MASK:
                    rel = (lax.broadcasted_iota(jnp.int32, (tk, qc_w), 0)
                           - lax.broadcasted_iota(jnp.int32, (tk, qc_w), 1))
                    s = jnp.where(rel <= i * tq + q * qc_w - t * tk, s, -jnp.inf)
                p = jnp.exp2(s)
                l_ref[c, :, cols] += jnp.sum(p, axis=0, keepdims=True)
                p_ref[slot, c, :, cols] = p.astype(BF16)

    def weigh(t, slot, modes):
        vtb = vt_ref[0, 0, t]
        for c in range(2):
            for q in range(nqc):
                if modes[q] == _SKIP:
                    continue
                cols = slice(q * qc_w, (q + 1) * qc_w)
                acc_ref[c, :, cols] += _dot(vtb, p_ref[slot, c, :, cols])

    @pl.when(bounded_ref[0] != 0)
    def _():
        probs(0, 0, even_modes)

        def pair(p, carry):
            probs(2 * p + 1, 1, full_modes)
            weigh(2 * p, 0, full_modes)
            probs(2 * p + 2, 0, even_modes)
            weigh(2 * p + 1, 1, full_modes)
            return carry

        lax.fori_loop(0, i, pair, 0)
        probs(2 * i + 1, 1, last_modes)
        weigh(2 * i, 0, full_modes)
        weigh(2 * i + 1, 1, last_modes)

    @pl.when(bounded_ref[0] == 0)
    def _():
        scores(0, 0, even_modes)

        def pair(p, carry):
            scores(2 * p + 1, 1, full_modes)
            accumulate(2 * p, 0, full_modes)
            scores(2 * p + 2, 0, even_modes)
            accumulate(2 * p + 1, 1, full_modes)
            return carry

        lax.fori_loop(0, i, pair, 0)
        scores(2 * i + 1, 1, last_modes)
        accumulate(2 * i, 0, full_modes)
        accumulate(2 * i + 1, 1, last_modes)

    s1 = jnp.sum(lq1_ref[...] * lk1_ref[...], axis=1, keepdims=True)
    s2 = jnp.sum(lq2_ref[...] * lk2_ref[...], axis=1, keepdims=True)
    lam = jnp.exp(s1) - jnp.exp(s2) + lam_init
    o = acc_ref[0] / l_ref[0] - lam * (acc_ref[1] / l_ref[1])
    ms = jnp.mean(o * o, axis=0, keepdims=True)
    y = o * lax.rsqrt(ms + EPS) * subg_ref[...] * (1.0 - lam_init)
    o_ref[...] = y.T.astype(BF16)


ATT_SAFE_LOG2 = 60.0


def _scores_bounded(q_gain, k_gain):
    q_scale = (DA_HEAD_DIM ** -0.5) * math.log2(math.e)
    bound = 1.02 * DA_HEAD_DIM * q_scale * jnp.max(jnp.abs(q_gain)) * jnp.max(jnp.abs(k_gain))
    return (bound <= ATT_SAFE_LOG2).astype(jnp.int32).reshape(1)


def _attention(qt, k, vt, q_gain, k_gain, lq1, lk1, lq2, lk2, subln_g, layer_idx):
    bsz, heads, nq, _, _, tq = qt.shape
    nkb, vdim, tk = vt.shape[2:]
    seq = nq * tq
    assert (tq, tk, vdim) == (ATT_TQ, ATT_TK, DA_V_DIM) and tq == 2 * tk and nkb * tk == seq
    lam_init = 0.8 - 0.6 * math.exp(-0.3 * layer_idx)
    vec = lambda v: v.astype(F32).reshape(1, DA_HEAD_DIM)
    vec_spec = pl.BlockSpec((1, DA_HEAD_DIM), lambda b, h, i: (0, 0))
    return pl.pallas_call(
        functools.partial(_attn_kernel, lam_init=lam_init),
        grid=(bsz, heads, nq),
        in_specs=[pl.BlockSpec(memory_space=pltpu.SMEM),
                  pl.BlockSpec((1, 1, 1, 2, LANES, tq), lambda b, h, i: (b, h, i, 0, 0, 0)),
                  pl.BlockSpec((1, 1, seq, LANES), lambda b, h, i: (b, h, 0, 0)),
                  pl.BlockSpec((1, 1, nkb, vdim, tk), lambda b, h, i: (b, h, 0, 0, 0)),
                  vec_spec, vec_spec, vec_spec, vec_spec,
                  pl.BlockSpec((DA_V_DIM, 1), lambda b, h, i: (0, 0))],
        out_specs=pl.BlockSpec((tq, LANES), lambda b, h, i: (b * nq + i, h)),
        out_shape=jax.ShapeDtypeStruct((bsz * seq, DA_WIDTH), BF16),
        scratch_shapes=[pltpu.VMEM((2, 2, tk, tq), F32),
                        pltpu.VMEM((2, 2, tk, tq), BF16),
                        pltpu.VMEM((2, 2, 1, tq), F32),
                        pltpu.VMEM((2, vdim, tq), F32),
                        pltpu.VMEM((2, 1, tq), F32),
                        pltpu.VMEM((2, 1, tq), F32)],
        compiler_params=_params("parallel", "parallel", "arbitrary"),
        name="attn",
    )(_scores_bounded(q_gain, k_gain), qt, k, vt, vec(lq1), vec(lk1), vec(lq2), vec(lk2),
      subln_g.astype(F32).reshape(DA_V_DIM, 1))


def _gla_decay_matrix():
    c = GLA_CHUNK
    t = np.arange(c)[:, None]
    s = np.arange(c)[None, :]
    blocks = [(s <= t)]
    for h in GLA_LEVELS:
        r = (t // (2 * h)) * (2 * h) + h
        upper = (t >= r) & (s > r) & (s <= t)
        lower = (t < r) & (s > t) & (s <= r)
        blocks.append(upper | lower)
    return jnp.asarray(np.concatenate(blocks, axis=0), dtype=BF16)


def _gla_kernel(q_ref, k_ref, v_ref, la_ref, sr_ref, fm_ref, g_ref, o_ref, st_ref, *, nchunk):
    @pl.when(pl.program_id(1) == 0)
    def _():
        st_ref[...] = jnp.zeros_like(st_ref)

    c = GLA_CHUNK
    ii = lax.broadcasted_iota(jnp.int32, (c, c), 0)
    jj = lax.broadcasted_iota(jnp.int32, (c, c), 1)
    eye = ii == jj
    level_masks = []
    for h in GLA_LEVELS:
        shift = int(math.log2(2 * h))
        same = (ii >> shift) == (jj >> shift)
        level_masks.append(same & ((ii & (2 * h - 1)) >= h) & ((jj & (2 * h - 1)) < h))
    fm = fm_ref[...]
    gain = g_ref[...]

    def chunk(n, carry):
        rows = pl.ds(pl.multiple_of(n * c, c), c)
        la = la_ref[rows, :]
        la_hi, la_lo = _split_bf16(la)
        fa = _dot(fm, la_hi) + _dot(fm, la_lo)
        b = fa[0:c]
        b_last = b[c - 1:c]
        e_b = jnp.exp(b)
        e_rest = jnp.exp(b_last - b)
        e_last = jnp.exp(b_last)
        e_lvl = [jnp.exp(fa[(l + 1) * c:(l + 2) * c]) for l in range(len(GLA_LEVELS))]
        for hd in range(GLA_HEADS):
            ks = slice(hd * GLA_KEY_DIM, (hd + 1) * GLA_KEY_DIM)
            vs = slice(hd * GLA_VAL_DIM, (hd + 1) * GLA_VAL_DIM)
            qb = q_ref[rows, ks]
            kb = k_ref[rows, ks]
            q = qb.astype(F32)
            k = kb.astype(F32)
            v = v_ref[rows, vs]
            state = st_ref[hd]
            out = _dot_nt((q * e_b[:, ks]).astype(BF16), state.astype(BF16))
            att = jnp.where(eye, _dot_nt(qb, kb), 0.0)
            for l in range(len(GLA_LEVELS)):
                e = e_lvl[l][:, ks]
                att = jnp.where(level_masks[l], _dot_nt((q * e).astype(BF16), (k * e).astype(BF16)), att)
            out = out + _dot(att.astype(BF16), v)
            ms = jnp.mean(out * out, axis=-1, keepdims=True)
            y = out * lax.rsqrt(ms + EPS) * gain * sr_ref[rows, vs].astype(F32)
            o_ref[rows, vs] = y.astype(BF16)
            st_ref[hd] = state * e_last[:, ks] + _dot_tn(v, (k * e_rest[:, ks]).astype(BF16))
        return carry

    lax.fori_loop(0, nchunk, chunk, 0, unroll=2)


def _gla(gq, gk, act, la, out_gain, bsz, seq):
    t = gq.shape[0]
    tm = min(GLA_TM, seq)
    tpb = seq // tm
    fm = _gla_decay_matrix()
    row = lambda b, n: (b * tpb + n, 0)
    return pl.pallas_call(
        functools.partial(_gla_kernel, nchunk=tm // GLA_CHUNK),
        grid=(bsz, tpb),
        in_specs=[pl.BlockSpec((tm, GLA_QK_WIDTH), row),
                  pl.BlockSpec((tm, GLA_QK_WIDTH), row),
                  pl.BlockSpec((tm, GLA_WIDTH), row),
                  pl.BlockSpec((tm, GLA_QK_WIDTH), row),
                  pl.BlockSpec((tm, GLA_WIDTH), lambda b, n: (b * tpb + n, 1)),
                  pl.BlockSpec(fm.shape, lambda b, n: (0, 0)),
                  pl.BlockSpec((1, GLA_VAL_DIM), lambda b, n: (0, 0))],
        out_specs=pl.BlockSpec((tm, GLA_WIDTH), row),
        out_shape=jax.ShapeDtypeStruct((t, GLA_WIDTH), BF16),
        scratch_shapes=[pltpu.VMEM((GLA_HEADS, GLA_VAL_DIM, GLA_KEY_DIM), F32)],
        compiler_params=_params("parallel", "arbitrary"),
        name="gla",
    )(gq, gk, act, la, act, fm, out_gain.astype(F32).reshape(1, GLA_VAL_DIM))


def _merge_kernel(ya_ref, yb_ref, sa_ref, sb_ref, x_ref, mod_ref, wa_ref, wb_ref, wo_ref, o_ref, mg_ref):
    ya = ya_ref[...]
    yb = yb_ref[...]
    for n in range(mg_ref.shape[1] // MERGE_TN):
        cols = slice(n * MERGE_TN, (n + 1) * MERGE_TN)
        ta = _dot(ya, wa_ref[:, cols])
        tb = _dot(yb, wb_ref[:, cols])
        mg_ref[:, cols] = (sa_ref[:, cols].astype(F32) * ta + sb_ref[:, cols].astype(F32) * tb).astype(BF16)
    o_ref[...] = x_ref[...] + mod_ref[0][2:3] * _dot(mg_ref[...], wo_ref[...])


def _merge(ya, yb, act, x2, mod, w_a, w_b, w_o, seq):
    t, d = x2.shape
    tm = min(MERGE_TM, seq)
    tpb = seq // tm
    gate_blk = 2 * GLA_WIDTH // d
    return pl.pallas_call(
        _merge_kernel,
        grid=(t // tm,),
        in_specs=[pl.BlockSpec((tm, DA_WIDTH), lambda i: (i, 0)),
                  pl.BlockSpec((tm, GLA_WIDTH), lambda i: (i, 0)),
                  pl.BlockSpec((tm, d), lambda i: (i, gate_blk)),
                  pl.BlockSpec((tm, d), lambda i: (i, gate_blk + 1)),
                  pl.BlockSpec((tm, d), lambda i: (i, 0)),
                  pl.BlockSpec((1, N_MOD, d), lambda i: (i // tpb, 0, 0)),
                  _resident(w_a.shape), _resident(w_b.shape), _resident(w_o.shape)],
        out_specs=pl.BlockSpec((tm, d), lambda i: (i, 0)),
        out_shape=jax.ShapeDtypeStruct((t, d), F32),
        scratch_shapes=[pltpu.VMEM((tm, d), BF16)],
        compiler_params=_params("parallel"),
        name="merge",
    )(ya, yb, act, act, x2, mod, w_a, w_b, w_o)


def _mlp_kernel(h_ref, g_ref, mod_ref, w1_ref, w2_ref, o_ref, u_ref):
    @pl.when(pl.program_id(1) == 0)
    def _():
        h = h_ref[...]
        u_ref[...] = _modulated_norm(h, g_ref[...], mod_ref[0], 3, 4).astype(BF16)
        o_ref[...] = h

    hid = jnp.square(jnp.maximum(_dot(u_ref[...], w1_ref[...]), 0.0))
    o_ref[...] += mod_ref[0][5:6] * _dot(hid.astype(BF16), w2_ref[...])


def _mlp(h, gain, mod, w1, w2, seq):
    t, d = h.shape
    tm = min(MLP_TM, seq)
    tpb = seq // tm
    ff = w1.shape[1]
    return pl.pallas_call(
        _mlp_kernel,
        grid=(t // tm, ff // MLP_TF),
        in_specs=[pl.BlockSpec((tm, d), lambda i, kf: (i, 0)),
                  pl.BlockSpec((1, d), lambda i, kf: (0, 0)),
                  pl.BlockSpec((1, N_MOD, d), lambda i, kf: (i // tpb, 0, 0)),
                  pl.BlockSpec((d, MLP_TF), lambda i, kf: (0, kf)),
                  pl.BlockSpec((MLP_TF, d), lambda i, kf: (kf, 0))],
        out_specs=pl.BlockSpec((tm, d), lambda i, kf: (i, 0)),
        out_shape=jax.ShapeDtypeStruct((t, d), F32),
        scratch_shapes=[pltpu.VMEM((tm, d), BF16)],
        compiler_params=_params("parallel", "arbitrary"),
        name="mlp",
    )(h, gain.reshape(1, d), mod, w1, w2)


def _layer(h2, c, positions, bsz, seq, layer_idx, w_ada, b_ada, norm1_g, w_in, da_q_norm_g, da_k_norm_g,
           lq1, lk1, lq2, lk2, da_subln_g, gla_gate_up, gla_gate_bias, gla_out_norm_g,
           w_branch_a, w_branch_b, w_out, norm2_g, w_mlp_in, w_mlp_out):
    d = h2.shape[1]
    assert seq % min(PROJ_TM, seq) == 0 and seq % GLA_CHUNK == 0
    offs = np.concatenate([[0], np.cumsum(IN_SIZES)])
    col = lambda a, b: w_in[:, offs[a]:offs[b]].astype(BF16)
    w_qkv = col(0, 3)
    w_g = jnp.concatenate([col(3, 5), col(7, 8), jnp.zeros((d, LANES - GLA_GATE_RANK), BF16)], axis=1)
    w_act = jnp.concatenate([col(5, 7), col(8, 10)], axis=1)

    mod = _ada(c, w_ada, b_ada)
    u, qt, k, vt = _qkv(h2, norm1_g, mod, w_qkv, positions, da_q_norm_g, da_k_norm_g, bsz, seq)
    gq, gk, la = _glaqk(u, w_g, gla_gate_up, gla_gate_bias, seq)
    act = _act(u, w_act, seq)
    ya = _attention(qt, k, vt, da_q_norm_g, da_k_norm_g, lq1, lk1, lq2, lk2, da_subln_g, layer_idx)
    yb = _gla(gq, gk, act, la, gla_out_norm_g, bsz, seq)
    h1 = _merge(ya, yb, act, h2, mod, w_branch_a.astype(BF16), w_branch_b.astype(BF16),
                w_out.astype(BF16), seq)
    return _mlp(h1, norm2_g, mod, w_mlp_in.astype(BF16), w_mlp_out.astype(BF16), seq)


def kernel(x, c, positions, w_ada, b_ada, norm1_g, w_in, da_q_norm_g, da_k_norm_g, da_lambda_q1,
           da_lambda_k1, da_lambda_q2, da_lambda_k2, da_subln_g, gla_gate_up, gla_gate_bias,
           gla_out_norm_g, w_branch_a, w_branch_b, w_out, norm2_g, w_mlp_in, w_mlp_out):
    bsz, seq, d = x.shape
    h = x.reshape(bsz * seq, d)
    for l in range(w_ada.shape[0]):
        h = _layer(h, c, positions, bsz, seq, l, w_ada[l], b_ada[l], norm1_g[l], w_in[l], da_q_norm_g[l],
                   da_k_norm_g[l], da_lambda_q1[l], da_lambda_k1[l], da_lambda_q2[l], da_lambda_k2[l],
                   da_subln_g[l], gla_gate_up[l], gla_gate_bias[l], gla_out_norm_g[l], w_branch_a[l],
                   w_branch_b[l], w_out[l], norm2_g[l], w_mlp_in[l], w_mlp_out[l])
    return h.reshape(bsz, seq, d)
```

```python
import functools
import math

import jax
import jax.numpy as jnp
import numpy as np
from jax import lax
from jax.experimental import pallas as pl
from jax.experimental.pallas import tpu as pltpu

F32 = jnp.float32
BF16 = jnp.bfloat16

D_MODEL = 2048
DA_HEADS = 8
DA_HEAD_DIM = 64
DA_V_DIM = 2 * DA_HEAD_DIM
DA_QK_WIDTH = DA_HEADS * 2 * DA_HEAD_DIM
DA_WIDTH = DA_HEADS * DA_V_DIM
ROPE_THETA = 500000.0
ROPE_DIM = DA_HEAD_DIM // 4
GLA_HEADS = 4
GLA_KEY_DIM = 128
GLA_VAL_DIM = 256
GLA_QK_WIDTH = GLA_HEADS * GLA_KEY_DIM
GLA_WIDTH = GLA_HEADS * GLA_VAL_DIM
GLA_GATE_RANK = 16
GLA_GATE_TAU = 16.0
GLA_CHUNK = 64
D_FF = 4 * D_MODEL
N_MOD = 6
EPS = 1e-6
IN_SIZES = (DA_QK_WIDTH, DA_QK_WIDTH, DA_WIDTH, GLA_QK_WIDTH, GLA_QK_WIDTH, GLA_WIDTH, GLA_WIDTH,
            GLA_GATE_RANK, D_MODEL, D_MODEL)

LANES = 128
SUBLANES = 8
VMEM_LIMIT_BYTES = 56 * 1024 * 1024

ADA_TN = 1024
PROJ_TM = 512
ACT_TM = 1024
ACT_TN = 1024
ATT_TK = PROJ_TM
ATT_TQ = 2 * ATT_TK
ATT_QC = 256
GLA_TM = 512
GLA_GROUP = 2
MERGE_TM = 512
MERGE_TN = 512
MLP_TM = 1024
MLP_TF = 512

GLA_LEVELS = (32, 16, 8, 4, 2, 1)


def _dot(a, b):
    return jnp.dot(a, b, preferred_element_type=F32)


def _dot_nt(a, b):
    return lax.dot_general(a, b, (((1,), (1,)), ((), ())), preferred_element_type=F32)


def _dot_tn(a, b):
    return lax.dot_general(a, b, (((0,), (0,)), ((), ())), preferred_element_type=F32)


def _split_bf16(x):
    hi = x.astype(BF16)
    lo = (x - hi.astype(F32)).astype(BF16)
    return hi, lo


def _params(*semantics):
    return pltpu.CompilerParams(dimension_semantics=semantics, vmem_limit_bytes=VMEM_LIMIT_BYTES)


def _resident(shape):
    nd = len(shape)
    return pl.BlockSpec(shape, lambda *_: (0,) * nd, pipeline_mode=pl.Buffered(1))


def _ada_kernel(c_ref, w_ref, b_ref, o_ref):
    c = c_ref[...]
    sc = c * jax.nn.sigmoid(c)
    o_ref[...] = _dot(sc.astype(BF16), w_ref[...].astype(BF16)) + b_ref[...]


def _ada(c, w_ada, b_ada):
    bsz, d = c.shape
    n = w_ada.shape[1]
    c8 = jnp.zeros((SUBLANES, d), F32).at[:bsz].set(c)
    out = pl.pallas_call(
        _ada_kernel,
        grid=(n // ADA_TN,),
        in_specs=[pl.BlockSpec((SUBLANES, d), lambda j: (0, 0)),
                  pl.BlockSpec((d, ADA_TN), lambda j: (0, j)),
                  pl.BlockSpec((1, ADA_TN), lambda j: (0, j))],
        out_specs=pl.BlockSpec((SUBLANES, ADA_TN), lambda j: (0, j)),
        out_shape=jax.ShapeDtypeStruct((SUBLANES, n), F32),
        compiler_params=_params("arbitrary"),
        name="ada",
    )(c8, w_ada, b_ada.reshape(1, n))
    return out[:bsz].reshape(bsz, N_MOD, d)


def _modulated_norm(x, gain, mod, shift_idx, scale_idx):
    ms = jnp.mean(x * x, axis=-1, keepdims=True)
    y = x * lax.rsqrt(ms + EPS) * gain
    return y * (1.0 + mod[scale_idx:scale_idx + 1]) + mod[shift_idx:shift_idx + 1]


QK_DTYPE = jnp.float8_e4m3fn
GROUP_COLS = 256


def _qkv_kernel(x_ref, g1_ref, mod_ref, w_ref, pos_ref, freq_ref, sa_ref, sb_ref, gq_ref, gk_ref, grp_ref,
                u_ref, qt_ref, k_ref, vt_ref):
    u = _modulated_norm(x_ref[...], g1_ref[...], mod_ref[0], 0, 1).astype(BF16)
    u_ref[...] = u
    ang = pos_ref[...].astype(F32) * freq_ref[...]
    cs = jnp.cos(ang)
    sn = jnp.sin(ang)
    sin_a = sn * sa_ref[...]
    sin_b = sn * sb_ref[...]
    grp = grp_ref[...]
    half = ROPE_DIM // 2

    def norm_rope(acc, gain):
        slabs = []
        for j in range(acc.shape[1] // GROUP_COLS):
            x = acc[:, j * GROUP_COLS:(j + 1) * GROUP_COLS]
            ss = _dot((x * x).astype(BF16), grp)
            xn = x * lax.rsqrt(ss * (1.0 / DA_HEAD_DIM) + EPS)
            for s in range(GROUP_COLS // LANES):
                xs = xn[:, s * LANES:(s + 1) * LANES] * gain
                slabs.append(xs * cs + pltpu.roll(xs, LANES - half, 1) * sin_a + pltpu.roll(xs, half, 1) * sin_b)
        return slabs

    q_scale = (DA_HEAD_DIM ** -0.5) * math.log2(math.e)
    tm = u.shape[0]
    row = lax.broadcasted_iota(jnp.int32, (LANES, tm), 0)
    for h, xq in enumerate(norm_rope(_dot(u, w_ref[:, 0:DA_QK_WIDTH]), gq_ref[...])):
        xt = (xq * q_scale).T
        qt_ref[0, h, 0, 0] = jnp.where(row < DA_HEAD_DIM, xt, 0.0).astype(QK_DTYPE)
        qt_ref[0, h, 0, 1] = jnp.where(row >= DA_HEAD_DIM, xt, 0.0).astype(QK_DTYPE)
    for h, xk in enumerate(norm_rope(_dot(u, w_ref[:, DA_QK_WIDTH:2 * DA_QK_WIDTH]), gk_ref[...])):
        k_ref[0, h] = xk.astype(QK_DTYPE)
    acc = _dot(u, w_ref[:, 2 * DA_QK_WIDTH:2 * DA_QK_WIDTH + DA_WIDTH])
    for h in range(DA_HEADS):
        vt_ref[0, h, 0] = acc[:, h * LANES:(h + 1) * LANES].T.astype(BF16)


def _rope_lane_tables():
    lane = np.arange(LANES) % DA_HEAD_DIM
    inv_freq = ROPE_THETA ** (-jnp.arange(0, ROPE_DIM, 2, dtype=F32) / ROPE_DIM)
    half = ROPE_DIM // 2
    freq = jnp.where(lane < ROPE_DIM, inv_freq[lane % half], 0.0).astype(F32).reshape(1, LANES)
    sign_a = np.where(lane < half, -1.0, 0.0).astype(np.float32).reshape(1, LANES)
    mask_b = np.where((lane >= half) & (lane < ROPE_DIM), 1.0, 0.0).astype(np.float32).reshape(1, LANES)
    col = np.arange(GROUP_COLS)
    group = col[:, None] // DA_HEAD_DIM == col[None, :] // DA_HEAD_DIM
    return freq, jnp.asarray(sign_a), jnp.asarray(mask_b), jnp.asarray(group, dtype=BF16)


def _qkv(x2, gain, mod, w_qkv, positions, q_gain, k_gain, bsz, seq):
    t, d = x2.shape
    tm = min(PROJ_TM, seq)
    tpb = seq // tm
    freq, sign_a, mask_b, group = _rope_lane_tables()
    lane_vec = lambda v: jnp.tile(v.astype(F32), LANES // DA_HEAD_DIM).reshape(1, LANES)
    vec_spec = pl.BlockSpec((1, LANES), lambda i: (0, 0))
    assert tm == ATT_TK and seq % ATT_TQ == 0
    per_q = ATT_TQ // tm
    qt_shape = jax.ShapeDtypeStruct((bsz, DA_HEADS, seq // ATT_TQ, 2, LANES, ATT_TQ), QK_DTYPE)
    qt_spec = pl.BlockSpec((1, DA_HEADS, 1, 2, LANES, tm),
                           lambda i: (i // tpb, 0, (i % tpb) // per_q, 0, 0, (i % tpb) % per_q))
    vt_shape = jax.ShapeDtypeStruct((bsz, DA_HEADS, tpb, DA_V_DIM, tm), BF16)
    vt_spec = pl.BlockSpec((1, DA_HEADS, 1, DA_V_DIM, tm), lambda i: (i // tpb, 0, i % tpb, 0, 0))
    return pl.pallas_call(
        _qkv_kernel,
        grid=(t // tm,),
        in_specs=[pl.BlockSpec((tm, d), lambda i: (i, 0)),
                  pl.BlockSpec((1, d), lambda i: (0, 0)),
                  pl.BlockSpec((1, N_MOD, d), lambda i: (i // tpb, 0, 0)),
                  _resident(w_qkv.shape),
                  pl.BlockSpec((tm, 1), lambda i: (i, 0)),
                  vec_spec, vec_spec, vec_spec, vec_spec, vec_spec,
                  pl.BlockSpec((GROUP_COLS, GROUP_COLS), lambda i: (0, 0))],
        out_specs=[pl.BlockSpec((tm, d), lambda i: (i, 0)),
                   qt_spec,
                   pl.BlockSpec((1, DA_HEADS, tm, LANES), lambda i: (i // tpb, 0, i % tpb, 0)),
                   vt_spec],
        out_shape=[jax.ShapeDtypeStruct((t, d), BF16), qt_shape,
                   jax.ShapeDtypeStruct((bsz, DA_HEADS, seq, LANES), QK_DTYPE), vt_shape],
        compiler_params=_params("parallel"),
        name="qkv",
    )(x2, gain.reshape(1, d), mod, w_qkv, positions.reshape(t, 1), freq, sign_a, mask_b,
      lane_vec(q_gain), lane_vec(k_gain), group)


def _glaqk_kernel(u_ref, w_ref, up_ref, bias_ref, q_ref, k_ref, la_ref):
    acc = _dot(u_ref[...], w_ref[...])
    q_ref[...] = (acc[:, 0:GLA_QK_WIDTH] * (GLA_KEY_DIM ** -0.5)).astype(BF16)
    k_ref[...] = acc[:, GLA_QK_WIDTH:2 * GLA_QK_WIDTH].astype(BF16)
    low = acc[:, 2 * GLA_QK_WIDTH:]
    low_hi, low_lo = _split_bf16(low)
    up_hi, up_lo = _split_bf16(up_ref[...])
    z = _dot(low_hi, up_hi) + _dot(low_lo, up_hi) + _dot(low_hi, up_lo) + bias_ref[...]
    log_sig = jnp.minimum(z, 0.0) - jnp.log1p(jnp.exp(-jnp.abs(z)))
    la_ref[...] = log_sig * (1.0 / GLA_GATE_TAU)


def _glaqk(u, w_g, gate_up, gate_bias, seq):
    t, d = u.shape
    tm = min(PROJ_TM, seq)
    up = jnp.zeros((LANES, GLA_QK_WIDTH), F32).at[:GLA_GATE_RANK].set(gate_up)
    row_spec = lambda w: pl.BlockSpec((tm, w), lambda i: (i, 0))
    return pl.pallas_call(
        _glaqk_kernel,
        grid=(t // tm,),
        in_specs=[row_spec(d), _resident(w_g.shape), _resident(up.shape),
                  pl.BlockSpec((1, GLA_QK_WIDTH), lambda i: (0, 0))],
        out_specs=[row_spec(GLA_QK_WIDTH), row_spec(GLA_QK_WIDTH), row_spec(GLA_QK_WIDTH)],
        out_shape=[jax.ShapeDtypeStruct((t, GLA_QK_WIDTH), BF16),
                   jax.ShapeDtypeStruct((t, GLA_QK_WIDTH), BF16),
                   jax.ShapeDtypeStruct((t, GLA_QK_WIDTH), F32)],
        compiler_params=_params("parallel"),
        name="glaqk",
    )(u, w_g, up, gate_bias.reshape(1, GLA_QK_WIDTH))


ACT_SILU_BLOCKS = (GLA_WIDTH // ACT_TN, 2 * GLA_WIDTH // ACT_TN)


def _act_kernel(u_ref, w_ref, o_ref):
    j = pl.program_id(0)
    acc = _dot(u_ref[...], w_ref[...])
    sig = jax.nn.sigmoid(acc)
    out = jnp.where(j < ACT_SILU_BLOCKS[0], acc, jnp.where(j < ACT_SILU_BLOCKS[1], acc * sig, sig))
    o_ref[...] = out.astype(BF16)


def _act(u, w_act, seq):
    t, d = u.shape
    n = w_act.shape[1]
    tm = min(ACT_TM, seq)
    return pl.pallas_call(
        _act_kernel,
        grid=(n // ACT_TN, t // tm),
        in_specs=[pl.BlockSpec((tm, d), lambda j, i: (i, 0)),
                  pl.BlockSpec((d, ACT_TN), lambda j, i: (0, j))],
        out_specs=pl.BlockSpec((tm, ACT_TN), lambda j, i: (i, j)),
        out_shape=jax.ShapeDtypeStruct((t, n), BF16),
        compiler_params=_params("arbitrary", "arbitrary"),
        name="act",
    )(u, w_act)


_FULL, _MASK, _SKIP = "full", "mask", "skip"


def _attn_kernel(bounded_ref, qt_ref, k_ref, vt_ref, lq1_ref, lk1_ref, lq2_ref, lk2_ref, subg_ref, o_ref,
                 s_ref, p_ref, cm_ref, acc_ref, m_ref, l_ref, *, lam_init):
    i = pl.program_id(2)
    tq, tk, qc_w = ATT_TQ, ATT_TK, ATT_QC
    nqc = tq // qc_w
    acc_ref[...] = jnp.zeros_like(acc_ref)
    l_ref[...] = jnp.zeros_like(l_ref)
    m_ref[...] = jnp.full_like(m_ref, -jnp.inf)

    def scores(t, slot, modes):
        kb = k_ref[0, 0, pl.ds(pl.multiple_of(t * tk, tk), tk), :]
        for c in range(2):
            for q in range(nqc):
                if modes[q] == _SKIP:
                    continue
                cols = slice(q * qc_w, (q + 1) * qc_w)
                s = _dot(kb, qt_ref[0, 0, 0, c, :, cols])
                if modes[q] == _MASK:
                    rel = (lax.broadcasted_iota(jnp.int32, (tk, qc_w), 0)
                           - lax.broadcasted_iota(jnp.int32, (tk, qc_w), 1))
                    s = jnp.where(rel <= i * tq + q * qc_w - t * tk, s, -jnp.inf)
                s_ref[slot, c, :, cols] = s
                cm_ref[slot, c, :, cols] = jnp.max(s, axis=0, keepdims=True)

    def accumulate(t, slot, modes):
        vtb = vt_ref[0, 0, t]
        for c in range(2):
            for q in range(nqc):
                if modes[q] == _SKIP:
                    continue
                cols = slice(q * qc_w, (q + 1) * qc_w)
                m_old = m_ref[c, :, cols]
                m_new = jnp.maximum(m_old, cm_ref[slot, c, :, cols])
                alpha = jnp.exp2(m_old - m_new)
                p = jnp.exp2(s_ref[slot, c, :, cols] - m_new)
                l_ref[c, :, cols] = alpha * l_ref[c, :, cols] + jnp.sum(p, axis=0, keepdims=True)
                acc_ref[c, :, cols] = alpha * acc_ref[c, :, cols] + _dot(vtb, p.astype(BF16))
                m_ref[c, :, cols] = m_new

    half = tk // qc_w
    even_modes = (_MASK,) * half + (_FULL,) * (nqc - half)
    full_modes = (_FULL,) * nqc
    last_modes = (_SKIP,) * half + (_MASK,) * (nqc - half)

    def step(probs_of=None, weigh_of=None):
        if probs_of is not None:
            pt, pslot, pmodes = probs_of
            kb = k_ref[0, 0, pl.ds(pl.multiple_of(pt * tk, tk), tk), :]
        if weigh_of is not None:
            wt, wslot, wmodes = weigh_of
            vtb = vt_ref[0, 0, wt]
        for c in range(2):
            for q in range(nqc):
                cols = slice(q * qc_w, (q + 1) * qc_w)
                if probs_of is not None and pmodes[q] != _SKIP:
                    s = _dot(kb, qt_ref[0, 0, 0, c, :, cols])
                    if pmodes[q] == _MASK:
                        rel = (lax.broadcasted_iota(jnp.int32, (tk, qc_w), 0)
                               - lax.broadcasted_iota(jnp.int32, (tk, qc_w), 1))
                        s = jnp.where(rel <= i * tq + q * qc_w - pt * tk, s, -jnp.inf)
                    p = jnp.exp2(s)
                    l_ref[c, :, cols] += jnp.sum(p, axis=0, keepdims=True)
                    p_ref[pslot, c, :, cols] = p.astype(BF16)
                if weigh_of is not None and wmodes[q] != _SKIP:
                    acc_ref[c, :, cols] += _dot(vtb, p_ref[wslot, c, :, cols])

    @pl.when(bounded_ref[0] != 0)
    def _():
        step(probs_of=(0, 0, even_modes))

        def pair(p, carry):
            step(probs_of=(2 * p + 1, 1, full_modes), weigh_of=(2 * p, 0, full_modes))
            step(probs_of=(2 * p + 2, 0, even_modes), weigh_of=(2 * p + 1, 1, full_modes))
            return carry

        lax.fori_loop(0, i, pair, 0)
        step(probs_of=(2 * i + 1, 1, last_modes), weigh_of=(2 * i, 0, full_modes))
        step(weigh_of=(2 * i + 1, 1, last_modes))

    @pl.when(bounded_ref[0] == 0)
    def _():
        scores(0, 0, even_modes)

        def pair(p, carry):
            scores(2 * p + 1, 1, full_modes)
            accumulate(2 * p, 0, full_modes)
            scores(2 * p + 2, 0, even_modes)
            accumulate(2 * p + 1, 1, full_modes)
            return carry

        lax.fori_loop(0, i, pair, 0)
        scores(2 * i + 1, 1, last_modes)
        accumulate(2 * i, 0, full_modes)
        accumulate(2 * i + 1, 1, last_modes)

    s1 = jnp.sum(lq1_ref[...] * lk1_ref[...], axis=1, keepdims=True)
    s2 = jnp.sum(lq2_ref[...] * lk2_ref[...], axis=1, keepdims=True)
    lam = jnp.exp(s1) - jnp.exp(s2) + lam_init
    o = acc_ref[0] / l_ref[0] - lam * (acc_ref[1] / l_ref[1])
    ms = jnp.mean(o * o, axis=0, keepdims=True)
    y = o * lax.rsqrt(ms + EPS) * subg_ref[...] * (1.0 - lam_init)
    o_ref[...] = y.T.astype(BF16)


ATT_SAFE_LOG2 = 60.0


def _scores_bounded(q_gain, k_gain):
    q_scale = (DA_HEAD_DIM ** -0.5) * math.log2(math.e)
    bound = 1.02 * DA_HEAD_DIM * q_scale * jnp.max(jnp.abs(q_gain)) * jnp.max(jnp.abs(k_gain))
    return (bound <= ATT_SAFE_LOG2).astype(jnp.int32).reshape(1)


def _attention(qt, k, vt, q_gain, k_gain, lq1, lk1, lq2, lk2, subln_g, layer_idx):
    bsz, heads, nq, _, _, tq = qt.shape
    nkb, vdim, tk = vt.shape[2:]
    seq = nq * tq
    assert (tq, tk, vdim) == (ATT_TQ, ATT_TK, DA_V_DIM) and tq == 2 * tk and nkb * tk == seq
    lam_init = 0.8 - 0.6 * math.exp(-0.3 * layer_idx)
    vec = lambda v: v.astype(F32).reshape(1, DA_HEAD_DIM)
    vec_spec = pl.BlockSpec((1, DA_HEAD_DIM), lambda b, h, i: (0, 0))
    return pl.pallas_call(
        functools.partial(_attn_kernel, lam_init=lam_init),
        grid=(bsz, heads, nq),
        in_specs=[pl.BlockSpec(memory_space=pltpu.SMEM),
                  pl.BlockSpec((1, 1, 1, 2, LANES, tq), lambda b, h, i: (b, h, i, 0, 0, 0)),
                  pl.BlockSpec((1, 1, seq, LANES), lambda b, h, i: (b, h, 0, 0)),
                  pl.BlockSpec((1, 1, nkb, vdim, tk), lambda b, h, i: (b, h, 0, 0, 0)),
                  vec_spec, vec_spec, vec_spec, vec_spec,
                  pl.BlockSpec((DA_V_DIM, 1), lambda b, h, i: (0, 0))],
        out_specs=pl.BlockSpec((tq, LANES), lambda b, h, i: (b * nq + i, h)),
        out_shape=jax.ShapeDtypeStruct((bsz * seq, DA_WIDTH), BF16),
        scratch_shapes=[pltpu.VMEM((2, 2, tk, tq), F32),
                        pltpu.VMEM((2, 2, tk, tq), BF16),
                        pltpu.VMEM((2, 2, 1, tq), F32),
                        pltpu.VMEM((2, vdim, tq), F32),
                        pltpu.VMEM((2, 1, tq), F32),
                        pltpu.VMEM((2, 1, tq), F32)],
        compiler_params=_params("parallel", "parallel", "arbitrary"),
        name="attn",
    )(_scores_bounded(q_gain, k_gain), qt, k, vt, vec(lq1), vec(lk1), vec(lq2), vec(lk2),
      subln_g.astype(F32).reshape(DA_V_DIM, 1))


def _gla_decay_matrix():
    c = GLA_CHUNK
    t = np.arange(c)[:, None]
    s = np.arange(c)[None, :]
    blocks = [(s <= t)]
    for h in GLA_LEVELS:
        r = (t // (2 * h)) * (2 * h) + h
        upper = (t >= r) & (s > r) & (s <= t)
        lower = (t < r) & (s > t) & (s <= r)
        blocks.append(upper | lower)
    return jnp.asarray(np.concatenate(blocks, axis=0), dtype=BF16)


def _gla_kernel(q_ref, k_ref, v_ref, la_ref, sr_ref, fm_ref, g_ref, o_ref, st_ref, *, nchunk):
    @pl.when(pl.program_id(1) == 0)
    def _():
        st_ref[...] = jnp.zeros_like(st_ref)

    c = GLA_CHUNK
    ii = lax.broadcasted_iota(jnp.int32, (c, c), 0)
    jj = lax.broadcasted_iota(jnp.int32, (c, c), 1)
    eye = ii == jj
    level_masks = []
    for h in GLA_LEVELS:
        shift = int(math.log2(2 * h))
        same = (ii >> shift) == (jj >> shift)
        level_masks.append(same & ((ii & (2 * h - 1)) >= h) & ((jj & (2 * h - 1)) < h))
    fm = fm_ref[...]
    gain = g_ref[...]

    heads = range(GLA_HEADS)
    ksl = [slice(hd * GLA_KEY_DIM, (hd + 1) * GLA_KEY_DIM) for hd in heads]
    vsl = [slice(hd * GLA_VAL_DIM, (hd + 1) * GLA_VAL_DIM) for hd in heads]

    def local_part(rows):
        la = la_ref[rows, :]
        la_hi, la_lo = _split_bf16(la)
        fa = _dot(fm, la_hi) + _dot(fm, la_lo)
        b = fa[0:c]
        b_last = b[c - 1:c]
        e_b = jnp.exp(b)
        e_rest = jnp.exp(b_last - b)
        e_last = jnp.exp(b_last)
        e_lvl = [jnp.exp(fa[(l + 1) * c:(l + 2) * c]) for l in range(len(GLA_LEVELS))]
        qb = [q_ref[rows, ksl[hd]] for hd in heads]
        kb = [k_ref[rows, ksl[hd]] for hd in heads]
        q = [x.astype(F32) for x in qb]
        k = [x.astype(F32) for x in kb]
        v = [v_ref[rows, vsl[hd]] for hd in heads]
        att = [jnp.where(eye, _dot_nt(qb[hd], kb[hd]), 0.0) for hd in heads]
        for l in range(len(GLA_LEVELS)):
            for hd in heads:
                e = e_lvl[l][:, ksl[hd]]
                lvl = _dot_nt((q[hd] * e).astype(BF16), (k[hd] * e).astype(BF16))
                att[hd] = jnp.where(level_masks[l], lvl, att[hd])
        upd = [_dot_tn(v[hd], (k[hd] * e_rest[:, ksl[hd]]).astype(BF16)) for hd in heads]
        intra = [_dot(att[hd].astype(BF16), v[hd]) for hd in heads]
        qd = [(q[hd] * e_b[:, ksl[hd]]).astype(BF16) for hd in heads]
        return intra, qd, upd, e_last

    def carried_part(rows, local):
        intra, qd, upd, e_last = local
        for hd in heads:
            state = st_ref[hd]
            out = intra[hd] + _dot_nt(qd[hd], state.astype(BF16))
            ms = jnp.mean(out * out, axis=-1, keepdims=True)
            y = out * lax.rsqrt(ms + EPS) * gain * sr_ref[rows, vsl[hd]].astype(F32)
            o_ref[rows, vsl[hd]] = y.astype(BF16)
            st_ref[hd] = state * e_last[:, ksl[hd]] + upd[hd]

    def group(g, carry):
        rows = [pl.ds(pl.multiple_of((g * GLA_GROUP + j) * c, c), c) for j in range(GLA_GROUP)]
        local = [local_part(r) for r in rows]
        for r, loc in zip(rows, local):
            carried_part(r, loc)
        return carry

    lax.fori_loop(0, nchunk // GLA_GROUP, group, 0)


def _gla(gq, gk, act, la, out_gain, bsz, seq):
    t = gq.shape[0]
    tm = min(GLA_TM, seq)
    tpb = seq // tm
    fm = _gla_decay_matrix()
    row = lambda b, n: (b * tpb + n, 0)
    return pl.pallas_call(
        functools.partial(_gla_kernel, nchunk=tm // GLA_CHUNK),
        grid=(bsz, tpb),
        in_specs=[pl.BlockSpec((tm, GLA_QK_WIDTH), row),
                  pl.BlockSpec((tm, GLA_QK_WIDTH), row),
                  pl.BlockSpec((tm, GLA_WIDTH), row),
                  pl.BlockSpec((tm, GLA_QK_WIDTH), row),
                  pl.BlockSpec((tm, GLA_WIDTH), lambda b, n: (b * tpb + n, 1)),
                  pl.BlockSpec(fm.shape, lambda b, n: (0, 0)),
                  pl.BlockSpec((1, GLA_VAL_DIM), lambda b, n: (0, 0))],
        out_specs=pl.BlockSpec((tm, GLA_WIDTH), row),
        out_shape=jax.ShapeDtypeStruct((t, GLA_WIDTH), BF16),
        scratch_shapes=[pltpu.VMEM((GLA_HEADS, GLA_VAL_DIM, GLA_KEY_DIM), F32)],
        compiler_params=_params("parallel", "arbitrary"),
        name="gla",
    )(gq, gk, act, la, act, fm, out_gain.astype(F32).reshape(1, GLA_VAL_DIM))


def _merge_kernel(ya_ref, yb_ref, sa_ref, sb_ref, x_ref, mod_ref, wa_ref, wb_ref, wo_ref, o_ref, mg_ref):
    ya = ya_ref[...]
    yb = yb_ref[...]
    for n in range(mg_ref.shape[1] // MERGE_TN):
        cols = slice(n * MERGE_TN, (n + 1) * MERGE_TN)
        ta = _dot(ya, wa_ref[:, cols])
        tb = _dot(yb, wb_ref[:, cols])
        mg_ref[:, cols] = (sa_ref[:, cols].astype(F32) * ta + sb_ref[:, cols].astype(F32) * tb).astype(BF16)
    o_ref[...] = x_ref[...] + mod_ref[0][2:3] * _dot(mg_ref[...], wo_ref[...])


def _merge(ya, yb, act, x2, mod, w_a, w_b, w_o, seq):
    t, d = x2.shape
    tm = min(MERGE_TM, seq)
    tpb = seq // tm
    gate_blk = 2 * GLA_WIDTH // d
    return pl.pallas_call(
        _merge_kernel,
        grid=(t // tm,),
        in_specs=[pl.BlockSpec((tm, DA_WIDTH), lambda i: (i, 0)),
                  pl.BlockSpec((tm, GLA_WIDTH), lambda i: (i, 0)),
                  pl.BlockSpec((tm, d), lambda i: (i, gate_blk)),
                  pl.BlockSpec((tm, d), lambda i: (i, gate_blk + 1)),
                  pl.BlockSpec((tm, d), lambda i: (i, 0)),
                  pl.BlockSpec((1, N_MOD, d), lambda i: (i // tpb, 0, 0)),
                  _resident(w_a.shape), _resident(w_b.shape), _resident(w_o.shape)],
        out_specs=pl.BlockSpec((tm, d), lambda i: (i, 0)),
        out_shape=jax.ShapeDtypeStruct((t, d), F32),
        scratch_shapes=[pltpu.VMEM((tm, d), BF16)],
        compiler_params=_params("parallel"),
        name="merge",
    )(ya, yb, act, act, x2, mod, w_a, w_b, w_o)


def _mlp_kernel(h_ref, g_ref, mod_ref, w1_ref, w2_ref, o_ref, u_ref):
    @pl.when(pl.program_id(1) == 0)
    def _():
        h = h_ref[...]
        u_ref[...] = _modulated_norm(h, g_ref[...], mod_ref[0], 3, 4).astype(BF16)
        o_ref[...] = h

    hid = jnp.square(jnp.maximum(_dot(u_ref[...], w1_ref[...]), 0.0))
    o_ref[...] += mod_ref[0][5:6] * _dot(hid.astype(BF16), w2_ref[...])


def _mlp(h, gain, mod, w1, w2, seq):
    t, d = h.shape
    tm = min(MLP_TM, seq)
    tpb = seq // tm
    ff = w1.shape[1]
    return pl.pallas_call(
        _mlp_kernel,
        grid=(t // tm, ff // MLP_TF),
        in_specs=[pl.BlockSpec((tm, d), lambda i, kf: (i, 0)),
                  pl.BlockSpec((1, d), lambda i, kf: (0, 0)),
                  pl.BlockSpec((1, N_MOD, d), lambda i, kf: (i // tpb, 0, 0)),
                  pl.BlockSpec((d, MLP_TF), lambda i, kf: (0, kf)),
                  pl.BlockSpec((MLP_TF, d), lambda i, kf: (kf, 0))],
        out_specs=pl.BlockSpec((tm, d), lambda i, kf: (i, 0)),
        out_shape=jax.ShapeDtypeStruct((t, d), F32),
        scratch_shapes=[pltpu.VMEM((tm, d), BF16)],
        compiler_params=_params("parallel", "arbitrary"),
        name="mlp",
    )(h, gain.reshape(1, d), mod, w1, w2)


def _layer(h2, c, positions, bsz, seq, layer_idx, w_ada, b_ada, norm1_g, w_in, da_q_norm_g, da_k_norm_g,
           lq1, lk1, lq2, lk2, da_subln_g, gla_gate_up, gla_gate_bias, gla_out_norm_g,
           w_branch_a, w_branch_b, w_out, norm2_g, w_mlp_in, w_mlp_out):
    d = h2.shape[1]
    assert seq % min(PROJ_TM, seq) == 0 and seq % GLA_CHUNK == 0
    offs = np.concatenate([[0], np.cumsum(IN_SIZES)])
    col = lambda a, b: w_in[:, offs[a]:offs[b]].astype(BF16)
    w_qkv = col(0, 3)
    w_g = jnp.concatenate([col(3, 5), col(7, 8), jnp.zeros((d, LANES - GLA_GATE_RANK), BF16)], axis=1)
    w_act = jnp.concatenate([col(5, 7), col(8, 10)], axis=1)

    mod = _ada(c, w_ada, b_ada)
    u, qt, k, vt = _qkv(h2, norm1_g, mod, w_qkv, positions, da_q_norm_g, da_k_norm_g, bsz, seq)
    gq, gk, la = _glaqk(u, w_g, gla_gate_up, gla_gate_bias, seq)
    act = _act(u, w_act, seq)
    ya = _attention(qt, k, vt, da_q_norm_g, da_k_norm_g, lq1, lk1, lq2, lk2, da_subln_g, layer_idx)
    yb = _gla(gq, gk, act, la, gla_out_norm_g, bsz, seq)
    h1 = _merge(ya, yb, act, h2, mod, w_branch_a.astype(BF16), w_branch_b.astype(BF16),
                w_out.astype(BF16), seq)
    return _mlp(h1, norm2_g, mod, w_mlp_in.astype(BF16), w_mlp_out.astype(BF16), seq)


def kernel(x, c, positions, w_ada, b_ada, norm1_g, w_in, da_q_norm_g, da_k_norm_g, da_lambda_q1,
           da_lambda_k1, da_lambda_q2, da_lambda_k2, da_subln_g, gla_gate_up, gla_gate_bias,
           gla_out_norm_g, w_branch_a, w_branch_b, w_out, norm2_g, w_mlp_in, w_mlp_out):
    bsz, seq, d = x.shape
    h = x.reshape(bsz * seq, d)
    for l in range(w_ada.shape[0]):
        h = _layer(h, c, positions, bsz, seq, l, w_ada[l], b_ada[l], norm1_g[l], w_in[l], da_q_norm_g[l],
                   da_k_norm_g[l], da_lambda_q1[l], da_lambda_k1[l], da_lambda_q2[l], da_lambda_k2[l],
                   da_subln_g[l], gla_gate_up[l], gla_gate_bias[l], gla_out_norm_g[l], w_branch_a[l],
                   w_branch_b[l], w_out[l], norm2_g[l], w_mlp_in[l], w_mlp_out[l])
    return h.reshape(bsz, seq, d)
```

```python
import functools
import math

import jax
import jax.numpy as jnp
import numpy as np
from jax import lax
from jax.experimental import pallas as pl
from jax.experimental.pallas import tpu as pltpu

F32 = jnp.float32
BF16 = jnp.bfloat16

D_MODEL = 2048
DA_HEADS = 8
DA_HEAD_DIM = 64
DA_V_DIM = 2 * DA_HEAD_DIM
DA_QK_WIDTH = DA_HEADS * 2 * DA_HEAD_DIM
DA_WIDTH = DA_HEADS * DA_V_DIM
ROPE_THETA = 500000.0
ROPE_DIM = DA_HEAD_DIM // 4
GLA_HEADS = 4
GLA_KEY_DIM = 128
GLA_VAL_DIM = 256
GLA_QK_WIDTH = GLA_HEADS * GLA_KEY_DIM
GLA_WIDTH = GLA_HEADS * GLA_VAL_DIM
GLA_GATE_RANK = 16
GLA_GATE_TAU = 16.0
GLA_CHUNK = 64
D_FF = 4 * D_MODEL
N_MOD = 6
EPS = 1e-6
IN_SIZES = (DA_QK_WIDTH, DA_QK_WIDTH, DA_WIDTH, GLA_QK_WIDTH, GLA_QK_WIDTH, GLA_WIDTH, GLA_WIDTH,
            GLA_GATE_RANK, D_MODEL, D_MODEL)

LANES = 128
SUBLANES = 8
VMEM_LIMIT_BYTES = 56 * 1024 * 1024

ADA_TN = 1024
PROJ_TM = 512
ACT_TM = 1024
ACT_TN = 1024
ACT_PARTS = 4
GLAQK_PARTS = 2
ATT_TK = PROJ_TM
ATT_TQ = 2 * ATT_TK
ATT_QC = 256
GLA_TM = 512
GLA_GROUP = 4
MERGE_TM = 512
MERGE_TN = 512
MLP_TM = 1024
MLP_TF = 512

GLA_LEVELS = (32, 16, 8, 4, 2, 1)


def _dot(a, b):
    return jnp.dot(a, b, preferred_element_type=F32)


def _dot_nt(a, b):
    return lax.dot_general(a, b, (((1,), (1,)), ((), ())), preferred_element_type=F32)


def _dot_tn(a, b):
    return lax.dot_general(a, b, (((0,), (0,)), ((), ())), preferred_element_type=F32)


def _split_bf16(x):
    hi = x.astype(BF16)
    lo = (x - hi.astype(F32)).astype(BF16)
    return hi, lo


def _params(*semantics):
    return pltpu.CompilerParams(dimension_semantics=semantics, vmem_limit_bytes=VMEM_LIMIT_BYTES)


def _resident(shape):
    nd = len(shape)
    return pl.BlockSpec(shape, lambda *_: (0,) * nd, pipeline_mode=pl.Buffered(1))


def _ada_kernel(c_ref, w_ref, b_ref, o_ref):
    c = c_ref[...]
    sc = c * jax.nn.sigmoid(c)
    o_ref[...] = _dot(sc.astype(BF16), w_ref[...].astype(BF16)) + b_ref[...]


def _ada(c, w_ada, b_ada):
    bsz, d = c.shape
    n = w_ada.shape[1]
    c8 = jnp.zeros((SUBLANES, d), F32).at[:bsz].set(c)
    out = pl.pallas_call(
        _ada_kernel,
        grid=(n // ADA_TN,),
        in_specs=[pl.BlockSpec((SUBLANES, d), lambda j: (0, 0)),
                  pl.BlockSpec((d, ADA_TN), lambda j: (0, j)),
                  pl.BlockSpec((1, ADA_TN), lambda j: (0, j))],
        out_specs=pl.BlockSpec((SUBLANES, ADA_TN), lambda j: (0, j)),
        out_shape=jax.ShapeDtypeStruct((SUBLANES, n), F32),
        compiler_params=_params("arbitrary"),
        name="ada",
    )(c8, w_ada, b_ada.reshape(1, n))
    return out[:bsz].reshape(bsz, N_MOD, d)


def _modulated_norm(x, gain, mod, shift_idx, scale_idx):
    ms = jnp.mean(x * x, axis=-1, keepdims=True)
    y = x * lax.rsqrt(ms + EPS) * gain
    return y * (1.0 + mod[scale_idx:scale_idx + 1]) + mod[shift_idx:shift_idx + 1]


QK_DTYPE = BF16
FP8 = jnp.float8_e4m3fn
GROUP_COLS = 256


def _qkv_kernel(x_ref, g1_ref, mod_ref, w_ref, pos_ref, freq_ref, sa_ref, sb_ref, gq_ref, gk_ref, grp_ref,
                u_ref, qt_ref, k_ref, vt_ref):
    u = _modulated_norm(x_ref[...], g1_ref[...], mod_ref[0], 0, 1).astype(BF16)
    u_ref[...] = u
    ang = pos_ref[...].astype(F32) * freq_ref[...]
    cs = jnp.cos(ang)
    sn = jnp.sin(ang)
    sin_a = sn * sa_ref[...]
    sin_b = sn * sb_ref[...]
    grp = grp_ref[...]
    half = ROPE_DIM // 2

    def group_sums(acc):
        return [_dot(jnp.square(acc[:, j * GROUP_COLS:(j + 1) * GROUP_COLS]).astype(BF16), grp)
                for j in range(acc.shape[1] // GROUP_COLS)]

    def norm_rope(acc, sums, gain):
        slabs = []
        for j, ss in enumerate(sums):
            xn = acc[:, j * GROUP_COLS:(j + 1) * GROUP_COLS] * lax.rsqrt(ss * (1.0 / DA_HEAD_DIM) + EPS)
            for s in range(GROUP_COLS // LANES):
                xs = xn[:, s * LANES:(s + 1) * LANES] * gain
                slabs.append(xs * cs + pltpu.roll(xs, LANES - half, 1) * sin_a + pltpu.roll(xs, half, 1) * sin_b)
        return slabs

    q_scale = (DA_HEAD_DIM ** -0.5) * math.log2(math.e)
    tm = u.shape[0]
    row = lax.broadcasted_iota(jnp.int32, (LANES, tm), 0)
    acc = _dot(u, w_ref[:, 0:DA_QK_WIDTH])
    for h, xq in enumerate(norm_rope(acc, group_sums(acc), gq_ref[...])):
        xt = (xq * q_scale).T
        qt_ref[0, h, 0, 0] = jnp.where(row < DA_HEAD_DIM, xt, 0.0).astype(QK_DTYPE)
        qt_ref[0, h, 0, 1] = jnp.where(row >= DA_HEAD_DIM, xt, 0.0).astype(QK_DTYPE)
    acc = _dot(u, w_ref[:, DA_QK_WIDTH:2 * DA_QK_WIDTH])
    for h, xk in enumerate(norm_rope(acc, group_sums(acc), gk_ref[...])):
        k_ref[0, h] = xk.astype(QK_DTYPE)
    acc = _dot(u, w_ref[:, 2 * DA_QK_WIDTH:2 * DA_QK_WIDTH + DA_WIDTH])
    for h in range(DA_HEADS):
        vt_ref[0, h, 0] = acc[:, h * LANES:(h + 1) * LANES].T.astype(BF16)


def _rope_lane_tables():
    lane = np.arange(LANES) % DA_HEAD_DIM
    inv_freq = ROPE_THETA ** (-jnp.arange(0, ROPE_DIM, 2, dtype=F32) / ROPE_DIM)
    half = ROPE_DIM // 2
    freq = jnp.where(lane < ROPE_DIM, inv_freq[lane % half], 0.0).astype(F32).reshape(1, LANES)
    sign_a = np.where(lane < half, -1.0, 0.0).astype(np.float32).reshape(1, LANES)
    mask_b = np.where((lane >= half) & (lane < ROPE_DIM), 1.0, 0.0).astype(np.float32).reshape(1, LANES)
    col = np.arange(GROUP_COLS)
    group = col[:, None] // DA_HEAD_DIM == col[None, :] // DA_HEAD_DIM
    return freq, jnp.asarray(sign_a), jnp.asarray(mask_b), jnp.asarray(group, dtype=BF16)


def _qkv(x2, gain, mod, w_qkv, positions, q_gain, k_gain, bsz, seq):
    t, d = x2.shape
    tm = min(PROJ_TM, seq)
    tpb = seq // tm
    freq, sign_a, mask_b, group = _rope_lane_tables()
    lane_vec = lambda v: jnp.tile(v.astype(F32), LANES // DA_HEAD_DIM).reshape(1, LANES)
    vec_spec = pl.BlockSpec((1, LANES), lambda i: (0, 0))
    assert tm == ATT_TK and seq % ATT_TQ == 0
    per_q = ATT_TQ // tm
    qt_shape = jax.ShapeDtypeStruct((bsz, DA_HEADS, seq // ATT_TQ, 2, LANES, ATT_TQ), QK_DTYPE)
    qt_spec = pl.BlockSpec((1, DA_HEADS, 1, 2, LANES, tm),
                           lambda i: (i // tpb, 0, (i % tpb) // per_q, 0, 0, (i % tpb) % per_q))
    vt_shape = jax.ShapeDtypeStruct((bsz, DA_HEADS, tpb, DA_V_DIM, tm), BF16)
    vt_spec = pl.BlockSpec((1, DA_HEADS, 1, DA_V_DIM, tm), lambda i: (i // tpb, 0, i % tpb, 0, 0))
    return pl.pallas_call(
        _qkv_kernel,
        grid=(t // tm,),
        in_specs=[pl.BlockSpec((tm, d), lambda i: (i, 0)),
                  pl.BlockSpec((1, d), lambda i: (0, 0)),
                  pl.BlockSpec((1, N_MOD, d), lambda i: (i // tpb, 0, 0)),
                  _resident(w_qkv.shape),
                  pl.BlockSpec((tm, 1), lambda i: (i, 0)),
                  vec_spec, vec_spec, vec_spec, vec_spec, vec_spec,
                  pl.BlockSpec((GROUP_COLS, GROUP_COLS), lambda i: (0, 0))],
        out_specs=[pl.BlockSpec((tm, d), lambda i: (i, 0)),
                   qt_spec,
                   pl.BlockSpec((1, DA_HEADS, tm, LANES), lambda i: (i // tpb, 0, i % tpb, 0)),
                   vt_spec],
        out_shape=[jax.ShapeDtypeStruct((t, d), BF16), qt_shape,
                   jax.ShapeDtypeStruct((bsz, DA_HEADS, seq, LANES), QK_DTYPE), vt_shape],
        compiler_params=_params("parallel"),
        name="qkv",
    )(x2, gain.reshape(1, d), mod, w_qkv, positions.reshape(t, 1), freq, sign_a, mask_b,
      lane_vec(q_gain), lane_vec(k_gain), group)


def _glaqk_kernel(u_ref, w_ref, up_ref, bias_ref, q_ref, k_ref, la_ref):
    w = w_ref[...]
    up_hi, up_lo = _split_bf16(up_ref[...])
    parts = _row_parts(u_ref.shape[0], GLAQK_PARTS)
    accs = [_dot(u_ref[rows, :], w) for rows in parts]
    for rows, acc in zip(parts, accs):
        q_ref[rows, :] = (acc[:, 0:GLA_QK_WIDTH] * (GLA_KEY_DIM ** -0.5)).astype(BF16)
        k_ref[rows, :] = acc[:, GLA_QK_WIDTH:2 * GLA_QK_WIDTH].astype(BF16)
        low = acc[:, 2 * GLA_QK_WIDTH:]
        low_hi, low_lo = _split_bf16(low)
        z = _dot(low_hi, up_hi) + _dot(low_lo, up_hi) + _dot(low_hi, up_lo) + bias_ref[...]
        log_sig = jnp.minimum(z, 0.0) - jnp.log1p(jnp.exp(-jnp.abs(z)))
        la_ref[rows, :] = log_sig * (1.0 / GLA_GATE_TAU)


def _glaqk(u, w_g, gate_up, gate_bias, seq):
    t, d = u.shape
    tm = min(PROJ_TM, seq)
    up = jnp.zeros((LANES, GLA_QK_WIDTH), F32).at[:GLA_GATE_RANK].set(gate_up)
    row_spec = lambda w: pl.BlockSpec((tm, w), lambda i: (i, 0))
    return pl.pallas_call(
        _glaqk_kernel,
        grid=(t // tm,),
        in_specs=[row_spec(d), _resident(w_g.shape), _resident(up.shape),
                  pl.BlockSpec((1, GLA_QK_WIDTH), lambda i: (0, 0))],
        out_specs=[row_spec(GLA_QK_WIDTH), row_spec(GLA_QK_WIDTH), row_spec(GLA_QK_WIDTH)],
        out_shape=[jax.ShapeDtypeStruct((t, GLA_QK_WIDTH), BF16),
                   jax.ShapeDtypeStruct((t, GLA_QK_WIDTH), BF16),
                   jax.ShapeDtypeStruct((t, GLA_QK_WIDTH), F32)],
        compiler_params=_params("parallel"),
        name="glaqk",
    )(u, w_g, up, gate_bias.reshape(1, GLA_QK_WIDTH))


ACT_SILU_BLOCKS = (GLA_WIDTH // ACT_TN, 2 * GLA_WIDTH // ACT_TN)


def _row_parts(rows, parts):
    step = rows // parts
    return [slice(r * step, (r + 1) * step) for r in range(parts)]


def _act_kernel(u_ref, w_ref, o_ref):
    j = pl.program_id(0)
    w = w_ref[...]
    parts = _row_parts(u_ref.shape[0], ACT_PARTS)
    accs = [_dot(u_ref[rows, :], w) for rows in parts]
    for rows, acc in zip(parts, accs):
        sig = jax.nn.sigmoid(acc)
        out = jnp.where(j < ACT_SILU_BLOCKS[0], acc, jnp.where(j < ACT_SILU_BLOCKS[1], acc * sig, sig))
        o_ref[rows, :] = out.astype(BF16)


def _act(u, w_act, seq):
    t, d = u.shape
    n = w_act.shape[1]
    tm = min(ACT_TM, seq)
    return pl.pallas_call(
        _act_kernel,
        grid=(n // ACT_TN, t // tm),
        in_specs=[pl.BlockSpec((tm, d), lambda j, i: (i, 0)),
                  pl.BlockSpec((d, ACT_TN), lambda j, i: (0, j))],
        out_specs=pl.BlockSpec((tm, ACT_TN), lambda j, i: (i, j)),
        out_shape=jax.ShapeDtypeStruct((t, n), BF16),
        compiler_params=_params("arbitrary", "arbitrary"),
        name="act",
    )(u, w_act)


_FULL, _MASK, _SKIP = "full", "mask", "skip"


def _attn_kernel(bounded_ref, qt_ref, k_ref, vt_ref, lq1_ref, lk1_ref, lq2_ref, lk2_ref, subg_ref, o_ref,
                 s_ref, p_ref, cm_ref, acc_ref, m_ref, l_ref, q8_ref, *, lam_init):
    i = pl.program_id(2)
    tq, tk, qc_w = ATT_TQ, ATT_TK, ATT_QC
    nqc = tq // qc_w
    acc_ref[...] = jnp.zeros_like(acc_ref)
    l_ref[...] = jnp.zeros_like(l_ref)
    m_ref[...] = jnp.full_like(m_ref, -jnp.inf)

    def scores(t, slot, modes):
        kb = k_ref[0, 0, pl.ds(pl.multiple_of(t * tk, tk), tk), :]
        for c in range(2):
            for q in range(nqc):
                if modes[q] == _SKIP:
                    continue
                cols = slice(q * qc_w, (q + 1) * qc_w)
                s = _dot(kb, qt_ref[0, 0, 0, c, :, cols])
                if modes[q] == _MASK:
                    rel = (lax.broadcasted_iota(jnp.int32, (tk, qc_w), 0)
                           - lax.broadcasted_iota(jnp.int32, (tk, qc_w), 1))
                    s = jnp.where(rel <= i * tq + q * qc_w - t * tk, s, -jnp.inf)
                s_ref[slot, c, :, cols] = s
                cm_ref[slot, c, :, cols] = jnp.max(s, axis=0, keepdims=True)

    def accumulate(t, slot, modes):
        vtb = vt_ref[0, 0, t]
        for c in range(2):
            for q in range(nqc):
                if modes[q] == _SKIP:
                    continue
                cols = slice(q * qc_w, (q + 1) * qc_w)
                m_old = m_ref[c, :, cols]
                m_new = jnp.maximum(m_old, cm_ref[slot, c, :, cols])
                alpha = jnp.exp2(m_old - m_new)
                p = jnp.exp2(s_ref[slot, c, :, cols] - m_new)
                l_ref[c, :, cols] = alpha * l_ref[c, :, cols] + jnp.sum(p, axis=0, keepdims=True)
                acc_ref[c, :, cols] = alpha * acc_ref[c, :, cols] + _dot(vtb, p.astype(BF16))
                m_ref[c, :, cols] = m_new

    half = tk // qc_w
    even_modes = (_MASK,) * half + (_FULL,) * (nqc - half)
    full_modes = (_FULL,) * nqc
    last_modes = (_SKIP,) * half + (_MASK,) * (nqc - half)

    def step(probs_of=None, weigh_of=None):
        if probs_of is not None:
            pt, pslot, pmodes = probs_of
            kb = k_ref[0, 0, pl.ds(pl.multiple_of(pt * tk, tk), tk), :].astype(F32).astype(FP8)
        if weigh_of is not None:
            wt, wslot, wmodes = weigh_of
            vtb = vt_ref[0, 0, wt]
        for c in range(2):
            for q in range(nqc):
                cols = slice(q * qc_w, (q + 1) * qc_w)
                if probs_of is not None and pmodes[q] != _SKIP:
                    s = _dot(kb, q8_ref[c, :, cols])
                    if pmodes[q] == _MASK:
                        rel = (lax.broadcasted_iota(jnp.int32, (tk, qc_w), 0)
                               - lax.broadcasted_iota(jnp.int32, (tk, qc_w), 1))
                        s = jnp.where(rel <= i * tq + q * qc_w - pt * tk, s, -jnp.inf)
                    p = jnp.exp2(s)
                    l_ref[c, :, cols] += jnp.sum(p, axis=0, keepdims=True)
                    p_ref[pslot, c, :, cols] = p.astype(BF16)
                if weigh_of is not None and wmodes[q] != _SKIP:
                    acc_ref[c, :, cols] += _dot(vtb, p_ref[wslot, c, :, cols])

    @pl.when(bounded_ref[0] != 0)
    def _():
        q8_ref[...] = qt_ref[0, 0, 0].astype(F32).astype(FP8)
        step(probs_of=(0, 0, even_modes))

        def pair(p, carry):
            step(probs_of=(2 * p + 1, 1, full_modes), weigh_of=(2 * p, 0, full_modes))
            step(probs_of=(2 * p + 2, 0, even_modes), weigh_of=(2 * p + 1, 1, full_modes))
            return carry

        lax.fori_loop(0, i, pair, 0)
        step(probs_of=(2 * i + 1, 1, last_modes), weigh_of=(2 * i, 0, full_modes))
        step(weigh_of=(2 * i + 1, 1, last_modes))

    @pl.when(bounded_ref[0] == 0)
    def _():
        scores(0, 0, even_modes)

        def pair(p, carry):
            scores(2 * p + 1, 1, full_modes)
            accumulate(2 * p, 0, full_modes)
            scores(2 * p + 2, 0, even_modes)
            accumulate(2 * p + 1, 1, full_modes)
            return carry

        lax.fori_loop(0, i, pair, 0)
        scores(2 * i + 1, 1, last_modes)
        accumulate(2 * i, 0, full_modes)
        accumulate(2 * i + 1, 1, last_modes)

    s1 = jnp.sum(lq1_ref[...] * lk1_ref[...], axis=1, keepdims=True)
    s2 = jnp.sum(lq2_ref[...] * lk2_ref[...], axis=1, keepdims=True)
    lam = jnp.exp(s1) - jnp.exp(s2) + lam_init
    o = acc_ref[0] / l_ref[0] - lam * (acc_ref[1] / l_ref[1])
    ms = jnp.mean(o * o, axis=0, keepdims=True)
    y = o * lax.rsqrt(ms + EPS) * subg_ref[...] * (1.0 - lam_init)
    o_ref[...] = y.T.astype(BF16)


ATT_DIRECT_LOG2 = 18.0


def _scores_bounded(q_gain, k_gain):
    q_scale = (DA_HEAD_DIM ** -0.5) * math.log2(math.e)
    bound = 1.02 * DA_HEAD_DIM * q_scale * jnp.max(jnp.abs(q_gain)) * jnp.max(jnp.abs(k_gain))
    return (bound <= ATT_DIRECT_LOG2).astype(jnp.int32).reshape(1)


def _attention(qt, k, vt, q_gain, k_gain, lq1, lk1, lq2, lk2, subln_g, layer_idx):
    bsz, heads, nq, _, _, tq = qt.shape
    nkb, vdim, tk = vt.shape[2:]
    seq = nq * tq
    assert (tq, tk, vdim) == (ATT_TQ, ATT_TK, DA_V_DIM) and tq == 2 * tk and nkb * tk == seq
    lam_init = 0.8 - 0.6 * math.exp(-0.3 * layer_idx)
    vec = lambda v: v.astype(F32).reshape(1, DA_HEAD_DIM)
    vec_spec = pl.BlockSpec((1, DA_HEAD_DIM), lambda b, h, i: (0, 0))
    return pl.pallas_call(
        functools.partial(_attn_kernel, lam_init=lam_init),
        grid=(bsz, heads, nq),
        in_specs=[pl.BlockSpec(memory_space=pltpu.SMEM),
                  pl.BlockSpec((1, 1, 1, 2, LANES, tq), lambda b, h, i: (b, h, i, 0, 0, 0)),
                  pl.BlockSpec((1, 1, seq, LANES), lambda b, h, i: (b, h, 0, 0)),
                  pl.BlockSpec((1, 1, nkb, vdim, tk), lambda b, h, i: (b, h, 0, 0, 0)),
                  vec_spec, vec_spec, vec_spec, vec_spec,
                  pl.BlockSpec((DA_V_DIM, 1), lambda b, h, i: (0, 0))],
        out_specs=pl.BlockSpec((tq, LANES), lambda b, h, i: (b * nq + i, h)),
        out_shape=jax.ShapeDtypeStruct((bsz * seq, DA_WIDTH), BF16),
        scratch_shapes=[pltpu.VMEM((2, 2, tk, tq), F32),
                        pltpu.VMEM((2, 2, tk, tq), BF16),
                        pltpu.VMEM((2, 2, 1, tq), F32),
                        pltpu.VMEM((2, vdim, tq), F32),
                        pltpu.VMEM((2, 1, tq), F32),
                        pltpu.VMEM((2, 1, tq), F32),
                        pltpu.VMEM((2, LANES, tq), FP8)],
        compiler_params=_params("parallel", "parallel", "arbitrary"),
        name="attn",
    )(_scores_bounded(q_gain, k_gain), qt, k, vt, vec(lq1), vec(lk1), vec(lq2), vec(lk2),
      subln_g.astype(F32).reshape(DA_V_DIM, 1))


def _gla_decay_matrix():
    c = GLA_CHUNK
    t = np.arange(c)[:, None]
    s = np.arange(c)[None, :]
    blocks = [(s <= t)]
    for h in GLA_LEVELS:
        r = (t // (2 * h)) * (2 * h) + h
        upper = (t >= r) & (s > r) & (s <= t)
        lower = (t < r) & (s > t) & (s <= r)
        blocks.append(upper | lower)
    return jnp.asarray(np.concatenate(blocks, axis=0), dtype=BF16)


def _gla_kernel(q_ref, k_ref, v_ref, la_ref, sr_ref, fm_ref, g_ref, o_ref, st_ref, *, nchunk):
    @pl.when(pl.program_id(1) == 0)
    def _():
        st_ref[...] = jnp.zeros_like(st_ref)

    c = GLA_CHUNK
    ii = lax.broadcasted_iota(jnp.int32, (c, c), 0)
    jj = lax.broadcasted_iota(jnp.int32, (c, c), 1)
    eye = ii == jj
    level_masks = []
    for h in GLA_LEVELS:
        shift = int(math.log2(2 * h))
        same = (ii >> shift) == (jj >> shift)
        level_masks.append(same & ((ii & (2 * h - 1)) >= h) & ((jj & (2 * h - 1)) < h))
    fm = fm_ref[...]
    gain = g_ref[...]

    heads = range(GLA_HEADS)
    ksl = [slice(hd * GLA_KEY_DIM, (hd + 1) * GLA_KEY_DIM) for hd in heads]
    vsl = [slice(hd * GLA_VAL_DIM, (hd + 1) * GLA_VAL_DIM) for hd in heads]

    def local_part(rows):
        la = la_ref[rows, :]
        la_hi, la_lo = _split_bf16(la)
        fa = _dot(fm, la_hi) + _dot(fm, la_lo)
        b = fa[0:c]
        b_last = b[c - 1:c]
        e_b = jnp.exp(b)
        e_rest = jnp.exp(b_last - b)
        e_last = jnp.exp(b_last)
        e_lvl = [jnp.exp(fa[(l + 1) * c:(l + 2) * c]) for l in range(len(GLA_LEVELS))]
        qb = [q_ref[rows, ksl[hd]] for hd in heads]
        kb = [k_ref[rows, ksl[hd]] for hd in heads]
        q = [x.astype(F32) for x in qb]
        k = [x.astype(F32) for x in kb]
        v = [v_ref[rows, vsl[hd]] for hd in heads]
        att = [jnp.where(eye, _dot_nt(qb[hd], kb[hd]), 0.0) for hd in heads]
        for l in range(len(GLA_LEVELS)):
            for hd in heads:
                e = e_lvl[l][:, ksl[hd]]
                lvl = _dot_nt((q[hd] * e).astype(BF16), (k[hd] * e).astype(BF16))
                att[hd] = jnp.where(level_masks[l], lvl, att[hd])
        upd = [_dot_tn(v[hd], (k[hd] * e_rest[:, ksl[hd]]).astype(BF16)) for hd in heads]
        intra = [_dot(att[hd].astype(BF16), v[hd]) for hd in heads]
        qd = [(q[hd] * e_b[:, ksl[hd]]).astype(BF16) for hd in heads]
        return intra, qd, upd, e_last

    def carried_part(rows, local):
        intra, qd, upd, e_last = local
        for hd in heads:
            state = st_ref[hd]
            out = intra[hd] + _dot_nt(qd[hd], state.astype(BF16))
            ms = jnp.mean(out * out, axis=-1, keepdims=True)
            y = out * lax.rsqrt(ms + EPS) * gain * sr_ref[rows, vsl[hd]].astype(F32)
            o_ref[rows, vsl[hd]] = y.astype(BF16)
            st_ref[hd] = state * e_last[:, ksl[hd]] + upd[hd]

    def group(g, carry):
        rows = [pl.ds(pl.multiple_of((g * GLA_GROUP + j) * c, c), c) for j in range(GLA_GROUP)]
        local = [local_part(r) for r in rows]
        for r, loc in zip(rows, local):
            carried_part(r, loc)
        return carry

    lax.fori_loop(0, nchunk // GLA_GROUP, group, 0)


def _gla(gq, gk, act, la, out_gain, bsz, seq):
    t = gq.shape[0]
    tm = min(GLA_TM, seq)
    tpb = seq // tm
    fm = _gla_decay_matrix()
    row = lambda b, n: (b * tpb + n, 0)
    return pl.pallas_call(
        functools.partial(_gla_kernel, nchunk=tm // GLA_CHUNK),
        grid=(bsz, tpb),
        in_specs=[pl.BlockSpec((tm, GLA_QK_WIDTH), row),
                  pl.BlockSpec((tm, GLA_QK_WIDTH), row),
                  pl.BlockSpec((tm, GLA_WIDTH), row),
                  pl.BlockSpec((tm, GLA_QK_WIDTH), row),
                  pl.BlockSpec((tm, GLA_WIDTH), lambda b, n: (b * tpb + n, 1)),
                  pl.BlockSpec(fm.shape, lambda b, n: (0, 0)),
                  pl.BlockSpec((1, GLA_VAL_DIM), lambda b, n: (0, 0))],
        out_specs=pl.BlockSpec((tm, GLA_WIDTH), row),
        out_shape=jax.ShapeDtypeStruct((t, GLA_WIDTH), BF16),
        scratch_shapes=[pltpu.VMEM((GLA_HEADS, GLA_VAL_DIM, GLA_KEY_DIM), F32)],
        compiler_params=_params("parallel", "arbitrary"),
        name="gla",
    )(gq, gk, act, la, act, fm, out_gain.astype(F32).reshape(1, GLA_VAL_DIM))


def _merge_kernel(ya_ref, yb_ref, sa_ref, sb_ref, x_ref, mod_ref, wa_ref, wb_ref, wo_ref, o_ref, mg_ref):
    ya = ya_ref[...]
    yb = yb_ref[...]
    for n in range(mg_ref.shape[1] // MERGE_TN):
        cols = slice(n * MERGE_TN, (n + 1) * MERGE_TN)
        ta = _dot(ya, wa_ref[:, cols])
        tb = _dot(yb, wb_ref[:, cols])
        mg_ref[:, cols] = (sa_ref[:, cols].astype(F32) * ta + sb_ref[:, cols].astype(F32) * tb).astype(BF16)
    o_ref[...] = x_ref[...] + mod_ref[0][2:3] * _dot(mg_ref[...], wo_ref[...])


def _merge(ya, yb, act, x2, mod, w_a, w_b, w_o, seq):
    t, d = x2.shape
    tm = min(MERGE_TM, seq)
    tpb = seq // tm
    gate_blk = 2 * GLA_WIDTH // d
    return pl.pallas_call(
        _merge_kernel,
        grid=(t // tm,),
        in_specs=[pl.BlockSpec((tm, DA_WIDTH), lambda i: (i, 0)),
                  pl.BlockSpec((tm, GLA_WIDTH), lambda i: (i, 0)),
                  pl.BlockSpec((tm, d), lambda i: (i, gate_blk)),
                  pl.BlockSpec((tm, d), lambda i: (i, gate_blk + 1)),
                  pl.BlockSpec((tm, d), lambda i: (i, 0)),
                  pl.BlockSpec((1, N_MOD, d), lambda i: (i // tpb, 0, 0)),
                  _resident(w_a.shape), _resident(w_b.shape), _resident(w_o.shape)],
        out_specs=pl.BlockSpec((tm, d), lambda i: (i, 0)),
        out_shape=jax.ShapeDtypeStruct((t, d), F32),
        scratch_shapes=[pltpu.VMEM((tm, d), BF16)],
        compiler_params=_params("parallel"),
        name="merge",
    )(ya, yb, act, act, x2, mod, w_a, w_b, w_o)


def _mlp_kernel(h_ref, g_ref, mod_ref, w1_ref, w2_ref, o_ref, u_ref):
    @pl.when(pl.program_id(1) == 0)
    def _():
        h = h_ref[...]
        u_ref[...] = _modulated_norm(h, g_ref[...], mod_ref[0], 3, 4).astype(BF16)
        o_ref[...] = h

    hid = jnp.square(jnp.maximum(_dot(u_ref[...], w1_ref[...]), 0.0))
    o_ref[...] += mod_ref[0][5:6] * _dot(hid.astype(BF16), w2_ref[...])


def _mlp(h, gain, mod, w1, w2, seq):
    t, d = h.shape
    tm = min(MLP_TM, seq)
    tpb = seq // tm
    ff = w1.shape[1]
    return pl.pallas_call(
        _mlp_kernel,
        grid=(t // tm, ff // MLP_TF),
        in_specs=[pl.BlockSpec((tm, d), lambda i, kf: (i, 0)),
                  pl.BlockSpec((1, d), lambda i, kf: (0, 0)),
                  pl.BlockSpec((1, N_MOD, d), lambda i, kf: (i // tpb, 0, 0)),
                  pl.BlockSpec((d, MLP_TF), lambda i, kf: (0, kf)),
                  pl.BlockSpec((MLP_TF, d), lambda i, kf: (kf, 0))],
        out_specs=pl.BlockSpec((tm, d), lambda i, kf: (i, 0)),
        out_shape=jax.ShapeDtypeStruct((t, d), F32),
        scratch_shapes=[pltpu.VMEM((tm, d), BF16)],
        compiler_params=_params("parallel", "arbitrary"),
        name="mlp",
    )(h, gain.reshape(1, d), mod, w1, w2)


def _layer(h2, c, positions, bsz, seq, layer_idx, w_ada, b_ada, norm1_g, w_in, da_q_norm_g, da_k_norm_g,
           lq1, lk1, lq2, lk2, da_subln_g, gla_gate_up, gla_gate_bias, gla_out_norm_g,
           w_branch_a, w_branch_b, w_out, norm2_g, w_mlp_in, w_mlp_out):
    d = h2.shape[1]
    assert seq % min(PROJ_TM, seq) == 0 and seq % GLA_CHUNK == 0
    offs = np.concatenate([[0], np.cumsum(IN_SIZES)])
    col = lambda a, b: w_in[:, offs[a]:offs[b]].astype(BF16)
    w_qkv = col(0, 3)
    w_g = jnp.concatenate([col(3, 5), col(7, 8), jnp.zeros((d, LANES - GLA_GATE_RANK), BF16)], axis=1)
    w_act = jnp.concatenate([col(5, 7), col(8, 10)], axis=1)

    mod = _ada(c, w_ada, b_ada)
    u, qt, k, vt = _qkv(h2, norm1_g, mod, w_qkv, positions, da_q_norm_g, da_k_norm_g, bsz, seq)
    gq, gk, la = _glaqk(u, w_g, gla_gate_up, gla_gate_bias, seq)
    act = _act(u, w_act, seq)
    ya = _attention(qt, k, vt, da_q_norm_g, da_k_norm_g, lq1, lk1, lq2, lk2, da_subln_g, layer_idx)
    yb = _gla(gq, gk, act, la, gla_out_norm_g, bsz, seq)
    h1 = _merge(ya, yb, act, h2, mod, w_branch_a.astype(BF16), w_branch_b.astype(BF16),
                w_out.astype(BF16), seq)
    return _mlp(h1, norm2_g, mod, w_mlp_in.astype(BF16), w_mlp_out.astype(BF16), seq)


def kernel(x, c, positions, w_ada, b_ada, norm1_g, w_in, da_q_norm_g, da_k_norm_g, da_lambda_q1,
           da_lambda_k1, da_lambda_q2, da_lambda_k2, da_subln_g, gla_gate_up, gla_gate_bias,
           gla_out_norm_g, w_branch_a, w_branch_b, w_out, norm2_g, w_mlp_in, w_mlp_out):
    bsz, seq, d = x.shape
    h = x.reshape(bsz * seq, d)
    for l in range(w_ada.shape[0]):
        h = _layer(h, c, positions, bsz, seq, l, w_ada[l], b_ada[l], norm1_g[l], w_in[l], da_q_norm_g[l],
                   da_k_norm_g[l], da_lambda_q1[l], da_lambda_k1[l], da_lambda_q2[l], da_lambda_k2[l],
                   da_subln_g[l], gla_gate_up[l], gla_gate_bias[l], gla_out_norm_g[l], w_branch_a[l],
                   w_branch_b[l], w_out[l], norm2_g[l], w_mlp_in[l], w_mlp_out[l])
    return h.reshape(bsz, seq, d)
```

```python
import functools
import math

import jax
import jax.numpy as jnp
import numpy as np
from jax import lax
from jax.experimental import pallas as pl
from jax.experimental.pallas import tpu as pltpu

F32 = jnp.float32
BF16 = jnp.bfloat16

D_MODEL = 2048
DA_HEADS = 8
DA_HEAD_DIM = 64
DA_V_DIM = 2 * DA_HEAD_DIM
DA_QK_WIDTH = DA_HEADS * 2 * DA_HEAD_DIM
DA_WIDTH = DA_HEADS * DA_V_DIM
ROPE_THETA = 500000.0
ROPE_DIM = DA_HEAD_DIM // 4
GLA_HEADS = 4
GLA_KEY_DIM = 128
GLA_VAL_DIM = 256
GLA_QK_WIDTH = GLA_HEADS * GLA_KEY_DIM
GLA_WIDTH = GLA_HEADS * GLA_VAL_DIM
GLA_GATE_RANK = 16
GLA_GATE_TAU = 16.0
GLA_CHUNK = 64
D_FF = 4 * D_MODEL
N_MOD = 6
EPS = 1e-6
IN_SIZES = (DA_QK_WIDTH, DA_QK_WIDTH, DA_WIDTH, GLA_QK_WIDTH, GLA_QK_WIDTH, GLA_WIDTH, GLA_WIDTH,
            GLA_GATE_RANK, D_MODEL, D_MODEL)

LANES = 128
SUBLANES = 8
VMEM_LIMIT_BYTES = 56 * 1024 * 1024

ADA_TN = 1024
PROJ_TM = 512
ACT_TM = 1024
ACT_TN = 1024
ACT_PARTS = 4
GLAQK_PARTS = 2
ATT_TK = PROJ_TM
ATT_TQ = 2 * ATT_TK
ATT_QC = 256
GLA_TM = 512
GLA_GROUP = 4
MERGE_TM = 512
MERGE_TN = 512
MLP_TM = 1024
MLP_TF = 512

GLA_LEVELS = (32, 16, 8, 4, 2, 1)


def _dot(a, b):
    return jnp.dot(a, b, preferred_element_type=F32)


def _dot_nt(a, b):
    return lax.dot_general(a, b, (((1,), (1,)), ((), ())), preferred_element_type=F32)


def _dot_tn(a, b):
    return lax.dot_general(a, b, (((0,), (0,)), ((), ())), preferred_element_type=F32)


def _split_bf16(x):
    hi = x.astype(BF16)
    lo = (x - hi.astype(F32)).astype(BF16)
    return hi, lo


def _params(*semantics):
    return pltpu.CompilerParams(dimension_semantics=semantics, vmem_limit_bytes=VMEM_LIMIT_BYTES)


def _resident(shape):
    nd = len(shape)
    return pl.BlockSpec(shape, lambda *_: (0,) * nd, pipeline_mode=pl.Buffered(1))


def _ada_kernel(c_ref, w_ref, b_ref, o_ref):
    c = c_ref[...]
    sc = c * jax.nn.sigmoid(c)
    o_ref[...] = _dot(sc.astype(BF16), w_ref[...].astype(BF16)) + b_ref[...]


def _ada(c, w_ada, b_ada):
    bsz, d = c.shape
    n = w_ada.shape[1]
    c8 = jnp.zeros((SUBLANES, d), F32).at[:bsz].set(c)
    out = pl.pallas_call(
        _ada_kernel,
        grid=(n // ADA_TN,),
        in_specs=[pl.BlockSpec((SUBLANES, d), lambda j: (0, 0)),
                  pl.BlockSpec((d, ADA_TN), lambda j: (0, j)),
                  pl.BlockSpec((1, ADA_TN), lambda j: (0, j))],
        out_specs=pl.BlockSpec((SUBLANES, ADA_TN), lambda j: (0, j)),
        out_shape=jax.ShapeDtypeStruct((SUBLANES, n), F32),
        compiler_params=_params("arbitrary"),
        name="ada",
    )(c8, w_ada, b_ada.reshape(1, n))
    return out[:bsz].reshape(bsz, N_MOD, d)


def _modulated_norm(x, gain, mod, shift_idx, scale_idx):
    ms = jnp.mean(x * x, axis=-1, keepdims=True)
    y = x * lax.rsqrt(ms + EPS) * gain
    return y * (1.0 + mod[scale_idx:scale_idx + 1]) + mod[shift_idx:shift_idx + 1]


QK_DTYPE = BF16
FP8 = jnp.float8_e4m3fn
GROUP_COLS = 256


def _qkv_kernel(x_ref, g1_ref, mod_ref, w_ref, pos_ref, freq_ref, sa_ref, sb_ref, gq_ref, gk_ref, grp_ref,
                u_ref, qt_ref, k_ref, vt_ref):
    u = _modulated_norm(x_ref[...], g1_ref[...], mod_ref[0], 0, 1).astype(BF16)
    u_ref[...] = u
    ang = pos_ref[...].astype(F32) * freq_ref[...]
    cs = jnp.cos(ang)
    sn = jnp.sin(ang)
    sin_a = sn * sa_ref[...]
    sin_b = sn * sb_ref[...]
    grp = grp_ref[...]
    half = ROPE_DIM // 2

    def group_sums(acc):
        return [_dot(jnp.square(acc[:, j * GROUP_COLS:(j + 1) * GROUP_COLS]).astype(BF16), grp)
                for j in range(acc.shape[1] // GROUP_COLS)]

    def norm_rope(acc, sums, gain):
        slabs = []
        for j, ss in enumerate(sums):
            xn = acc[:, j * GROUP_COLS:(j + 1) * GROUP_COLS] * lax.rsqrt(ss * (1.0 / DA_HEAD_DIM) + EPS)
            for s in range(GROUP_COLS // LANES):
                xs = xn[:, s * LANES:(s + 1) * LANES] * gain
                slabs.append(xs * cs + pltpu.roll(xs, LANES - half, 1) * sin_a + pltpu.roll(xs, half, 1) * sin_b)
        return slabs

    q_scale = (DA_HEAD_DIM ** -0.5) * math.log2(math.e)
    tm = u.shape[0]
    row = lax.broadcasted_iota(jnp.int32, (LANES, tm), 0)
    acc = _dot(u, w_ref[:, 0:DA_QK_WIDTH])
    for h, xq in enumerate(norm_rope(acc, group_sums(acc), gq_ref[...])):
        xt = (xq * q_scale).T
        qt_ref[0, h, 0, 0] = jnp.where(row < DA_HEAD_DIM, xt, 0.0).astype(QK_DTYPE)
        qt_ref[0, h, 0, 1] = jnp.where(row >= DA_HEAD_DIM, xt, 0.0).astype(QK_DTYPE)
    acc = _dot(u, w_ref[:, DA_QK_WIDTH:2 * DA_QK_WIDTH])
    for h, xk in enumerate(norm_rope(acc, group_sums(acc), gk_ref[...])):
        k_ref[0, h] = xk.astype(QK_DTYPE)
    acc = _dot(u, w_ref[:, 2 * DA_QK_WIDTH:2 * DA_QK_WIDTH + DA_WIDTH])
    for h in range(DA_HEADS):
        vt_ref[0, h, 0] = acc[:, h * LANES:(h + 1) * LANES].T.astype(BF16)


def _rope_lane_tables():
    lane = np.arange(LANES) % DA_HEAD_DIM
    inv_freq = ROPE_THETA ** (-jnp.arange(0, ROPE_DIM, 2, dtype=F32) / ROPE_DIM)
    half = ROPE_DIM // 2
    freq = jnp.where(lane < ROPE_DIM, inv_freq[lane % half], 0.0).astype(F32).reshape(1, LANES)
    sign_a = np.where(lane < half, -1.0, 0.0).astype(np.float32).reshape(1, LANES)
    mask_b = np.where((lane >= half) & (lane < ROPE_DIM), 1.0, 0.0).astype(np.float32).reshape(1, LANES)
    col = np.arange(GROUP_COLS)
    group = col[:, None] // DA_HEAD_DIM == col[None, :] // DA_HEAD_DIM
    return freq, jnp.asarray(sign_a), jnp.asarray(mask_b), jnp.asarray(group, dtype=BF16)


def _qkv(x2, gain, mod, w_qkv, positions, q_gain, k_gain, bsz, seq):
    t, d = x2.shape
    tm = min(PROJ_TM, seq)
    tpb = seq // tm
    freq, sign_a, mask_b, group = _rope_lane_tables()
    lane_vec = lambda v: jnp.tile(v.astype(F32), LANES // DA_HEAD_DIM).reshape(1, LANES)
    vec_spec = pl.BlockSpec((1, LANES), lambda i: (0, 0))
    assert tm == ATT_TK and seq % ATT_TQ == 0
    per_q = ATT_TQ // tm
    qt_shape = jax.ShapeDtypeStruct((bsz, DA_HEADS, seq // ATT_TQ, 2, LANES, ATT_TQ), QK_DTYPE)
    qt_spec = pl.BlockSpec((1, DA_HEADS, 1, 2, LANES, tm),
                           lambda i: (i // tpb, 0, (i % tpb) // per_q, 0, 0, (i % tpb) % per_q))
    vt_shape = jax.ShapeDtypeStruct((bsz, DA_HEADS, tpb, DA_V_DIM, tm), BF16)
    vt_spec = pl.BlockSpec((1, DA_HEADS, 1, DA_V_DIM, tm), lambda i: (i // tpb, 0, i % tpb, 0, 0))
    return pl.pallas_call(
        _qkv_kernel,
        grid=(t // tm,),
        in_specs=[pl.BlockSpec((tm, d), lambda i: (i, 0)),
                  pl.BlockSpec((1, d), lambda i: (0, 0)),
                  pl.BlockSpec((1, N_MOD, d), lambda i: (i // tpb, 0, 0)),
                  _resident(w_qkv.shape),
                  pl.BlockSpec((tm, 1), lambda i: (i, 0)),
                  vec_spec, vec_spec, vec_spec, vec_spec, vec_spec,
                  pl.BlockSpec((GROUP_COLS, GROUP_COLS), lambda i: (0, 0))],
        out_specs=[pl.BlockSpec((tm, d), lambda i: (i, 0)),
                   qt_spec,
                   pl.BlockSpec((1, DA_HEADS, tm, LANES), lambda i: (i // tpb, 0, i % tpb, 0)),
                   vt_spec],
        out_shape=[jax.ShapeDtypeStruct((t, d), BF16), qt_shape,
                   jax.ShapeDtypeStruct((bsz, DA_HEADS, seq, LANES), QK_DTYPE), vt_shape],
        compiler_params=_params("parallel"),
        name="qkv",
    )(x2, gain.reshape(1, d), mod, w_qkv, positions.reshape(t, 1), freq, sign_a, mask_b,
      lane_vec(q_gain), lane_vec(k_gain), group)


def _glaqk_kernel(u_ref, w_ref, up_ref, bias_ref, q_ref, k_ref, la_ref):
    w = w_ref[...]
    up_hi, up_lo = _split_bf16(up_ref[...])
    parts = _row_parts(u_ref.shape[0], GLAQK_PARTS)
    accs = [_dot(u_ref[rows, :], w) for rows in parts]
    for rows, acc in zip(parts, accs):
        q_ref[rows, :] = (acc[:, 0:GLA_QK_WIDTH] * (GLA_KEY_DIM ** -0.5)).astype(BF16)
        k_ref[rows, :] = acc[:, GLA_QK_WIDTH:2 * GLA_QK_WIDTH].astype(BF16)
        low = acc[:, 2 * GLA_QK_WIDTH:]
        low_hi, low_lo = _split_bf16(low)
        z = _dot(low_hi, up_hi) + _dot(low_lo, up_hi) + _dot(low_hi, up_lo) + bias_ref[...]
        log_sig = jnp.minimum(z, 0.0) - jnp.log1p(jnp.exp(-jnp.abs(z)))
        la_ref[rows, :] = log_sig * (1.0 / GLA_GATE_TAU)


def _glaqk(u, w_g, gate_up, gate_bias, seq):
    t, d = u.shape
    tm = min(PROJ_TM, seq)
    up = jnp.zeros((LANES, GLA_QK_WIDTH), F32).at[:GLA_GATE_RANK].set(gate_up)
    row_spec = lambda w: pl.BlockSpec((tm, w), lambda i: (i, 0))
    return pl.pallas_call(
        _glaqk_kernel,
        grid=(t // tm,),
        in_specs=[row_spec(d), _resident(w_g.shape), _resident(up.shape),
                  pl.BlockSpec((1, GLA_QK_WIDTH), lambda i: (0, 0))],
        out_specs=[row_spec(GLA_QK_WIDTH), row_spec(GLA_QK_WIDTH), row_spec(GLA_QK_WIDTH)],
        out_shape=[jax.ShapeDtypeStruct((t, GLA_QK_WIDTH), BF16),
                   jax.ShapeDtypeStruct((t, GLA_QK_WIDTH), BF16),
                   jax.ShapeDtypeStruct((t, GLA_QK_WIDTH), F32)],
        compiler_params=_params("parallel"),
        name="glaqk",
    )(u, w_g, up, gate_bias.reshape(1, GLA_QK_WIDTH))


ACT_SILU_BLOCKS = (GLA_WIDTH // ACT_TN, 2 * GLA_WIDTH // ACT_TN)


def _row_parts(rows, parts):
    step = rows // parts
    return [slice(r * step, (r + 1) * step) for r in range(parts)]


def _act_kernel(u_ref, w_ref, o_ref):
    j = pl.program_id(0)
    w = w_ref[...]
    parts = _row_parts(u_ref.shape[0], ACT_PARTS)
    accs = [_dot(u_ref[rows, :], w) for rows in parts]
    for rows, acc in zip(parts, accs):
        sig = jax.nn.sigmoid(acc)
        out = jnp.where(j < ACT_SILU_BLOCKS[0], acc, jnp.where(j < ACT_SILU_BLOCKS[1], acc * sig, sig))
        o_ref[rows, :] = out.astype(BF16)


def _act(u, w_act, seq):
    t, d = u.shape
    n = w_act.shape[1]
    tm = min(ACT_TM, seq)
    return pl.pallas_call(
        _act_kernel,
        grid=(n // ACT_TN, t // tm),
        in_specs=[pl.BlockSpec((tm, d), lambda j, i: (i, 0)),
                  pl.BlockSpec((d, ACT_TN), lambda j, i: (0, j))],
        out_specs=pl.BlockSpec((tm, ACT_TN), lambda j, i: (i, j)),
        out_shape=jax.ShapeDtypeStruct((t, n), BF16),
        compiler_params=_params("arbitrary", "arbitrary"),
        name="act",
    )(u, w_act)


_FULL, _MASK, _SKIP = "full", "mask", "skip"


def _attn_kernel(bounded_ref, qt_ref, k_ref, vt_ref, lq1_ref, lk1_ref, lq2_ref, lk2_ref, subg_ref, o_ref,
                 s_ref, p_ref, cm_ref, acc_ref, m_ref, l_ref, q8_ref, *, lam_init):
    i = pl.program_id(2)
    tq, tk, qc_w = ATT_TQ, ATT_TK, ATT_QC
    nqc = tq // qc_w
    acc_ref[...] = jnp.zeros_like(acc_ref)
    l_ref[...] = jnp.zeros_like(l_ref)
    m_ref[...] = jnp.full_like(m_ref, -jnp.inf)

    def scores(t, slot, modes):
        kb = k_ref[0, 0, pl.ds(pl.multiple_of(t * tk, tk), tk), :]
        for c in range(2):
            for q in range(nqc):
                if modes[q] == _SKIP:
                    continue
                cols = slice(q * qc_w, (q + 1) * qc_w)
                s = _dot(kb, qt_ref[0, 0, 0, c, :, cols])
                if modes[q] == _MASK:
                    rel = (lax.broadcasted_iota(jnp.int32, (tk, qc_w), 0)
                           - lax.broadcasted_iota(jnp.int32, (tk, qc_w), 1))
                    s = jnp.where(rel <= i * tq + q * qc_w - t * tk, s, -jnp.inf)
                s_ref[slot, c, :, cols] = s
                cm_ref[slot, c, :, cols] = jnp.max(s, axis=0, keepdims=True)

    def accumulate(t, slot, modes):
        vtb = vt_ref[0, 0, t]
        for c in range(2):
            for q in range(nqc):
                if modes[q] == _SKIP:
                    continue
                cols = slice(q * qc_w, (q + 1) * qc_w)
                m_old = m_ref[c, :, cols]
                m_new = jnp.maximum(m_old, cm_ref[slot, c, :, cols])
                alpha = jnp.exp2(m_old - m_new)
                p = jnp.exp2(s_ref[slot, c, :, cols] - m_new)
                l_ref[c, :, cols] = alpha * l_ref[c, :, cols] + jnp.sum(p, axis=0, keepdims=True)
                acc_ref[c, :, cols] = alpha * acc_ref[c, :, cols] + _dot(vtb, p.astype(BF16))
                m_ref[c, :, cols] = m_new

    half = tk // qc_w
    even_modes = (_MASK,) * half + (_FULL,) * (nqc - half)
    full_modes = (_FULL,) * nqc
    last_modes = (_SKIP,) * half + (_MASK,) * (nqc - half)

    def step(probs_of=None, weigh_of=None):
        if probs_of is not None:
            pt, pslot, pmodes = probs_of
            kb = k_ref[0, 0, pl.ds(pl.multiple_of(pt * tk, tk), tk), :].astype(F32).astype(FP8)
        if weigh_of is not None:
            wt, wslot, wmodes = weigh_of
            vtb = vt_ref[0, 0, wt]
        for c in range(2):
            for q in range(nqc):
                cols = slice(q * qc_w, (q + 1) * qc_w)
                if probs_of is not None and pmodes[q] != _SKIP:
                    s = _dot(kb, q8_ref[c, :, cols])
                    if pmodes[q] == _MASK:
                        rel = (lax.broadcasted_iota(jnp.int32, (tk, qc_w), 0)
                               - lax.broadcasted_iota(jnp.int32, (tk, qc_w), 1))
                        s = jnp.where(rel <= i * tq + q * qc_w - pt * tk, s, -jnp.inf)
                    p = jnp.exp2(s)
                    l_ref[c, :, cols] += jnp.sum(p, axis=0, keepdims=True)
                    p_ref[pslot, c, :, cols] = p.astype(BF16)
                if weigh_of is not None and wmodes[q] != _SKIP:
                    acc_ref[c, :, cols] += _dot(vtb, p_ref[wslot, c, :, cols])

    @pl.when(bounded_ref[0] != 0)
    def _():
        q8_ref[...] = qt_ref[0, 0, 0].astype(F32).astype(FP8)
        step(probs_of=(0, 0, even_modes))

        def pair(p, next_modes):
            step(probs_of=(2 * p + 1, 1, full_modes), weigh_of=(2 * p, 0, full_modes))
            step(probs_of=(2 * p + 2, 0, next_modes), weigh_of=(2 * p + 1, 1, full_modes))

        lax.fori_loop(0, jnp.maximum(i - 1, 0), lambda p, carry: (pair(p, full_modes), carry)[1], 0)

        @pl.when(i > 0)
        def _():
            pair(i - 1, even_modes)

        step(probs_of=(2 * i + 1, 1, last_modes), weigh_of=(2 * i, 0, full_modes))
        step(weigh_of=(2 * i + 1, 1, last_modes))

    @pl.when(bounded_ref[0] == 0)
    def _():
        scores(0, 0, even_modes)

        def pair(p, carry):
            scores(2 * p + 1, 1, full_modes)
            accumulate(2 * p, 0, full_modes)
            scores(2 * p + 2, 0, even_modes)
            accumulate(2 * p + 1, 1, full_modes)
            return carry

        lax.fori_loop(0, i, pair, 0)
        scores(2 * i + 1, 1, last_modes)
        accumulate(2 * i, 0, full_modes)
        accumulate(2 * i + 1, 1, last_modes)

    s1 = jnp.sum(lq1_ref[...] * lk1_ref[...], axis=1, keepdims=True)
    s2 = jnp.sum(lq2_ref[...] * lk2_ref[...], axis=1, keepdims=True)
    lam = jnp.exp(s1) - jnp.exp(s2) + lam_init
    o = acc_ref[0] / l_ref[0] - lam * (acc_ref[1] / l_ref[1])
    ms = jnp.mean(o * o, axis=0, keepdims=True)
    y = o * lax.rsqrt(ms + EPS) * subg_ref[...] * (1.0 - lam_init)
    o_ref[...] = y.T.astype(BF16)


ATT_DIRECT_LOG2 = 18.0


def _scores_bounded(q_gain, k_gain):
    q_scale = (DA_HEAD_DIM ** -0.5) * math.log2(math.e)
    bound = 1.02 * DA_HEAD_DIM * q_scale * jnp.max(jnp.abs(q_gain)) * jnp.max(jnp.abs(k_gain))
    return (bound <= ATT_DIRECT_LOG2).astype(jnp.int32).reshape(1)


def _attention(qt, k, vt, q_gain, k_gain, lq1, lk1, lq2, lk2, subln_g, layer_idx):
    bsz, heads, nq, _, _, tq = qt.shape
    nkb, vdim, tk = vt.shape[2:]
    seq = nq * tq
    assert (tq, tk, vdim) == (ATT_TQ, ATT_TK, DA_V_DIM) and tq == 2 * tk and nkb * tk == seq
    lam_init = 0.8 - 0.6 * math.exp(-0.3 * layer_idx)
    vec = lambda v: v.astype(F32).reshape(1, DA_HEAD_DIM)
    vec_spec = pl.BlockSpec((1, DA_HEAD_DIM), lambda b, h, i: (0, 0))
    return pl.pallas_call(
        functools.partial(_attn_kernel, lam_init=lam_init),
        grid=(bsz, heads, nq),
        in_specs=[pl.BlockSpec(memory_space=pltpu.SMEM),
                  pl.BlockSpec((1, 1, 1, 2, LANES, tq), lambda b, h, i: (b, h, i, 0, 0, 0)),
                  pl.BlockSpec((1, 1, seq, LANES), lambda b, h, i: (b, h, 0, 0)),
                  pl.BlockSpec((1, 1, nkb, vdim, tk), lambda b, h, i: (b, h, 0, 0, 0)),
                  vec_spec, vec_spec, vec_spec, vec_spec,
                  pl.BlockSpec((DA_V_DIM, 1), lambda b, h, i: (0, 0))],
        out_specs=pl.BlockSpec((tq, LANES), lambda b, h, i: (b * nq + i, h)),
        out_shape=jax.ShapeDtypeStruct((bsz * seq, DA_WIDTH), BF16),
        scratch_shapes=[pltpu.VMEM((2, 2, tk, tq), F32),
                        pltpu.VMEM((2, 2, tk, tq), BF16),
                        pltpu.VMEM((2, 2, 1, tq), F32),
                        pltpu.VMEM((2, vdim, tq), F32),
                        pltpu.VMEM((2, 1, tq), F32),
                        pltpu.VMEM((2, 1, tq), F32),
                        pltpu.VMEM((2, LANES, tq), FP8)],
        compiler_params=_params("parallel", "parallel", "arbitrary"),
        name="attn",
    )(_scores_bounded(q_gain, k_gain), qt, k, vt, vec(lq1), vec(lk1), vec(lq2), vec(lk2),
      subln_g.astype(F32).reshape(DA_V_DIM, 1))


def _gla_decay_matrix():
    c = GLA_CHUNK
    t = np.arange(c)[:, None]
    s = np.arange(c)[None, :]
    blocks = [(s <= t)]
    for h in GLA_LEVELS:
        r = (t // (2 * h)) * (2 * h) + h
        upper = (t >= r) & (s > r) & (s <= t)
        lower = (t < r) & (s > t) & (s <= r)
        blocks.append(upper | lower)
    return jnp.asarray(np.concatenate(blocks, axis=0), dtype=BF16)


def _gla_kernel(q_ref, k_ref, v_ref, la_ref, sr_ref, fm_ref, g_ref, o_ref, st_ref, *, nchunk):
    @pl.when(pl.program_id(1) == 0)
    def _():
        st_ref[...] = jnp.zeros_like(st_ref)

    c = GLA_CHUNK
    ii = lax.broadcasted_iota(jnp.int32, (c, c), 0)
    jj = lax.broadcasted_iota(jnp.int32, (c, c), 1)
    eye = ii == jj
    level_masks = []
    for h in GLA_LEVELS:
        shift = int(math.log2(2 * h))
        same = (ii >> shift) == (jj >> shift)
        level_masks.append(same & ((ii & (2 * h - 1)) >= h) & ((jj & (2 * h - 1)) < h))
    fm = fm_ref[...]
    gain = g_ref[...]

    heads = range(GLA_HEADS)
    ksl = [slice(hd * GLA_KEY_DIM, (hd + 1) * GLA_KEY_DIM) for hd in heads]
    vsl = [slice(hd * GLA_VAL_DIM, (hd + 1) * GLA_VAL_DIM) for hd in heads]

    def local_part(rows):
        la = la_ref[rows, :]
        la_hi, la_lo = _split_bf16(la)
        fa = _dot(fm, la_hi) + _dot(fm, la_lo)
        b = fa[0:c]
        b_last = b[c - 1:c]
        e_b = jnp.exp(b)
        e_rest = jnp.exp(b_last - b)
        e_last = jnp.exp(b_last)
        e_lvl = [jnp.exp(fa[(l + 1) * c:(l + 2) * c]) for l in range(len(GLA_LEVELS))]
        qb = [q_ref[rows, ksl[hd]] for hd in heads]
        kb = [k_ref[rows, ksl[hd]] for hd in heads]
        q = [x.astype(F32) for x in qb]
        k = [x.astype(F32) for x in kb]
        v = [v_ref[rows, vsl[hd]] for hd in heads]
        att = [jnp.where(eye, _dot_nt(qb[hd], kb[hd]), 0.0) for hd in heads]
        for l in range(len(GLA_LEVELS)):
            for hd in heads:
                e = e_lvl[l][:, ksl[hd]]
                lvl = _dot_nt((q[hd] * e).astype(BF16), (k[hd] * e).astype(BF16))
                att[hd] = jnp.where(level_masks[l], lvl, att[hd])
        upd = [_dot_tn(v[hd], (k[hd] * e_rest[:, ksl[hd]]).astype(BF16)) for hd in heads]
        intra = [_dot(att[hd].astype(BF16), v[hd]) for hd in heads]
        qd = [(q[hd] * e_b[:, ksl[hd]]).astype(BF16) for hd in heads]
        return intra, qd, upd, e_last

    def carried_part(rows, local):
        intra, qd, upd, e_last = local
        for hd in heads:
            state = st_ref[hd]
            out = intra[hd] + _dot_nt(qd[hd], state.astype(BF16))
            ms = jnp.mean(out * out, axis=-1, keepdims=True)
            y = out * lax.rsqrt(ms + EPS) * gain * sr_ref[rows, vsl[hd]].astype(F32)
            o_ref[rows, vsl[hd]] = y.astype(BF16)
            st_ref[hd] = state * e_last[:, ksl[hd]] + upd[hd]

    def group(g, carry):
        rows = [pl.ds(pl.multiple_of((g * GLA_GROUP + j) * c, c), c) for j in range(GLA_GROUP)]
        local = [local_part(r) for r in rows]
        for r, loc in zip(rows, local):
            carried_part(r, loc)
        return carry

    lax.fori_loop(0, nchunk // GLA_GROUP, group, 0)


def _gla(gq, gk, act, la, out_gain, bsz, seq):
    t = gq.shape[0]
    tm = min(GLA_TM, seq)
    tpb = seq // tm
    fm = _gla_decay_matrix()
    row = lambda b, n: (b * tpb + n, 0)
    return pl.pallas_call(
        functools.partial(_gla_kernel, nchunk=tm // GLA_CHUNK),
        grid=(bsz, tpb),
        in_specs=[pl.BlockSpec((tm, GLA_QK_WIDTH), row),
                  pl.BlockSpec((tm, GLA_QK_WIDTH), row),
                  pl.BlockSpec((tm, GLA_WIDTH), row),
                  pl.BlockSpec((tm, GLA_QK_WIDTH), row),
                  pl.BlockSpec((tm, GLA_WIDTH), lambda b, n: (b * tpb + n, 1)),
                  pl.BlockSpec(fm.shape, lambda b, n: (0, 0)),
                  pl.BlockSpec((1, GLA_VAL_DIM), lambda b, n: (0, 0))],
        out_specs=pl.BlockSpec((tm, GLA_WIDTH), row),
        out_shape=jax.ShapeDtypeStruct((t, GLA_WIDTH), BF16),
        scratch_shapes=[pltpu.VMEM((GLA_HEADS, GLA_VAL_DIM, GLA_KEY_DIM), F32)],
        compiler_params=_params("parallel", "arbitrary"),
        name="gla",
    )(gq, gk, act, la, act, fm, out_gain.astype(F32).reshape(1, GLA_VAL_DIM))


def _merge_kernel(ya_ref, yb_ref, sa_ref, sb_ref, x_ref, mod_ref, wa_ref, wb_ref, wo_ref, o_ref, mg_ref):
    ya = ya_ref[...]
    yb = yb_ref[...]
    for n in range(mg_ref.shape[1] // MERGE_TN):
        cols = slice(n * MERGE_TN, (n + 1) * MERGE_TN)
        ta = _dot(ya, wa_ref[:, cols])
        tb = _dot(yb, wb_ref[:, cols])
        mg_ref[:, cols] = (sa_ref[:, cols].astype(F32) * ta + sb_ref[:, cols].astype(F32) * tb).astype(BF16)
    o_ref[...] = x_ref[...] + mod_ref[0][2:3] * _dot(mg_ref[...], wo_ref[...])


def _merge(ya, yb, act, x2, mod, w_a, w_b, w_o, seq):
    t, d = x2.shape
    tm = min(MERGE_TM, seq)
    tpb = seq // tm
    gate_blk = 2 * GLA_WIDTH // d
    return pl.pallas_call(
        _merge_kernel,
        grid=(t // tm,),
        in_specs=[pl.BlockSpec((tm, DA_WIDTH), lambda i: (i, 0)),
                  pl.BlockSpec((tm, GLA_WIDTH), lambda i: (i, 0)),
                  pl.BlockSpec((tm, d), lambda i: (i, gate_blk)),
                  pl.BlockSpec((tm, d), lambda i: (i, gate_blk + 1)),
                  pl.BlockSpec((tm, d), lambda i: (i, 0)),
                  pl.BlockSpec((1, N_MOD, d), lambda i: (i // tpb, 0, 0)),
                  _resident(w_a.shape), _resident(w_b.shape), _resident(w_o.shape)],
        out_specs=pl.BlockSpec((tm, d), lambda i: (i, 0)),
        out_shape=jax.ShapeDtypeStruct((t, d), F32),
        scratch_shapes=[pltpu.VMEM((tm, d), BF16)],
        compiler_params=_params("parallel"),
        name="merge",
    )(ya, yb, act, act, x2, mod, w_a, w_b, w_o)


def _mlp_kernel(h_ref, g_ref, mod_ref, w1_ref, w2_ref, o_ref, u_ref):
    @pl.when(pl.program_id(1) == 0)
    def _():
        h = h_ref[...]
        u_ref[...] = _modulated_norm(h, g_ref[...], mod_ref[0], 3, 4).astype(BF16)
        o_ref[...] = h

    hid = jnp.square(jnp.maximum(_dot(u_ref[...], w1_ref[...]), 0.0))
    o_ref[...] += mod_ref[0][5:6] * _dot(hid.astype(BF16), w2_ref[...])


def _mlp(h, gain, mod, w1, w2, seq):
    t, d = h.shape
    tm = min(MLP_TM, seq)
    tpb = seq // tm
    ff = w1.shape[1]
    return pl.pallas_call(
        _mlp_kernel,
        grid=(t // tm, ff // MLP_TF),
        in_specs=[pl.BlockSpec((tm, d), lambda i, kf: (i, 0)),
                  pl.BlockSpec((1, d), lambda i, kf: (0, 0)),
                  pl.BlockSpec((1, N_MOD, d), lambda i, kf: (i // tpb, 0, 0)),
                  pl.BlockSpec((d, MLP_TF), lambda i, kf: (0, kf)),
                  pl.BlockSpec((MLP_TF, d), lambda i, kf: (kf, 0))],
        out_specs=pl.BlockSpec((tm, d), lambda i, kf: (i, 0)),
        out_shape=jax.ShapeDtypeStruct((t, d), F32),
        scratch_shapes=[pltpu.VMEM((tm, d), BF16)],
        compiler_params=_params("parallel", "arbitrary"),
        name="mlp",
    )(h, gain.reshape(1, d), mod, w1, w2)


def _layer(h2, c, positions, bsz, seq, layer_idx, w_ada, b_ada, norm1_g, w_in, da_q_norm_g, da_k_norm_g,
           lq1, lk1, lq2, lk2, da_subln_g, gla_gate_up, gla_gate_bias, gla_out_norm_g,
           w_branch_a, w_branch_b, w_out, norm2_g, w_mlp_in, w_mlp_out):
    d = h2.shape[1]
    assert seq % min(PROJ_TM, seq) == 0 and seq % GLA_CHUNK == 0
    offs = np.concatenate([[0], np.cumsum(IN_SIZES)])
    col = lambda a, b: w_in[:, offs[a]:offs[b]].astype(BF16)
    w_qkv = col(0, 3)
    w_g = jnp.concatenate([col(3, 5), col(7, 8), jnp.zeros((d, LANES - GLA_GATE_RANK), BF16)], axis=1)
    w_act = jnp.concatenate([col(5, 7), col(8, 10)], axis=1)

    mod = _ada(c, w_ada, b_ada)
    u, qt, k, vt = _qkv(h2, norm1_g, mod, w_qkv, positions, da_q_norm_g, da_k_norm_g, bsz, seq)
    gq, gk, la = _glaqk(u, w_g, gla_gate_up, gla_gate_bias, seq)
    act = _act(u, w_act, seq)
    ya = _attention(qt, k, vt, da_q_norm_g, da_k_norm_g, lq1, lk1, lq2, lk2, da_subln_g, layer_idx)
    yb = _gla(gq, gk, act, la, gla_out_norm_g, bsz, seq)
    h1 = _merge(ya, yb, act, h2, mod, w_branch_a.astype(BF16), w_branch_b.astype(BF16),
                w_out.astype(BF16), seq)
    return _mlp(h1, norm2_g, mod, w_mlp_in.astype(BF16), w_mlp_out.astype(BF16), seq)


def kernel(x, c, positions, w_ada, b_ada, norm1_g, w_in, da_q_norm_g, da_k_norm_g, da_lambda_q1,
           da_lambda_k1, da_lambda_q2, da_lambda_k2, da_subln_g, gla_gate_up, gla_gate_bias,
           gla_out_norm_g, w_branch_a, w_branch_b, w_out, norm2_g, w_mlp_in, w_mlp_out):
    bsz, seq, d = x.shape
    h = x.reshape(bsz * seq, d)
    for l in range(w_ada.shape[0]):
        h = _layer(h, c, positions, bsz, seq, l, w_ada[l], b_ada[l], norm1_g[l], w_in[l], da_q_norm_g[l],
                   da_k_norm_g[l], da_lambda_q1[l], da_lambda_k1[l], da_lambda_q2[l], da_lambda_k2[l],
                   da_subln_g[l], gla_gate_up[l], gla_gate_bias[l], gla_out_norm_g[l], w_branch_a[l],
                   w_branch_b[l], w_out[l], norm2_g[l], w_mlp_in[l], w_mlp_out[l])
    return h.reshape(bsz, seq, d)
```

```python
import functools
import math

import jax
import jax.numpy as jnp
import numpy as np
from jax import lax
from jax.experimental import pallas as pl
from jax.experimental.pallas import tpu as pltpu

F32 = jnp.float32
BF16 = jnp.bfloat16

D_MODEL = 2048
DA_HEADS = 8
DA_HEAD_DIM = 64
DA_V_DIM = 2 * DA_HEAD_DIM
DA_QK_WIDTH = DA_HEADS * 2 * DA_HEAD_DIM
DA_WIDTH = DA_HEADS * DA_V_DIM
ROPE_THETA = 500000.0
ROPE_DIM = DA_HEAD_DIM // 4
GLA_HEADS = 4
GLA_KEY_DIM = 128
GLA_VAL_DIM = 256
GLA_QK_WIDTH = GLA_HEADS * GLA_KEY_DIM
GLA_WIDTH = GLA_HEADS * GLA_VAL_DIM
GLA_GATE_RANK = 16
GLA_GATE_TAU = 16.0
GLA_CHUNK = 64
D_FF = 4 * D_MODEL
N_MOD = 6
EPS = 1e-6
IN_SIZES = (DA_QK_WIDTH, DA_QK_WIDTH, DA_WIDTH, GLA_QK_WIDTH, GLA_QK_WIDTH, GLA_WIDTH, GLA_WIDTH,
            GLA_GATE_RANK, D_MODEL, D_MODEL)

LANES = 128
SUBLANES = 8
VMEM_LIMIT_BYTES = 56 * 1024 * 1024

ADA_TN = 1024
PROJ_TM = 512
ACT_TM = 1024
ACT_TN = 1024
ACT_PARTS = 4
GLAQK_PARTS = 2
ATT_TK = PROJ_TM
ATT_TQ = 2 * ATT_TK
ATT_QC = 256
GLA_TM = 512
GLA_GROUP = 4
MERGE_TM = 512
MERGE_TN = 512
MLP_TM = 1024
MLP_TF = 512

GLA_LEVELS = (32, 16, 8, 4, 2, 1)


def _dot(a, b):
    return jnp.dot(a, b, preferred_element_type=F32)


def _dot_nt(a, b):
    return lax.dot_general(a, b, (((1,), (1,)), ((), ())), preferred_element_type=F32)


def _dot_tn(a, b):
    return lax.dot_general(a, b, (((0,), (0,)), ((), ())), preferred_element_type=F32)


def _split_bf16(x):
    hi = x.astype(BF16)
    lo = (x - hi.astype(F32)).astype(BF16)
    return hi, lo


def _params(*semantics):
    return pltpu.CompilerParams(dimension_semantics=semantics, vmem_limit_bytes=VMEM_LIMIT_BYTES)


def _resident(shape):
    nd = len(shape)
    return pl.BlockSpec(shape, lambda *_: (0,) * nd, pipeline_mode=pl.Buffered(1))


def _ada_kernel(c_ref, w_ref, b_ref, o_ref):
    c = c_ref[...]
    sc = c * jax.nn.sigmoid(c)
    o_ref[...] = _dot(sc.astype(BF16), w_ref[...].astype(BF16)) + b_ref[...]


def _ada(c, w_ada, b_ada):
    bsz, d = c.shape
    n = w_ada.shape[1]
    c8 = jnp.zeros((SUBLANES, d), F32).at[:bsz].set(c)
    out = pl.pallas_call(
        _ada_kernel,
        grid=(n // ADA_TN,),
        in_specs=[pl.BlockSpec((SUBLANES, d), lambda j: (0, 0)),
                  pl.BlockSpec((d, ADA_TN), lambda j: (0, j)),
                  pl.BlockSpec((1, ADA_TN), lambda j: (0, j))],
        out_specs=pl.BlockSpec((SUBLANES, ADA_TN), lambda j: (0, j)),
        out_shape=jax.ShapeDtypeStruct((SUBLANES, n), F32),
        compiler_params=_params("arbitrary"),
        name="ada",
    )(c8, w_ada, b_ada.reshape(1, n))
    return out[:bsz].reshape(bsz, N_MOD, d)


def _modulated_norm(x, gain, mod, shift_idx, scale_idx):
    ms = jnp.mean(x * x, axis=-1, keepdims=True)
    y = x * lax.rsqrt(ms + EPS) * gain
    return y * (1.0 + mod[scale_idx:scale_idx + 1]) + mod[shift_idx:shift_idx + 1]


QK_DTYPE = BF16
FP8 = jnp.float8_e4m3fn
GROUP_COLS = 256


def _qkv_kernel(x_ref, g1_ref, mod_ref, w_ref, pos_ref, freq_ref, sa_ref, sb_ref, gq_ref, gk_ref, grp_ref,
                u_ref, qt_ref, k_ref, vt_ref):
    u = _modulated_norm(x_ref[...], g1_ref[...], mod_ref[0], 0, 1).astype(BF16)
    u_ref[...] = u
    ang = pos_ref[...].astype(F32) * freq_ref[...]
    cs = jnp.cos(ang)
    sn = jnp.sin(ang)
    sin_a = sn * sa_ref[...]
    sin_b = sn * sb_ref[...]
    grp = grp_ref[...]
    half = ROPE_DIM // 2

    def group_sums(acc):
        return [_dot(jnp.square(acc[:, j * GROUP_COLS:(j + 1) * GROUP_COLS]).astype(BF16), grp)
                for j in range(acc.shape[1] // GROUP_COLS)]

    def norm_rope(acc, sums, gain):
        slabs = []
        for j, ss in enumerate(sums):
            xn = acc[:, j * GROUP_COLS:(j + 1) * GROUP_COLS] * lax.rsqrt(ss * (1.0 / DA_HEAD_DIM) + EPS)
            for s in range(GROUP_COLS // LANES):
                xs = xn[:, s * LANES:(s + 1) * LANES] * gain
                slabs.append(xs * cs + pltpu.roll(xs, LANES - half, 1) * sin_a + pltpu.roll(xs, half, 1) * sin_b)
        return slabs

    q_scale = (DA_HEAD_DIM ** -0.5) * math.log2(math.e)
    tm = u.shape[0]
    row = lax.broadcasted_iota(jnp.int32, (LANES, tm), 0)
    acc = _dot(u, w_ref[:, 0:DA_QK_WIDTH])
    for h, xq in enumerate(norm_rope(acc, group_sums(acc), gq_ref[...])):
        xt = (xq * q_scale).T
        qt_ref[0, h, 0, 0] = jnp.where(row < DA_HEAD_DIM, xt, 0.0).astype(QK_DTYPE)
        qt_ref[0, h, 0, 1] = jnp.where(row >= DA_HEAD_DIM, xt, 0.0).astype(QK_DTYPE)
    acc = _dot(u, w_ref[:, DA_QK_WIDTH:2 * DA_QK_WIDTH])
    for h, xk in enumerate(norm_rope(acc, group_sums(acc), gk_ref[...])):
        k_ref[0, h] = xk.astype(QK_DTYPE)
    acc = _dot(u, w_ref[:, 2 * DA_QK_WIDTH:2 * DA_QK_WIDTH + DA_WIDTH])
    for h in range(DA_HEADS):
        vt_ref[0, h, 0] = acc[:, h * LANES:(h + 1) * LANES].T.astype(BF16)


def _rope_lane_tables():
    lane = np.arange(LANES) % DA_HEAD_DIM
    inv_freq = ROPE_THETA ** (-jnp.arange(0, ROPE_DIM, 2, dtype=F32) / ROPE_DIM)
    half = ROPE_DIM // 2
    freq = jnp.where(lane < ROPE_DIM, inv_freq[lane % half], 0.0).astype(F32).reshape(1, LANES)
    sign_a = np.where(lane < half, -1.0, 0.0).astype(np.float32).reshape(1, LANES)
    mask_b = np.where((lane >= half) & (lane < ROPE_DIM), 1.0, 0.0).astype(np.float32).reshape(1, LANES)
    col = np.arange(GROUP_COLS)
    group = col[:, None] // DA_HEAD_DIM == col[None, :] // DA_HEAD_DIM
    return freq, jnp.asarray(sign_a), jnp.asarray(mask_b), jnp.asarray(group, dtype=BF16)


def _qkv(x2, gain, mod, w_qkv, positions, q_gain, k_gain, bsz, seq):
    t, d = x2.shape
    tm = min(PROJ_TM, seq)
    tpb = seq // tm
    freq, sign_a, mask_b, group = _rope_lane_tables()
    lane_vec = lambda v: jnp.tile(v.astype(F32), LANES // DA_HEAD_DIM).reshape(1, LANES)
    vec_spec = pl.BlockSpec((1, LANES), lambda i: (0, 0))
    assert tm == ATT_TK and seq % ATT_TQ == 0
    per_q = ATT_TQ // tm
    qt_shape = jax.ShapeDtypeStruct((bsz, DA_HEADS, seq // ATT_TQ, 2, LANES, ATT_TQ), QK_DTYPE)
    qt_spec = pl.BlockSpec((1, DA_HEADS, 1, 2, LANES, tm),
                           lambda i: (i // tpb, 0, (i % tpb) // per_q, 0, 0, (i % tpb) % per_q))
    vt_shape = jax.ShapeDtypeStruct((bsz, DA_HEADS, tpb, DA_V_DIM, tm), BF16)
    vt_spec = pl.BlockSpec((1, DA_HEADS, 1, DA_V_DIM, tm), lambda i: (i // tpb, 0, i % tpb, 0, 0))
    return pl.pallas_call(
        _qkv_kernel,
        grid=(t // tm,),
        in_specs=[pl.BlockSpec((tm, d), lambda i: (i, 0)),
                  pl.BlockSpec((1, d), lambda i: (0, 0)),
                  pl.BlockSpec((1, N_MOD, d), lambda i: (i // tpb, 0, 0)),
                  _resident(w_qkv.shape),
                  pl.BlockSpec((tm, 1), lambda i: (i, 0)),
                  vec_spec, vec_spec, vec_spec, vec_spec, vec_spec,
                  pl.BlockSpec((GROUP_COLS, GROUP_COLS), lambda i: (0, 0))],
        out_specs=[pl.BlockSpec((tm, d), lambda i: (i, 0)),
                   qt_spec,
                   pl.BlockSpec((1, DA_HEADS, tm, LANES), lambda i: (i // tpb, 0, i % tpb, 0)),
                   vt_spec],
        out_shape=[jax.ShapeDtypeStruct((t, d), BF16), qt_shape,
                   jax.ShapeDtypeStruct((bsz, DA_HEADS, seq, LANES), QK_DTYPE), vt_shape],
        compiler_params=_params("parallel"),
        name="qkv",
    )(x2, gain.reshape(1, d), mod, w_qkv, positions.reshape(t, 1), freq, sign_a, mask_b,
      lane_vec(q_gain), lane_vec(k_gain), group)


def _glaqk_kernel(u_ref, w_ref, up_ref, bias_ref, q_ref, k_ref, la_ref):
    w = w_ref[...]
    up_hi, up_lo = _split_bf16(up_ref[...])
    parts = _row_parts(u_ref.shape[0], GLAQK_PARTS)
    accs = [_dot(u_ref[rows, :], w) for rows in parts]
    for rows, acc in zip(parts, accs):
        q_ref[rows, :] = (acc[:, 0:GLA_QK_WIDTH] * (GLA_KEY_DIM ** -0.5)).astype(BF16)
        k_ref[rows, :] = acc[:, GLA_QK_WIDTH:2 * GLA_QK_WIDTH].astype(BF16)
        low = acc[:, 2 * GLA_QK_WIDTH:]
        low_hi, low_lo = _split_bf16(low)
        z = _dot(low_hi, up_hi) + _dot(low_lo, up_hi) + _dot(low_hi, up_lo) + bias_ref[...]
        log_sig = jnp.minimum(z, 0.0) - jnp.log1p(jnp.exp(-jnp.abs(z)))
        la_ref[rows, :] = log_sig * (1.0 / GLA_GATE_TAU)


def _glaqk(u, w_g, gate_up, gate_bias, seq):
    t, d = u.shape
    tm = min(PROJ_TM, seq)
    up = jnp.zeros((LANES, GLA_QK_WIDTH), F32).at[:GLA_GATE_RANK].set(gate_up)
    row_spec = lambda w: pl.BlockSpec((tm, w), lambda i: (i, 0))
    return pl.pallas_call(
        _glaqk_kernel,
        grid=(t // tm,),
        in_specs=[row_spec(d), _resident(w_g.shape), _resident(up.shape),
                  pl.BlockSpec((1, GLA_QK_WIDTH), lambda i: (0, 0))],
        out_specs=[row_spec(GLA_QK_WIDTH), row_spec(GLA_QK_WIDTH), row_spec(GLA_QK_WIDTH)],
        out_shape=[jax.ShapeDtypeStruct((t, GLA_QK_WIDTH), BF16),
                   jax.ShapeDtypeStruct((t, GLA_QK_WIDTH), BF16),
                   jax.ShapeDtypeStruct((t, GLA_QK_WIDTH), F32)],
        compiler_params=_params("parallel"),
        name="glaqk",
    )(u, w_g, up, gate_bias.reshape(1, GLA_QK_WIDTH))


ACT_SILU_BLOCKS = (GLA_WIDTH // ACT_TN, 2 * GLA_WIDTH // ACT_TN)


def _row_parts(rows, parts):
    step = rows // parts
    return [slice(r * step, (r + 1) * step) for r in range(parts)]


def _act_kernel(u_ref, w_ref, o_ref):
    j = pl.program_id(0)
    w = w_ref[...]
    parts = _row_parts(u_ref.shape[0], ACT_PARTS)
    accs = [_dot(u_ref[rows, :], w) for rows in parts]
    for rows, acc in zip(parts, accs):
        sig = jax.nn.sigmoid(acc)
        out = jnp.where(j < ACT_SILU_BLOCKS[0], acc, jnp.where(j < ACT_SILU_BLOCKS[1], acc * sig, sig))
        o_ref[rows, :] = out.astype(BF16)


def _act(u, w_act, seq):
    t, d = u.shape
    n = w_act.shape[1]
    tm = min(ACT_TM, seq)
    return pl.pallas_call(
        _act_kernel,
        grid=(n // ACT_TN, t // tm),
        in_specs=[pl.BlockSpec((tm, d), lambda j, i: (i, 0)),
                  pl.BlockSpec((d, ACT_TN), lambda j, i: (0, j))],
        out_specs=pl.BlockSpec((tm, ACT_TN), lambda j, i: (i, j)),
        out_shape=jax.ShapeDtypeStruct((t, n), BF16),
        compiler_params=_params("arbitrary", "arbitrary"),
        name="act",
    )(u, w_act)


_FULL, _MASK, _SKIP = "full", "mask", "skip"


def _attn_kernel(bounded_ref, qt_ref, k_ref, vt_ref, lq1_ref, lk1_ref, lq2_ref, lk2_ref, subg_ref, o_ref,
                 s_ref, p_ref, cm_ref, acc_ref, m_ref, l_ref, q8_ref, *, lam_init):
    i = pl.program_id(2)
    tq, tk, qc_w = ATT_TQ, ATT_TK, ATT_QC
    nqc = tq // qc_w
    acc_ref[...] = jnp.zeros_like(acc_ref)
    l_ref[...] = jnp.zeros_like(l_ref)
    m_ref[...] = jnp.full_like(m_ref, -jnp.inf)

    def scores(t, slot, modes):
        kb = k_ref[0, 0, pl.ds(pl.multiple_of(t * tk, tk), tk), :]
        for c in range(2):
            for q in range(nqc):
                if modes[q] == _SKIP:
                    continue
                cols = slice(q * qc_w, (q + 1) * qc_w)
                s = _dot(kb, qt_ref[0, 0, 0, c, :, cols])
                if modes[q] == _MASK:
                    rel = (lax.broadcasted_iota(jnp.int32, (tk, qc_w), 0)
                           - lax.broadcasted_iota(jnp.int32, (tk, qc_w), 1))
                    s = jnp.where(rel <= i * tq + q * qc_w - t * tk, s, -jnp.inf)
                s_ref[slot, c, :, cols] = s
                cm_ref[slot, c, :, cols] = jnp.max(s, axis=0, keepdims=True)

    def accumulate(t, slot, modes):
        vtb = vt_ref[0, 0, t]
        for c in range(2):
            for q in range(nqc):
                if modes[q] == _SKIP:
                    continue
                cols = slice(q * qc_w, (q + 1) * qc_w)
                m_old = m_ref[c, :, cols]
                m_new = jnp.maximum(m_old, cm_ref[slot, c, :, cols])
                alpha = jnp.exp2(m_old - m_new)
                p = jnp.exp2(s_ref[slot, c, :, cols] - m_new)
                l_ref[c, :, cols] = alpha * l_ref[c, :, cols] + jnp.sum(p, axis=0, keepdims=True)
                acc_ref[c, :, cols] = alpha * acc_ref[c, :, cols] + _dot(vtb, p.astype(BF16))
                m_ref[c, :, cols] = m_new

    half = tk // qc_w
    even_modes = (_MASK,) * half + (_FULL,) * (nqc - half)
    full_modes = (_FULL,) * nqc
    last_modes = (_SKIP,) * half + (_MASK,) * (nqc - half)

    def step(probs_of=None, weigh_of=None):
        if probs_of is not None:
            pt, pslot, pmodes = probs_of
            kb = k_ref[0, 0, pl.ds(pl.multiple_of(pt * tk, tk), tk), :].astype(F32).astype(FP8)
        if weigh_of is not None:
            wt, wslot, wmodes = weigh_of
            vtb = vt_ref[0, 0, wt]
        for c in range(2):
            for q in range(nqc):
                cols = slice(q * qc_w, (q + 1) * qc_w)
                if probs_of is not None and pmodes[q] != _SKIP:
                    s = _dot(kb, q8_ref[c, :, cols])
                    if pmodes[q] == _MASK:
                        rel = (lax.broadcasted_iota(jnp.int32, (tk, qc_w), 0)
                               - lax.broadcasted_iota(jnp.int32, (tk, qc_w), 1))
                        s = jnp.where(rel <= i * tq + q * qc_w - pt * tk, s, -jnp.inf)
                    p = jnp.exp2(s)
                    l_ref[c, :, cols] += jnp.sum(p, axis=0, keepdims=True)
                    p_ref[pslot, c, :, cols] = p.astype(BF16)
                if weigh_of is not None and wmodes[q] != _SKIP:
                    acc_ref[c, :, cols] += _dot(vtb, p_ref[wslot, c, :, cols])

    @pl.when(bounded_ref[0] != 0)
    def _():
        q8_ref[...] = qt_ref[0, 0, 0].astype(F32).astype(FP8)
        step(probs_of=(0, 0, even_modes))

        def pair(p, next_modes):
            step(probs_of=(2 * p + 1, 1, full_modes), weigh_of=(2 * p, 0, full_modes))
            step(probs_of=(2 * p + 2, 0, next_modes), weigh_of=(2 * p + 1, 1, full_modes))

        unmasked = jnp.maximum(i - 1, 0)

        def two_pairs(j, carry):
            pair(2 * j, full_modes)
            pair(2 * j + 1, full_modes)
            return carry

        lax.fori_loop(0, unmasked // 2, two_pairs, 0)

        @pl.when(unmasked % 2 == 1)
        def _():
            pair(unmasked - 1, full_modes)

        @pl.when(i > 0)
        def _():
            pair(i - 1, even_modes)

        step(probs_of=(2 * i + 1, 1, last_modes), weigh_of=(2 * i, 0, full_modes))
        step(weigh_of=(2 * i + 1, 1, last_modes))

    @pl.when(bounded_ref[0] == 0)
    def _():
        scores(0, 0, even_modes)

        def pair(p, carry):
            scores(2 * p + 1, 1, full_modes)
            accumulate(2 * p, 0, full_modes)
            scores(2 * p + 2, 0, even_modes)
            accumulate(2 * p + 1, 1, full_modes)
            return carry

        lax.fori_loop(0, i, pair, 0)
        scores(2 * i + 1, 1, last_modes)
        accumulate(2 * i, 0, full_modes)
        accumulate(2 * i + 1, 1, last_modes)

    s1 = jnp.sum(lq1_ref[...] * lk1_ref[...], axis=1, keepdims=True)
    s2 = jnp.sum(lq2_ref[...] * lk2_ref[...], axis=1, keepdims=True)
    lam = jnp.exp(s1) - jnp.exp(s2) + lam_init
    o = acc_ref[0] / l_ref[0] - lam * (acc_ref[1] / l_ref[1])
    ms = jnp.mean(o * o, axis=0, keepdims=True)
    y = o * lax.rsqrt(ms + EPS) * subg_ref[...] * (1.0 - lam_init)
    o_ref[...] = y.T.astype(BF16)


ATT_DIRECT_LOG2 = 18.0


def _scores_bounded(q_gain, k_gain):
    q_scale = (DA_HEAD_DIM ** -0.5) * math.log2(math.e)
    bound = 1.02 * DA_HEAD_DIM * q_scale * jnp.max(jnp.abs(q_gain)) * jnp.max(jnp.abs(k_gain))
    return (bound <= ATT_DIRECT_LOG2).astype(jnp.int32).reshape(1)


def _attention(qt, k, vt, q_gain, k_gain, lq1, lk1, lq2, lk2, subln_g, layer_idx):
    bsz, heads, nq, _, _, tq = qt.shape
    nkb, vdim, tk = vt.shape[2:]
    seq = nq * tq
    assert (tq, tk, vdim) == (ATT_TQ, ATT_TK, DA_V_DIM) and tq == 2 * tk and nkb * tk == seq
    lam_init = 0.8 - 0.6 * math.exp(-0.3 * layer_idx)
    vec = lambda v: v.astype(F32).reshape(1, DA_HEAD_DIM)
    vec_spec = pl.BlockSpec((1, DA_HEAD_DIM), lambda b, h, i: (0, 0))
    return pl.pallas_call(
        functools.partial(_attn_kernel, lam_init=lam_init),
        grid=(bsz, heads, nq),
        in_specs=[pl.BlockSpec(memory_space=pltpu.SMEM),
                  pl.BlockSpec((1, 1, 1, 2, LANES, tq), lambda b, h, i: (b, h, i, 0, 0, 0)),
                  pl.BlockSpec((1, 1, seq, LANES), lambda b, h, i: (b, h, 0, 0)),
                  pl.BlockSpec((1, 1, nkb, vdim, tk), lambda b, h, i: (b, h, 0, 0, 0)),
                  vec_spec, vec_spec, vec_spec, vec_spec,
                  pl.BlockSpec((DA_V_DIM, 1), lambda b, h, i: (0, 0))],
        out_specs=pl.BlockSpec((tq, LANES), lambda b, h, i: (b * nq + i, h)),
        out_shape=jax.ShapeDtypeStruct((bsz * seq, DA_WIDTH), BF16),
        scratch_shapes=[pltpu.VMEM((2, 2, tk, tq), F32),
                        pltpu.VMEM((2, 2, tk, tq), BF16),
                        pltpu.VMEM((2, 2, 1, tq), F32),
                        pltpu.VMEM((2, vdim, tq), F32),
                        pltpu.VMEM((2, 1, tq), F32),
                        pltpu.VMEM((2, 1, tq), F32),
                        pltpu.VMEM((2, LANES, tq), FP8)],
        compiler_params=_params("parallel", "parallel", "arbitrary"),
        name="attn",
    )(_scores_bounded(q_gain, k_gain), qt, k, vt, vec(lq1), vec(lk1), vec(lq2), vec(lk2),
      subln_g.astype(F32).reshape(DA_V_DIM, 1))


def _gla_decay_matrix():
    c = GLA_CHUNK
    t = np.arange(c)[:, None]
    s = np.arange(c)[None, :]
    blocks = [(s <= t)]
    for h in GLA_LEVELS:
        r = (t // (2 * h)) * (2 * h) + h
        upper = (t >= r) & (s > r) & (s <= t)
        lower = (t < r) & (s > t) & (s <= r)
        blocks.append(upper | lower)
    return jnp.asarray(np.concatenate(blocks, axis=0), dtype=BF16)


def _gla_kernel(q_ref, k_ref, v_ref, la_ref, sr_ref, fm_ref, g_ref, o_ref, st_ref, *, nchunk):
    @pl.when(pl.program_id(1) == 0)
    def _():
        st_ref[...] = jnp.zeros_like(st_ref)

    c = GLA_CHUNK
    ii = lax.broadcasted_iota(jnp.int32, (c, c), 0)
    jj = lax.broadcasted_iota(jnp.int32, (c, c), 1)
    eye = ii == jj
    level_masks = []
    for h in GLA_LEVELS:
        shift = int(math.log2(2 * h))
        same = (ii >> shift) == (jj >> shift)
        level_masks.append(same & ((ii & (2 * h - 1)) >= h) & ((jj & (2 * h - 1)) < h))
    fm = fm_ref[...]
    gain = g_ref[...]

    heads = range(GLA_HEADS)
    ksl = [slice(hd * GLA_KEY_DIM, (hd + 1) * GLA_KEY_DIM) for hd in heads]
    vsl = [slice(hd * GLA_VAL_DIM, (hd + 1) * GLA_VAL_DIM) for hd in heads]

    def local_part(rows):
        la = la_ref[rows, :]
        la_hi, la_lo = _split_bf16(la)
        fa = _dot(fm, la_hi) + _dot(fm, la_lo)
        b = fa[0:c]
        b_last = b[c - 1:c]
        e_b = jnp.exp(b)
        e_rest = jnp.exp(b_last - b)
        e_last = jnp.exp(b_last)
        e_lvl = [jnp.exp(fa[(l + 1) * c:(l + 2) * c]) for l in range(len(GLA_LEVELS))]
        qb = [q_ref[rows, ksl[hd]] for hd in heads]
        kb = [k_ref[rows, ksl[hd]] for hd in heads]
        q = [x.astype(F32) for x in qb]
        k = [x.astype(F32) for x in kb]
        v = [v_ref[rows, vsl[hd]] for hd in heads]
        att = [jnp.where(eye, _dot_nt(qb[hd], kb[hd]), 0.0) for hd in heads]
        for l in range(len(GLA_LEVELS)):
            for hd in heads:
                e = e_lvl[l][:, ksl[hd]]
                lvl = _dot_nt((q[hd] * e).astype(BF16), (k[hd] * e).astype(BF16))
                att[hd] = jnp.where(level_masks[l], lvl, att[hd])
        upd = [_dot_tn(v[hd], (k[hd] * e_rest[:, ksl[hd]]).astype(BF16)) for hd in heads]
        intra = [_dot(att[hd].astype(BF16), v[hd]) for hd in heads]
        qd = [(q[hd] * e_b[:, ksl[hd]]).astype(BF16) for hd in heads]
        return intra, qd, upd, e_last

    def carried_part(rows, local):
        intra, qd, upd, e_last = local
        for hd in heads:
            state = st_ref[hd]
            out = intra[hd] + _dot_nt(qd[hd], state.astype(BF16))
            ms = jnp.mean(out * out, axis=-1, keepdims=True)
            y = out * lax.rsqrt(ms + EPS) * gain * sr_ref[rows, vsl[hd]].astype(F32)
            o_ref[rows, vsl[hd]] = y.astype(BF16)
            st_ref[hd] = state * e_last[:, ksl[hd]] + upd[hd]

    def group(g, carry):
        rows = [pl.ds(pl.multiple_of((g * GLA_GROUP + j) * c, c), c) for j in range(GLA_GROUP)]
        local = [local_part(r) for r in rows]
        for r, loc in zip(rows, local):
            carried_part(r, loc)
        return carry

    lax.fori_loop(0, nchunk // GLA_GROUP, group, 0)


def _gla(gq, gk, act, la, out_gain, bsz, seq):
    t = gq.shape[0]
    tm = min(GLA_TM, seq)
    tpb = seq // tm
    fm = _gla_decay_matrix()
    row = lambda b, n: (b * tpb + n, 0)
    return pl.pallas_call(
        functools.partial(_gla_kernel, nchunk=tm // GLA_CHUNK),
        grid=(bsz, tpb),
        in_specs=[pl.BlockSpec((tm, GLA_QK_WIDTH), row),
                  pl.BlockSpec((tm, GLA_QK_WIDTH), row),
                  pl.BlockSpec((tm, GLA_WIDTH), row),
                  pl.BlockSpec((tm, GLA_QK_WIDTH), row),
                  pl.BlockSpec((tm, GLA_WIDTH), lambda b, n: (b * tpb + n, 1)),
                  pl.BlockSpec(fm.shape, lambda b, n: (0, 0)),
                  pl.BlockSpec((1, GLA_VAL_DIM), lambda b, n: (0, 0))],
        out_specs=pl.BlockSpec((tm, GLA_WIDTH), row),
        out_shape=jax.ShapeDtypeStruct((t, GLA_WIDTH), BF16),
        scratch_shapes=[pltpu.VMEM((GLA_HEADS, GLA_VAL_DIM, GLA_KEY_DIM), F32)],
        compiler_params=_params("parallel", "arbitrary"),
        name="gla",
    )(gq, gk, act, la, act, fm, out_gain.astype(F32).reshape(1, GLA_VAL_DIM))


def _merge_kernel(ya_ref, yb_ref, sa_ref, sb_ref, x_ref, mod_ref, wa_ref, wb_ref, wo_ref, o_ref, mg_ref):
    ya = ya_ref[...]
    yb = yb_ref[...]
    for n in range(mg_ref.shape[1] // MERGE_TN):
        cols = slice(n * MERGE_TN, (n + 1) * MERGE_TN)
        ta = _dot(ya, wa_ref[:, cols])
        tb = _dot(yb, wb_ref[:, cols])
        mg_ref[:, cols] = (sa_ref[:, cols].astype(F32) * ta + sb_ref[:, cols].astype(F32) * tb).astype(BF16)
    o_ref[...] = x_ref[...] + mod_ref[0][2:3] * _dot(mg_ref[...], wo_ref[...])


def _merge(ya, yb, act, x2, mod, w_a, w_b, w_o, seq):
    t, d = x2.shape
    tm = min(MERGE_TM, seq)
    tpb = seq // tm
    gate_blk = 2 * GLA_WIDTH // d
    return pl.pallas_call(
        _merge_kernel,
        grid=(t // tm,),
        in_specs=[pl.BlockSpec((tm, DA_WIDTH), lambda i: (i, 0)),
                  pl.BlockSpec((tm, GLA_WIDTH), lambda i: (i, 0)),
                  pl.BlockSpec((tm, d), lambda i: (i, gate_blk)),
                  pl.BlockSpec((tm, d), lambda i: (i, gate_blk + 1)),
                  pl.BlockSpec((tm, d), lambda i: (i, 0)),
                  pl.BlockSpec((1, N_MOD, d), lambda i: (i // tpb, 0, 0)),
                  _resident(w_a.shape), _resident(w_b.shape), _resident(w_o.shape)],
        out_specs=pl.BlockSpec((tm, d), lambda i: (i, 0)),
        out_shape=jax.ShapeDtypeStruct((t, d), F32),
        scratch_shapes=[pltpu.VMEM((tm, d), BF16)],
        compiler_params=_params("parallel"),
        name="merge",
    )(ya, yb, act, act, x2, mod, w_a, w_b, w_o)


def _mlp_kernel(h_ref, g_ref, mod_ref, w1_ref, w2_ref, o_ref, u_ref):
    @pl.when(pl.program_id(1) == 0)
    def _():
        h = h_ref[...]
        u_ref[...] = _modulated_norm(h, g_ref[...], mod_ref[0], 3, 4).astype(BF16)
        o_ref[...] = h

    hid = jnp.square(jnp.maximum(_dot(u_ref[...], w1_ref[...]), 0.0))
    o_ref[...] += mod_ref[0][5:6] * _dot(hid.astype(BF16), w2_ref[...])


def _mlp(h, gain, mod, w1, w2, seq):
    t, d = h.shape
    tm = min(MLP_TM, seq)
    tpb = seq // tm
    ff = w1.shape[1]
    return pl.pallas_call(
        _mlp_kernel,
        grid=(t // tm, ff // MLP_TF),
        in_specs=[pl.BlockSpec((tm, d), lambda i, kf: (i, 0)),
                  pl.BlockSpec((1, d), lambda i, kf: (0, 0)),
                  pl.BlockSpec((1, N_MOD, d), lambda i, kf: (i // tpb, 0, 0)),
                  pl.BlockSpec((d, MLP_TF), lambda i, kf: (0, kf)),
                  pl.BlockSpec((MLP_TF, d), lambda i, kf: (kf, 0))],
        out_specs=pl.BlockSpec((tm, d), lambda i, kf: (i, 0)),
        out_shape=jax.ShapeDtypeStruct((t, d), F32),
        scratch_shapes=[pltpu.VMEM((tm, d), BF16)],
        compiler_params=_params("parallel", "arbitrary"),
        name="mlp",
    )(h, gain.reshape(1, d), mod, w1, w2)


def _layer(h2, c, positions, bsz, seq, layer_idx, w_ada, b_ada, norm1_g, w_in, da_q_norm_g, da_k_norm_g,
           lq1, lk1, lq2, lk2, da_subln_g, gla_gate_up, gla_gate_bias, gla_out_norm_g,
           w_branch_a, w_branch_b, w_out, norm2_g, w_mlp_in, w_mlp_out):
    d = h2.shape[1]
    assert seq % min(PROJ_TM, seq) == 0 and seq % GLA_CHUNK == 0
    offs = np.concatenate([[0], np.cumsum(IN_SIZES)])
    col = lambda a, b: w_in[:, offs[a]:offs[b]].astype(BF16)
    w_qkv = col(0, 3)
    w_g = jnp.concatenate([col(3, 5), col(7, 8), jnp.zeros((d, LANES - GLA_GATE_RANK), BF16)], axis=1)
    w_act = jnp.concatenate([col(5, 7), col(8, 10)], axis=1)

    mod = _ada(c, w_ada, b_ada)
    u, qt, k, vt = _qkv(h2, norm1_g, mod, w_qkv, positions, da_q_norm_g, da_k_norm_g, bsz, seq)
    gq, gk, la = _glaqk(u, w_g, gla_gate_up, gla_gate_bias, seq)
    act = _act(u, w_act, seq)
    ya = _attention(qt, k, vt, da_q_norm_g, da_k_norm_g, lq1, lk1, lq2, lk2, da_subln_g, layer_idx)
    yb = _gla(gq, gk, act, la, gla_out_norm_g, bsz, seq)
    h1 = _merge(ya, yb, act, h2, mod, w_branch_a.astype(BF16), w_branch_b.astype(BF16),
                w_out.astype(BF16), seq)
    return _mlp(h1, norm2_g, mod, w_mlp_in.astype(BF16), w_mlp_out.astype(BF16), seq)


def kernel(x, c, positions, w_ada, b_ada, norm1_g, w_in, da_q_norm_g, da_k_norm_g, da_lambda_q1,
           da_lambda_k1, da_lambda_q2, da_lambda_k2, da_subln_g, gla_gate_up, gla_gate_bias,
           gla_out_norm_g, w_branch_a, w_branch_b, w_out, norm2_g, w_mlp_in, w_mlp_out):
    bsz, seq, d = x.shape
    h = x.reshape(bsz * seq, d)
    for l in range(w_ada.shape[0]):
        h = _layer(h, c, positions, bsz, seq, l, w_ada[l], b_ada[l], norm1_g[l], w_in[l], da_q_norm_g[l],
                   da_k_norm_g[l], da_lambda_q1[l], da_lambda_k1[l], da_lambda_q2[l], da_lambda_k2[l],
                   da_subln_g[l], gla_gate_up[l], gla_gate_bias[l], gla_out_norm_g[l], w_branch_a[l],
                   w_branch_b[l], w_out[l], norm2_g[l], w_mlp_in[l], w_mlp_out[l])
    return h.reshape(bsz, seq, d)
```

```python
import functools
import math

import jax
import jax.numpy as jnp
import numpy as np
from jax import lax
from jax.experimental import pallas as pl
from jax.experimental.pallas import tpu as pltpu

F32 = jnp.float32
BF16 = jnp.bfloat16

D_MODEL = 2048
DA_HEADS = 8
DA_HEAD_DIM = 64
DA_V_DIM = 2 * DA_HEAD_DIM
DA_QK_WIDTH = DA_HEADS * 2 * DA_HEAD_DIM
DA_WIDTH = DA_HEADS * DA_V_DIM
ROPE_THETA = 500000.0
ROPE_DIM = DA_HEAD_DIM // 4
GLA_HEADS = 4
GLA_KEY_DIM = 128
GLA_VAL_DIM = 256
GLA_QK_WIDTH = GLA_HEADS * GLA_KEY_DIM
GLA_WIDTH = GLA_HEADS * GLA_VAL_DIM
GLA_GATE_RANK = 16
GLA_GATE_TAU = 16.0
GLA_CHUNK = 64
D_FF = 4 * D_MODEL
N_MOD = 6
EPS = 1e-6
IN_SIZES = (DA_QK_WIDTH, DA_QK_WIDTH, DA_WIDTH, GLA_QK_WIDTH, GLA_QK_WIDTH, GLA_WIDTH, GLA_WIDTH,
            GLA_GATE_RANK, D_MODEL, D_MODEL)

LANES = 128
SUBLANES = 8
VMEM_LIMIT_BYTES = 56 * 1024 * 1024

ADA_TN = 1024
PROJ_TM = 512
ACT_TM = 1024
ACT_TN = 1024
ACT_PARTS = 4
GLAQK_PARTS = 2
ATT_TK = PROJ_TM
ATT_TQ = 2 * ATT_TK
ATT_QC = 256
ATT_PAIRS_PER_TRIP = 4
GLA_TM = 512
GLA_GROUP = 4
MERGE_TM = 512
MERGE_TN = 512
MLP_TM = 1024
MLP_TF = 512

GLA_LEVELS = (32, 16, 8, 4, 2, 1)


def _dot(a, b):
    return jnp.dot(a, b, preferred_element_type=F32)


def _dot_nt(a, b):
    return lax.dot_general(a, b, (((1,), (1,)), ((), ())), preferred_element_type=F32)


def _dot_tn(a, b):
    return lax.dot_general(a, b, (((0,), (0,)), ((), ())), preferred_element_type=F32)


def _split_bf16(x):
    hi = x.astype(BF16)
    lo = (x - hi.astype(F32)).astype(BF16)
    return hi, lo


def _params(*semantics):
    return pltpu.CompilerParams(dimension_semantics=semantics, vmem_limit_bytes=VMEM_LIMIT_BYTES)


def _resident(shape):
    nd = len(shape)
    return pl.BlockSpec(shape, lambda *_: (0,) * nd, pipeline_mode=pl.Buffered(1))


def _ada_kernel(c_ref, w_ref, b_ref, o_ref):
    c = c_ref[...]
    sc = c * jax.nn.sigmoid(c)
    o_ref[...] = _dot(sc.astype(BF16), w_ref[...].astype(BF16)) + b_ref[...]


def _ada(c, w_ada, b_ada):
    bsz, d = c.shape
    n = w_ada.shape[1]
    c8 = jnp.zeros((SUBLANES, d), F32).at[:bsz].set(c)
    out = pl.pallas_call(
        _ada_kernel,
        grid=(n // ADA_TN,),
        in_specs=[pl.BlockSpec((SUBLANES, d), lambda j: (0, 0)),
                  pl.BlockSpec((d, ADA_TN), lambda j: (0, j)),
                  pl.BlockSpec((1, ADA_TN), lambda j: (0, j))],
        out_specs=pl.BlockSpec((SUBLANES, ADA_TN), lambda j: (0, j)),
        out_shape=jax.ShapeDtypeStruct((SUBLANES, n), F32),
        compiler_params=_params("arbitrary"),
        name="ada",
    )(c8, w_ada, b_ada.reshape(1, n))
    return out[:bsz].reshape(bsz, N_MOD, d)


def _modulated_norm(x, gain, mod, shift_idx, scale_idx):
    ms = jnp.mean(x * x, axis=-1, keepdims=True)
    y = x * lax.rsqrt(ms + EPS) * gain
    return y * (1.0 + mod[scale_idx:scale_idx + 1]) + mod[shift_idx:shift_idx + 1]


QK_DTYPE = BF16
FP8 = jnp.float8_e4m3fn
GROUP_COLS = 256


def _qkv_kernel(x_ref, g1_ref, mod_ref, w_ref, pos_ref, freq_ref, sa_ref, sb_ref, gq_ref, gk_ref, grp_ref,
                u_ref, qt_ref, k_ref, vt_ref):
    u = _modulated_norm(x_ref[...], g1_ref[...], mod_ref[0], 0, 1).astype(BF16)
    u_ref[...] = u
    ang = pos_ref[...].astype(F32) * freq_ref[...]
    cs = jnp.cos(ang)
    sn = jnp.sin(ang)
    sin_a = sn * sa_ref[...]
    sin_b = sn * sb_ref[...]
    grp = grp_ref[...]
    half = ROPE_DIM // 2

    def group_sums(acc):
        return [_dot(jnp.square(acc[:, j * GROUP_COLS:(j + 1) * GROUP_COLS]).astype(BF16), grp)
                for j in range(acc.shape[1] // GROUP_COLS)]

    def norm_rope(acc, sums, gain):
        slabs = []
        for j, ss in enumerate(sums):
            xn = acc[:, j * GROUP_COLS:(j + 1) * GROUP_COLS] * lax.rsqrt(ss * (1.0 / DA_HEAD_DIM) + EPS)
            for s in range(GROUP_COLS // LANES):
                xs = xn[:, s * LANES:(s + 1) * LANES] * gain
                slabs.append(xs * cs + pltpu.roll(xs, LANES - half, 1) * sin_a + pltpu.roll(xs, half, 1) * sin_b)
        return slabs

    q_scale = (DA_HEAD_DIM ** -0.5) * math.log2(math.e)
    tm = u.shape[0]
    row = lax.broadcasted_iota(jnp.int32, (LANES, tm), 0)
    acc = _dot(u, w_ref[:, 0:DA_QK_WIDTH])
    for h, xq in enumerate(norm_rope(acc, group_sums(acc), gq_ref[...])):
        xt = (xq * q_scale).T
        qt_ref[0, h, 0, 0] = jnp.where(row < DA_HEAD_DIM, xt, 0.0).astype(QK_DTYPE)
        qt_ref[0, h, 0, 1] = jnp.where(row >= DA_HEAD_DIM, xt, 0.0).astype(QK_DTYPE)
    acc = _dot(u, w_ref[:, DA_QK_WIDTH:2 * DA_QK_WIDTH])
    for h, xk in enumerate(norm_rope(acc, group_sums(acc), gk_ref[...])):
        k_ref[0, h] = xk.astype(QK_DTYPE)
    acc = _dot(u, w_ref[:, 2 * DA_QK_WIDTH:2 * DA_QK_WIDTH + DA_WIDTH])
    for h in range(DA_HEADS):
        vt_ref[0, h, 0] = acc[:, h * LANES:(h + 1) * LANES].T.astype(BF16)


def _rope_lane_tables():
    lane = np.arange(LANES) % DA_HEAD_DIM
    inv_freq = ROPE_THETA ** (-jnp.arange(0, ROPE_DIM, 2, dtype=F32) / ROPE_DIM)
    half = ROPE_DIM // 2
    freq = jnp.where(lane < ROPE_DIM, inv_freq[lane % half], 0.0).astype(F32).reshape(1, LANES)
    sign_a = np.where(lane < half, -1.0, 0.0).astype(np.float32).reshape(1, LANES)
    mask_b = np.where((lane >= half) & (lane < ROPE_DIM), 1.0, 0.0).astype(np.float32).reshape(1, LANES)
    col = np.arange(GROUP_COLS)
    group = col[:, None] // DA_HEAD_DIM == col[None, :] // DA_HEAD_DIM
    return freq, jnp.asarray(sign_a), jnp.asarray(mask_b), jnp.asarray(group, dtype=BF16)


def _qkv(x2, gain, mod, w_qkv, positions, q_gain, k_gain, bsz, seq):
    t, d = x2.shape
    tm = min(PROJ_TM, seq)
    tpb = seq // tm
    freq, sign_a, mask_b, group = _rope_lane_tables()
    lane_vec = lambda v: jnp.tile(v.astype(F32), LANES // DA_HEAD_DIM).reshape(1, LANES)
    vec_spec = pl.BlockSpec((1, LANES), lambda i: (0, 0))
    assert tm == ATT_TK and seq % ATT_TQ == 0
    per_q = ATT_TQ // tm
    qt_shape = jax.ShapeDtypeStruct((bsz, DA_HEADS, seq // ATT_TQ, 2, LANES, ATT_TQ), QK_DTYPE)
    qt_spec = pl.BlockSpec((1, DA_HEADS, 1, 2, LANES, tm),
                           lambda i: (i // tpb, 0, (i % tpb) // per_q, 0, 0, (i % tpb) % per_q))
    vt_shape = jax.ShapeDtypeStruct((bsz, DA_HEADS, tpb, DA_V_DIM, tm), BF16)
    vt_spec = pl.BlockSpec((1, DA_HEADS, 1, DA_V_DIM, tm), lambda i: (i // tpb, 0, i % tpb, 0, 0))
    return pl.pallas_call(
        _qkv_kernel,
        grid=(t // tm,),
        in_specs=[pl.BlockSpec((tm, d), lambda i: (i, 0)),
                  pl.BlockSpec((1, d), lambda i: (0, 0)),
                  pl.BlockSpec((1, N_MOD, d), lambda i: (i // tpb, 0, 0)),
                  _resident(w_qkv.shape),
                  pl.BlockSpec((tm, 1), lambda i: (i, 0)),
                  vec_spec, vec_spec, vec_spec, vec_spec, vec_spec,
                  pl.BlockSpec((GROUP_COLS, GROUP_COLS), lambda i: (0, 0))],
        out_specs=[pl.BlockSpec((tm, d), lambda i: (i, 0)),
                   qt_spec,
                   pl.BlockSpec((1, DA_HEADS, tm, LANES), lambda i: (i // tpb, 0, i % tpb, 0)),
                   vt_spec],
        out_shape=[jax.ShapeDtypeStruct((t, d), BF16), qt_shape,
                   jax.ShapeDtypeStruct((bsz, DA_HEADS, seq, LANES), QK_DTYPE), vt_shape],
        compiler_params=_params("parallel"),
        name="qkv",
    )(x2, gain.reshape(1, d), mod, w_qkv, positions.reshape(t, 1), freq, sign_a, mask_b,
      lane_vec(q_gain), lane_vec(k_gain), group)


def _glaqk_kernel(u_ref, w_ref, up_ref, bias_ref, q_ref, k_ref, la_ref):
    w = w_ref[...]
    up_hi, up_lo = _split_bf16(up_ref[...])
    parts = _row_parts(u_ref.shape[0], GLAQK_PARTS)
    accs = [_dot(u_ref[rows, :], w) for rows in parts]
    for rows, acc in zip(parts, accs):
        q_ref[rows, :] = (acc[:, 0:GLA_QK_WIDTH] * (GLA_KEY_DIM ** -0.5)).astype(BF16)
        k_ref[rows, :] = acc[:, GLA_QK_WIDTH:2 * GLA_QK_WIDTH].astype(BF16)
        low = acc[:, 2 * GLA_QK_WIDTH:]
        low_hi, low_lo = _split_bf16(low)
        z = _dot(low_hi, up_hi) + _dot(low_lo, up_hi) + _dot(low_hi, up_lo) + bias_ref[...]
        log_sig = jnp.minimum(z, 0.0) - jnp.log1p(jnp.exp(-jnp.abs(z)))
        la_ref[rows, :] = log_sig * (1.0 / GLA_GATE_TAU)


def _glaqk(u, w_g, gate_up, gate_bias, seq):
    t, d = u.shape
    tm = min(PROJ_TM, seq)
    up = jnp.zeros((LANES, GLA_QK_WIDTH), F32).at[:GLA_GATE_RANK].set(gate_up)
    row_spec = lambda w: pl.BlockSpec((tm, w), lambda i: (i, 0))
    return pl.pallas_call(
        _glaqk_kernel,
        grid=(t // tm,),
        in_specs=[row_spec(d), _resident(w_g.shape), _resident(up.shape),
                  pl.BlockSpec((1, GLA_QK_WIDTH), lambda i: (0, 0))],
        out_specs=[row_spec(GLA_QK_WIDTH), row_spec(GLA_QK_WIDTH), row_spec(GLA_QK_WIDTH)],
        out_shape=[jax.ShapeDtypeStruct((t, GLA_QK_WIDTH), BF16),
                   jax.ShapeDtypeStruct((t, GLA_QK_WIDTH), BF16),
                   jax.ShapeDtypeStruct((t, GLA_QK_WIDTH), F32)],
        compiler_params=_params("parallel"),
        name="glaqk",
    )(u, w_g, up, gate_bias.reshape(1, GLA_QK_WIDTH))


ACT_SILU_BLOCKS = (GLA_WIDTH // ACT_TN, 2 * GLA_WIDTH // ACT_TN)


def _row_parts(rows, parts):
    step = rows // parts
    return [slice(r * step, (r + 1) * step) for r in range(parts)]


def _act_kernel(u_ref, w_ref, o_ref):
    j = pl.program_id(0)
    w = w_ref[...]
    parts = _row_parts(u_ref.shape[0], ACT_PARTS)
    accs = [_dot(u_ref[rows, :], w) for rows in parts]
    for rows, acc in zip(parts, accs):
        sig = jax.nn.sigmoid(acc)
        out = jnp.where(j < ACT_SILU_BLOCKS[0], acc, jnp.where(j < ACT_SILU_BLOCKS[1], acc * sig, sig))
        o_ref[rows, :] = out.astype(BF16)


def _act(u, w_act, seq):
    t, d = u.shape
    n = w_act.shape[1]
    tm = min(ACT_TM, seq)
    return pl.pallas_call(
        _act_kernel,
        grid=(n // ACT_TN, t // tm),
        in_specs=[pl.BlockSpec((tm, d), lambda j, i: (i, 0)),
                  pl.BlockSpec((d, ACT_TN), lambda j, i: (0, j))],
        out_specs=pl.BlockSpec((tm, ACT_TN), lambda j, i: (i, j)),
        out_shape=jax.ShapeDtypeStruct((t, n), BF16),
        compiler_params=_params("arbitrary", "arbitrary"),
        name="act",
    )(u, w_act)


_FULL, _MASK, _SKIP = "full", "mask", "skip"


def _attn_kernel(bounded_ref, qt_ref, k_ref, vt_ref, lq1_ref, lk1_ref, lq2_ref, lk2_ref, subg_ref, o_ref,
                 s_ref, p_ref, cm_ref, acc_ref, m_ref, l_ref, q8_ref, *, lam_init):
    i = pl.program_id(2)
    tq, tk, qc_w = ATT_TQ, ATT_TK, ATT_QC
    nqc = tq // qc_w
    acc_ref[...] = jnp.zeros_like(acc_ref)
    l_ref[...] = jnp.zeros_like(l_ref)
    m_ref[...] = jnp.full_like(m_ref, -jnp.inf)

    def scores(t, slot, modes):
        kb = k_ref[0, 0, pl.ds(pl.multiple_of(t * tk, tk), tk), :]
        for c in range(2):
            for q in range(nqc):
                if modes[q] == _SKIP:
                    continue
                cols = slice(q * qc_w, (q + 1) * qc_w)
                s = _dot(kb, qt_ref[0, 0, 0, c, :, cols])
                if modes[q] == _MASK:
                    rel = (lax.broadcasted_iota(jnp.int32, (tk, qc_w), 0)
                           - lax.broadcasted_iota(jnp.int32, (tk, qc_w), 1))
                    s = jnp.where(rel <= i * tq + q * qc_w - t * tk, s, -jnp.inf)
                s_ref[slot, c, :, cols] = s
                cm_ref[slot, c, :, cols] = jnp.max(s, axis=0, keepdims=True)

    def accumulate(t, slot, modes):
        vtb = vt_ref[0, 0, t]
        for c in range(2):
            for q in range(nqc):
                if modes[q] == _SKIP:
                    continue
                cols = slice(q * qc_w, (q + 1) * qc_w)
                m_old = m_ref[c, :, cols]
                m_new = jnp.maximum(m_old, cm_ref[slot, c, :, cols])
                alpha = jnp.exp2(m_old - m_new)
                p = jnp.exp2(s_ref[slot, c, :, cols] - m_new)
                l_ref[c, :, cols] = alpha * l_ref[c, :, cols] + jnp.sum(p, axis=0, keepdims=True)
                acc_ref[c, :, cols] = alpha * acc_ref[c, :, cols] + _dot(vtb, p.astype(BF16))
                m_ref[c, :, cols] = m_new

    half = tk // qc_w
    even_modes = (_MASK,) * half + (_FULL,) * (nqc - half)
    full_modes = (_FULL,) * nqc
    last_modes = (_SKIP,) * half + (_MASK,) * (nqc - half)

    def step(probs_of=None, weigh_of=None):
        if probs_of is not None:
            pt, pslot, pmodes = probs_of
            kb = k_ref[0, 0, pl.ds(pl.multiple_of(pt * tk, tk), tk), :].astype(F32).astype(FP8)
        if weigh_of is not None:
            wt, wslot, wmodes = weigh_of
            vtb = vt_ref[0, 0, wt]
        for c in range(2):
            for q in range(nqc):
                cols = slice(q * qc_w, (q + 1) * qc_w)
                if probs_of is not None and pmodes[q] != _SKIP:
                    s = _dot(kb, q8_ref[c, :, cols])
                    if pmodes[q] == _MASK:
                        rel = (lax.broadcasted_iota(jnp.int32, (tk, qc_w), 0)
                               - lax.broadcasted_iota(jnp.int32, (tk, qc_w), 1))
                        s = jnp.where(rel <= i * tq + q * qc_w - pt * tk, s, -jnp.inf)
                    p = jnp.exp2(s)
                    l_ref[c, :, cols] += jnp.sum(p, axis=0, keepdims=True)
                    p_ref[pslot, c, :, cols] = p.astype(BF16)
                if weigh_of is not None and wmodes[q] != _SKIP:
                    acc_ref[c, :, cols] += _dot(vtb, p_ref[wslot, c, :, cols])

    @pl.when(bounded_ref[0] != 0)
    def _():
        q8_ref[...] = qt_ref[0, 0, 0].astype(F32).astype(FP8)
        step(probs_of=(0, 0, even_modes))

        def pair(p, next_modes):
            step(probs_of=(2 * p + 1, 1, full_modes), weigh_of=(2 * p, 0, full_modes))
            step(probs_of=(2 * p + 2, 0, next_modes), weigh_of=(2 * p + 1, 1, full_modes))

        unmasked = jnp.maximum(i - 1, 0)

        def group(first, count):
            for n in range(count):
                pair(first + n, full_modes)

        trips = unmasked // ATT_PAIRS_PER_TRIP
        lax.fori_loop(0, trips, lambda j, carry: (group(j * ATT_PAIRS_PER_TRIP, ATT_PAIRS_PER_TRIP), carry)[1], 0)
        done = trips * ATT_PAIRS_PER_TRIP
        size = ATT_PAIRS_PER_TRIP // 2
        while size >= 1:
            @pl.when((unmasked & size) != 0)
            def _(done=done, size=size):
                group(done, size)

            done = done + (unmasked & size)
            size //= 2

        @pl.when(i > 0)
        def _():
            pair(i - 1, even_modes)

        step(probs_of=(2 * i + 1, 1, last_modes), weigh_of=(2 * i, 0, full_modes))
        step(weigh_of=(2 * i + 1, 1, last_modes))

    @pl.when(bounded_ref[0] == 0)
    def _():
        scores(0, 0, even_modes)

        def pair(p, carry):
            scores(2 * p + 1, 1, full_modes)
            accumulate(2 * p, 0, full_modes)
            scores(2 * p + 2, 0, even_modes)
            accumulate(2 * p + 1, 1, full_modes)
            return carry

        lax.fori_loop(0, i, pair, 0)
        scores(2 * i + 1, 1, last_modes)
        accumulate(2 * i, 0, full_modes)
        accumulate(2 * i + 1, 1, last_modes)

    s1 = jnp.sum(lq1_ref[...] * lk1_ref[...], axis=1, keepdims=True)
    s2 = jnp.sum(lq2_ref[...] * lk2_ref[...], axis=1, keepdims=True)
    lam = jnp.exp(s1) - jnp.exp(s2) + lam_init
    o = acc_ref[0] / l_ref[0] - lam * (acc_ref[1] / l_ref[1])
    ms = jnp.mean(o * o, axis=0, keepdims=True)
    y = o * lax.rsqrt(ms + EPS) * subg_ref[...] * (1.0 - lam_init)
    o_ref[...] = y.T.astype(BF16)


ATT_DIRECT_LOG2 = 18.0


def _scores_bounded(q_gain, k_gain):
    q_scale = (DA_HEAD_DIM ** -0.5) * math.log2(math.e)
    bound = 1.02 * DA_HEAD_DIM * q_scale * jnp.max(jnp.abs(q_gain)) * jnp.max(jnp.abs(k_gain))
    return (bound <= ATT_DIRECT_LOG2).astype(jnp.int32).reshape(1)


def _attention(qt, k, vt, q_gain, k_gain, lq1, lk1, lq2, lk2, subln_g, layer_idx):
    bsz, heads, nq, _, _, tq = qt.shape
    nkb, vdim, tk = vt.shape[2:]
    seq = nq * tq
    assert (tq, tk, vdim) == (ATT_TQ, ATT_TK, DA_V_DIM) and tq == 2 * tk and nkb * tk == seq
    lam_init = 0.8 - 0.6 * math.exp(-0.3 * layer_idx)
    vec = lambda v: v.astype(F32).reshape(1, DA_HEAD_DIM)
    vec_spec = pl.BlockSpec((1, DA_HEAD_DIM), lambda b, h, i: (0, 0))
    return pl.pallas_call(
        functools.partial(_attn_kernel, lam_init=lam_init),
        grid=(bsz, heads, nq),
        in_specs=[pl.BlockSpec(memory_space=pltpu.SMEM),
                  pl.BlockSpec((1, 1, 1, 2, LANES, tq), lambda b, h, i: (b, h, i, 0, 0, 0)),
                  pl.BlockSpec((1, 1, seq, LANES), lambda b, h, i: (b, h, 0, 0)),
                  pl.BlockSpec((1, 1, nkb, vdim, tk), lambda b, h, i: (b, h, 0, 0, 0)),
                  vec_spec, vec_spec, vec_spec, vec_spec,
                  pl.BlockSpec((DA_V_DIM, 1), lambda b, h, i: (0, 0))],
        out_specs=pl.BlockSpec((tq, LANES), lambda b, h, i: (b * nq + i, h)),
        out_shape=jax.ShapeDtypeStruct((bsz * seq, DA_WIDTH), BF16),
        scratch_shapes=[pltpu.VMEM((2, 2, tk, tq), F32),
                        pltpu.VMEM((2, 2, tk, tq), BF16),
                        pltpu.VMEM((2, 2, 1, tq), F32),
                        pltpu.VMEM((2, vdim, tq), F32),
                        pltpu.VMEM((2, 1, tq), F32),
                        pltpu.VMEM((2, 1, tq), F32),
                        pltpu.VMEM((2, LANES, tq), FP8)],
        compiler_params=_params("parallel", "parallel", "arbitrary"),
        name="attn",
    )(_scores_bounded(q_gain, k_gain), qt, k, vt, vec(lq1), vec(lk1), vec(lq2), vec(lk2),
      subln_g.astype(F32).reshape(DA_V_DIM, 1))


def _gla_decay_matrix():
    c = GLA_CHUNK
    t = np.arange(c)[:, None]
    s = np.arange(c)[None, :]
    blocks = [(s <= t)]
    for h in GLA_LEVELS:
        r = (t // (2 * h)) * (2 * h) + h
        upper = (t >= r) & (s > r) & (s <= t)
        lower = (t < r) & (s > t) & (s <= r)
        blocks.append(upper | lower)
    return jnp.asarray(np.concatenate(blocks, axis=0), dtype=BF16)


def _gla_kernel(q_ref, k_ref, v_ref, la_ref, sr_ref, fm_ref, g_ref, o_ref, st_ref, *, nchunk):
    @pl.when(pl.program_id(1) == 0)
    def _():
        st_ref[...] = jnp.zeros_like(st_ref)

    c = GLA_CHUNK
    ii = lax.broadcasted_iota(jnp.int32, (c, c), 0)
    jj = lax.broadcasted_iota(jnp.int32, (c, c), 1)
    eye = ii == jj
    level_masks = []
    for h in GLA_LEVELS:
        shift = int(math.log2(2 * h))
        same = (ii >> shift) == (jj >> shift)
        level_masks.append(same & ((ii & (2 * h - 1)) >= h) & ((jj & (2 * h - 1)) < h))
    fm = fm_ref[...]
    gain = g_ref[...]

    heads = range(GLA_HEADS)
    ksl = [slice(hd * GLA_KEY_DIM, (hd + 1) * GLA_KEY_DIM) for hd in heads]
    vsl = [slice(hd * GLA_VAL_DIM, (hd + 1) * GLA_VAL_DIM) for hd in heads]

    def local_part(rows):
        la = la_ref[rows, :]
        la_hi, la_lo = _split_bf16(la)
        fa = _dot(fm, la_hi) + _dot(fm, la_lo)
        b = fa[0:c]
        b_last = b[c - 1:c]
        e_b = jnp.exp(b)
        e_rest = jnp.exp(b_last - b)
        e_last = jnp.exp(b_last)
        e_lvl = [jnp.exp(fa[(l + 1) * c:(l + 2) * c]) for l in range(len(GLA_LEVELS))]
        qb = [q_ref[rows, ksl[hd]] for hd in heads]
        kb = [k_ref[rows, ksl[hd]] for hd in heads]
        q = [x.astype(F32) for x in qb]
        k = [x.astype(F32) for x in kb]
        v = [v_ref[rows, vsl[hd]] for hd in heads]
        att = [jnp.where(eye, _dot_nt(qb[hd], kb[hd]), 0.0) for hd in heads]
        for l in range(len(GLA_LEVELS)):
            for hd in heads:
                e = e_lvl[l][:, ksl[hd]]
                lvl = _dot_nt((q[hd] * e).astype(BF16), (k[hd] * e).astype(BF16))
                att[hd] = jnp.where(level_masks[l], lvl, att[hd])
        upd = [_dot_tn(v[hd], (k[hd] * e_rest[:, ksl[hd]]).astype(BF16)) for hd in heads]
        intra = [_dot(att[hd].astype(BF16), v[hd]) for hd in heads]
        qd = [(q[hd] * e_b[:, ksl[hd]]).astype(BF16) for hd in heads]
        return intra, qd, upd, e_last

    def carried_part(rows, local):
        intra, qd, upd, e_last = local
        for hd in heads:
            state = st_ref[hd]
            out = intra[hd] + _dot_nt(qd[hd], state.astype(BF16))
            ms = jnp.mean(out * out, axis=-1, keepdims=True)
            y = out * lax.rsqrt(ms + EPS) * gain * sr_ref[rows, vsl[hd]].astype(F32)
            o_ref[rows, vsl[hd]] = y.astype(BF16)
            st_ref[hd] = state * e_last[:, ksl[hd]] + upd[hd]

    def group(g, carry):
        rows = [pl.ds(pl.multiple_of((g * GLA_GROUP + j) * c, c), c) for j in range(GLA_GROUP)]
        local = [local_part(r) for r in rows]
        for r, loc in zip(rows, local):
            carried_part(r, loc)
        return carry

    lax.fori_loop(0, nchunk // GLA_GROUP, group, 0)


def _gla(gq, gk, act, la, out_gain, bsz, seq):
    t = gq.shape[0]
    tm = min(GLA_TM, seq)
    tpb = seq // tm
    fm = _gla_decay_matrix()
    row = lambda b, n: (b * tpb + n, 0)
    return pl.pallas_call(
        functools.partial(_gla_kernel, nchunk=tm // GLA_CHUNK),
        grid=(bsz, tpb),
        in_specs=[pl.BlockSpec((tm, GLA_QK_WIDTH), row),
                  pl.BlockSpec((tm, GLA_QK_WIDTH), row),
                  pl.BlockSpec((tm, GLA_WIDTH), row),
                  pl.BlockSpec((tm, GLA_QK_WIDTH), row),
                  pl.BlockSpec((tm, GLA_WIDTH), lambda b, n: (b * tpb + n, 1)),
                  pl.BlockSpec(fm.shape, lambda b, n: (0, 0)),
                  pl.BlockSpec((1, GLA_VAL_DIM), lambda b, n: (0, 0))],
        out_specs=pl.BlockSpec((tm, GLA_WIDTH), row),
        out_shape=jax.ShapeDtypeStruct((t, GLA_WIDTH), BF16),
        scratch_shapes=[pltpu.VMEM((GLA_HEADS, GLA_VAL_DIM, GLA_KEY_DIM), F32)],
        compiler_params=_params("parallel", "arbitrary"),
        name="gla",
    )(gq, gk, act, la, act, fm, out_gain.astype(F32).reshape(1, GLA_VAL_DIM))


def _merge_kernel(ya_ref, yb_ref, sa_ref, sb_ref, x_ref, mod_ref, wa_ref, wb_ref, wo_ref, o_ref, mg_ref):
    ya = ya_ref[...]
    yb = yb_ref[...]
    for n in range(mg_ref.shape[1] // MERGE_TN):
        cols = slice(n * MERGE_TN, (n + 1) * MERGE_TN)
        ta = _dot(ya, wa_ref[:, cols])
        tb = _dot(yb, wb_ref[:, cols])
        mg_ref[:, cols] = (sa_ref[:, cols].astype(F32) * ta + sb_ref[:, cols].astype(F32) * tb).astype(BF16)
    o_ref[...] = x_ref[...] + mod_ref[0][2:3] * _dot(mg_ref[...], wo_ref[...])


def _merge(ya, yb, act, x2, mod, w_a, w_b, w_o, seq):
    t, d = x2.shape
    tm = min(MERGE_TM, seq)
    tpb = seq // tm
    gate_blk = 2 * GLA_WIDTH // d
    return pl.pallas_call(
        _merge_kernel,
        grid=(t // tm,),
        in_specs=[pl.BlockSpec((tm, DA_WIDTH), lambda i: (i, 0)),
                  pl.BlockSpec((tm, GLA_WIDTH), lambda i: (i, 0)),
                  pl.BlockSpec((tm, d), lambda i: (i, gate_blk)),
                  pl.BlockSpec((tm, d), lambda i: (i, gate_blk + 1)),
                  pl.BlockSpec((tm, d), lambda i: (i, 0)),
                  pl.BlockSpec((1, N_MOD, d), lambda i: (i // tpb, 0, 0)),
                  _resident(w_a.shape), _resident(w_b.shape), _resident(w_o.shape)],
        out_specs=pl.BlockSpec((tm, d), lambda i: (i, 0)),
        out_shape=jax.ShapeDtypeStruct((t, d), F32),
        scratch_shapes=[pltpu.VMEM((tm, d), BF16)],
        compiler_params=_params("parallel"),
        name="merge",
    )(ya, yb, act, act, x2, mod, w_a, w_b, w_o)


def _mlp_kernel(h_ref, g_ref, mod_ref, w1_ref, w2_ref, o_ref, u_ref):
    @pl.when(pl.program_id(1) == 0)
    def _():
        h = h_ref[...]
        u_ref[...] = _modulated_norm(h, g_ref[...], mod_ref[0], 3, 4).astype(BF16)
        o_ref[...] = h

    hid = jnp.square(jnp.maximum(_dot(u_ref[...], w1_ref[...]), 0.0))
    o_ref[...] += mod_ref[0][5:6] * _dot(hid.astype(BF16), w2_ref[...])


def _mlp(h, gain, mod, w1, w2, seq):
    t, d = h.shape
    tm = min(MLP_TM, seq)
    tpb = seq // tm
    ff = w1.shape[1]
    return pl.pallas_call(
        _mlp_kernel,
        grid=(t // tm, ff // MLP_TF),
        in_specs=[pl.BlockSpec((tm, d), lambda i, kf: (i, 0)),
                  pl.BlockSpec((1, d), lambda i, kf: (0, 0)),
                  pl.BlockSpec((1, N_MOD, d), lambda i, kf: (i // tpb, 0, 0)),
                  pl.BlockSpec((d, MLP_TF), lambda i, kf: (0, kf)),
                  pl.BlockSpec((MLP_TF, d), lambda i, kf: (kf, 0))],
        out_specs=pl.BlockSpec((tm, d), lambda i, kf: (i, 0)),
        out_shape=jax.ShapeDtypeStruct((t, d), F32),
        scratch_shapes=[pltpu.VMEM((tm, d), BF16)],
        compiler_params=_params("parallel", "arbitrary"),
        name="mlp",
    )(h, gain.reshape(1, d), mod, w1, w2)


def _layer(h2, c, positions, bsz, seq, layer_idx, w_ada, b_ada, norm1_g, w_in, da_q_norm_g, da_k_norm_g,
           lq1, lk1, lq2, lk2, da_subln_g, gla_gate_up, gla_gate_bias, gla_out_norm_g,
           w_branch_a, w_branch_b, w_out, norm2_g, w_mlp_in, w_mlp_out):
    d = h2.shape[1]
    assert seq % min(PROJ_TM, seq) == 0 and seq % GLA_CHUNK == 0
    offs = np.concatenate([[0], np.cumsum(IN_SIZES)])
    col = lambda a, b: w_in[:, offs[a]:offs[b]].astype(BF16)
    w_qkv = col(0, 3)
    w_g = jnp.concatenate([col(3, 5), col(7, 8), jnp.zeros((d, LANES - GLA_GATE_RANK), BF16)], axis=1)
    w_act = jnp.concatenate([col(5, 7), col(8, 10)], axis=1)

    mod = _ada(c, w_ada, b_ada)
    u, qt, k, vt = _qkv(h2, norm1_g, mod, w_qkv, positions, da_q_norm_g, da_k_norm_g, bsz, seq)
    gq, gk, la = _glaqk(u, w_g, gla_gate_up, gla_gate_bias, seq)
    act = _act(u, w_act, seq)
    ya = _attention(qt, k, vt, da_q_norm_g, da_k_norm_g, lq1, lk1, lq2, lk2, da_subln_g, layer_idx)
    yb = _gla(gq, gk, act, la, gla_out_norm_g, bsz, seq)
    h1 = _merge(ya, yb, act, h2, mod, w_branch_a.astype(BF16), w_branch_b.astype(BF16),
                w_out.astype(BF16), seq)
    return _mlp(h1, norm2_g, mod, w_mlp_in.astype(BF16), w_mlp_out.astype(BF16), seq)


def kernel(x, c, positions, w_ada, b_ada, norm1_g, w_in, da_q_norm_g, da_k_norm_g, da_lambda_q1,
           da_lambda_k1, da_lambda_q2, da_lambda_k2, da_subln_g, gla_gate_up, gla_gate_bias,
           gla_out_norm_g, w_branch_a, w_branch_b, w_out, norm2_g, w_mlp_in, w_mlp_out):
    bsz, seq, d = x.shape
    h = x.reshape(bsz * seq, d)
    for l in range(w_ada.shape[0]):
        h = _layer(h, c, positions, bsz, seq, l, w_ada[l], b_ada[l], norm1_g[l], w_in[l], da_q_norm_g[l],
                   da_k_norm_g[l], da_lambda_q1[l], da_lambda_k1[l], da_lambda_q2[l], da_lambda_k2[l],
                   da_subln_g[l], gla_gate_up[l], gla_gate_bias[l], gla_out_norm_g[l], w_branch_a[l],
                   w_branch_b[l], w_out[l], norm2_g[l], w_mlp_in[l], w_mlp_out[l])
    return h.reshape(bsz, seq, d)
```

```python
import functools
import math

import jax
import jax.numpy as jnp
import numpy as np
from jax import lax
from jax.experimental import pallas as pl
from jax.experimental.pallas import tpu as pltpu

F32 = jnp.float32
BF16 = jnp.bfloat16

D_MODEL = 2048
DA_HEADS = 8
DA_HEAD_DIM = 64
DA_V_DIM = 2 * DA_HEAD_DIM
DA_QK_WIDTH = DA_HEADS * 2 * DA_HEAD_DIM
DA_WIDTH = DA_HEADS * DA_V_DIM
ROPE_THETA = 500000.0
ROPE_DIM = DA_HEAD_DIM // 4
GLA_HEADS = 4
GLA_KEY_DIM = 128
GLA_VAL_DIM = 256
GLA_QK_WIDTH = GLA_HEADS * GLA_KEY_DIM
GLA_WIDTH = GLA_HEADS * GLA_VAL_DIM
GLA_GATE_RANK = 16
GLA_GATE_TAU = 16.0
GLA_CHUNK = 64
D_FF = 4 * D_MODEL
N_MOD = 6
EPS = 1e-6
IN_SIZES = (DA_QK_WIDTH, DA_QK_WIDTH, DA_WIDTH, GLA_QK_WIDTH, GLA_QK_WIDTH, GLA_WIDTH, GLA_WIDTH,
            GLA_GATE_RANK, D_MODEL, D_MODEL)

LANES = 128
SUBLANES = 8
VMEM_LIMIT_BYTES = 56 * 1024 * 1024

ADA_TN = 1024
PROJ_TM = 512
ACT_TM = 1024
ACT_TN = 1024
ACT_PARTS = 4
GLAQK_PARTS = 2
ATT_TK = PROJ_TM
ATT_TQ = 2 * ATT_TK
ATT_QC = 256
ATT_PAIRS_PER_TRIP = 4
GLA_TM = 512
GLA_GROUP = 4
MERGE_TM = 512
MERGE_TN = 512
MLP_TM = 1024
MLP_TF = 512

GLA_LEVELS = (32, 16, 8, 4, 2, 1)


def _dot(a, b):
    return jnp.dot(a, b, preferred_element_type=F32)


def _dot_nt(a, b):
    return lax.dot_general(a, b, (((1,), (1,)), ((), ())), preferred_element_type=F32)


def _dot_tn(a, b):
    return lax.dot_general(a, b, (((0,), (0,)), ((), ())), preferred_element_type=F32)


def _split_bf16(x):
    hi = x.astype(BF16)
    lo = (x - hi.astype(F32)).astype(BF16)
    return hi, lo


def _params(*semantics):
    return pltpu.CompilerParams(dimension_semantics=semantics, vmem_limit_bytes=VMEM_LIMIT_BYTES)


def _resident(shape):
    nd = len(shape)
    return pl.BlockSpec(shape, lambda *_: (0,) * nd, pipeline_mode=pl.Buffered(1))


def _ada_kernel(c_ref, w_ref, b_ref, o_ref):
    c = c_ref[...]
    sc = c * jax.nn.sigmoid(c)
    o_ref[...] = _dot(sc.astype(BF16), w_ref[...].astype(BF16)) + b_ref[...]


def _ada(c, w_ada, b_ada):
    bsz, d = c.shape
    n = w_ada.shape[1]
    c8 = jnp.zeros((SUBLANES, d), F32).at[:bsz].set(c)
    out = pl.pallas_call(
        _ada_kernel,
        grid=(n // ADA_TN,),
        in_specs=[pl.BlockSpec((SUBLANES, d), lambda j: (0, 0)),
                  pl.BlockSpec((d, ADA_TN), lambda j: (0, j)),
                  pl.BlockSpec((1, ADA_TN), lambda j: (0, j))],
        out_specs=pl.BlockSpec((SUBLANES, ADA_TN), lambda j: (0, j)),
        out_shape=jax.ShapeDtypeStruct((SUBLANES, n), F32),
        compiler_params=_params("arbitrary"),
        name="ada",
    )(c8, w_ada, b_ada.reshape(1, n))
    return out[:bsz].reshape(bsz, N_MOD, d)


def _modulated_norm(x, gain, mod, shift_idx, scale_idx):
    ms = jnp.mean(x * x, axis=-1, keepdims=True)
    y = x * lax.rsqrt(ms + EPS) * gain
    return y * (1.0 + mod[scale_idx:scale_idx + 1]) + mod[shift_idx:shift_idx + 1]


QK_DTYPE = BF16
FP8 = jnp.float8_e4m3fn
GROUP_COLS = 256


def _qkv_kernel(x_ref, g1_ref, mod_ref, w_ref, pos_ref, freq_ref, sa_ref, sb_ref, gq_ref, gk_ref, grp_ref,
                u_ref, qt_ref, k_ref, vt_ref):
    u = _modulated_norm(x_ref[...], g1_ref[...], mod_ref[0], 0, 1).astype(BF16)
    u_ref[...] = u
    ang = pos_ref[...].astype(F32) * freq_ref[...]
    cs = jnp.cos(ang)
    sn = jnp.sin(ang)
    sin_a = sn * sa_ref[...]
    sin_b = sn * sb_ref[...]
    grp = grp_ref[...]
    half = ROPE_DIM // 2

    def group_sums(acc):
        return [_dot(jnp.square(acc[:, j * GROUP_COLS:(j + 1) * GROUP_COLS]).astype(BF16), grp)
                for j in range(acc.shape[1] // GROUP_COLS)]

    def norm_rope(acc, sums, gain):
        slabs = []
        for j, ss in enumerate(sums):
            xn = acc[:, j * GROUP_COLS:(j + 1) * GROUP_COLS] * lax.rsqrt(ss * (1.0 / DA_HEAD_DIM) + EPS)
            for s in range(GROUP_COLS // LANES):
                xs = xn[:, s * LANES:(s + 1) * LANES] * gain
                slabs.append(xs * cs + pltpu.roll(xs, LANES - half, 1) * sin_a + pltpu.roll(xs, half, 1) * sin_b)
        return slabs

    q_scale = (DA_HEAD_DIM ** -0.5) * math.log2(math.e)
    tm = u.shape[0]
    row = lax.broadcasted_iota(jnp.int32, (LANES, tm), 0)
    acc = _dot(u, w_ref[:, 0:DA_QK_WIDTH])
    for h, xq in enumerate(norm_rope(acc, group_sums(acc), gq_ref[...])):
        xt = (xq * q_scale).T
        qt_ref[0, h, 0, 0] = jnp.where(row < DA_HEAD_DIM, xt, 0.0).astype(QK_DTYPE)
        qt_ref[0, h, 0, 1] = jnp.where(row >= DA_HEAD_DIM, xt, 0.0).astype(QK_DTYPE)
    acc = _dot(u, w_ref[:, DA_QK_WIDTH:2 * DA_QK_WIDTH])
    for h, xk in enumerate(norm_rope(acc, group_sums(acc), gk_ref[...])):
        k_ref[0, h] = xk.astype(QK_DTYPE)
    acc = _dot(u, w_ref[:, 2 * DA_QK_WIDTH:2 * DA_QK_WIDTH + DA_WIDTH])
    for h in range(DA_HEADS):
        vt_ref[0, h, 0] = acc[:, h * LANES:(h + 1) * LANES].T.astype(BF16)


def _rope_lane_tables():
    lane = np.arange(LANES) % DA_HEAD_DIM
    inv_freq = ROPE_THETA ** (-jnp.arange(0, ROPE_DIM, 2, dtype=F32) / ROPE_DIM)
    half = ROPE_DIM // 2
    freq = jnp.where(lane < ROPE_DIM, inv_freq[lane % half], 0.0).astype(F32).reshape(1, LANES)
    sign_a = np.where(lane < half, -1.0, 0.0).astype(np.float32).reshape(1, LANES)
    mask_b = np.where((lane >= half) & (lane < ROPE_DIM), 1.0, 0.0).astype(np.float32).reshape(1, LANES)
    col = np.arange(GROUP_COLS)
    group = col[:, None] // DA_HEAD_DIM == col[None, :] // DA_HEAD_DIM
    return freq, jnp.asarray(sign_a), jnp.asarray(mask_b), jnp.asarray(group, dtype=BF16)


def _qkv(x2, gain, mod, w_qkv, positions, q_gain, k_gain, bsz, seq):
    t, d = x2.shape
    tm = min(PROJ_TM, seq)
    tpb = seq // tm
    freq, sign_a, mask_b, group = _rope_lane_tables()
    lane_vec = lambda v: jnp.tile(v.astype(F32), LANES // DA_HEAD_DIM).reshape(1, LANES)
    vec_spec = pl.BlockSpec((1, LANES), lambda i: (0, 0))
    assert tm == ATT_TK and seq % ATT_TQ == 0
    per_q = ATT_TQ // tm
    qt_shape = jax.ShapeDtypeStruct((bsz, DA_HEADS, seq // ATT_TQ, 2, LANES, ATT_TQ), QK_DTYPE)
    qt_spec = pl.BlockSpec((1, DA_HEADS, 1, 2, LANES, tm),
                           lambda i: (i // tpb, 0, (i % tpb) // per_q, 0, 0, (i % tpb) % per_q))
    vt_shape = jax.ShapeDtypeStruct((bsz, DA_HEADS, tpb, DA_V_DIM, tm), BF16)
    vt_spec = pl.BlockSpec((1, DA_HEADS, 1, DA_V_DIM, tm), lambda i: (i // tpb, 0, i % tpb, 0, 0))
    return pl.pallas_call(
        _qkv_kernel,
        grid=(t // tm,),
        in_specs=[pl.BlockSpec((tm, d), lambda i: (i, 0)),
                  pl.BlockSpec((1, d), lambda i: (0, 0)),
                  pl.BlockSpec((1, N_MOD, d), lambda i: (i // tpb, 0, 0)),
                  _resident(w_qkv.shape),
                  pl.BlockSpec((tm, 1), lambda i: (i, 0)),
                  vec_spec, vec_spec, vec_spec, vec_spec, vec_spec,
                  pl.BlockSpec((GROUP_COLS, GROUP_COLS), lambda i: (0, 0))],
        out_specs=[pl.BlockSpec((tm, d), lambda i: (i, 0)),
                   qt_spec,
                   pl.BlockSpec((1, DA_HEADS, tm, LANES), lambda i: (i // tpb, 0, i % tpb, 0)),
                   vt_spec],
        out_shape=[jax.ShapeDtypeStruct((t, d), BF16), qt_shape,
                   jax.ShapeDtypeStruct((bsz, DA_HEADS, seq, LANES), QK_DTYPE), vt_shape],
        compiler_params=_params("parallel"),
        name="qkv",
    )(x2, gain.reshape(1, d), mod, w_qkv, positions.reshape(t, 1), freq, sign_a, mask_b,
      lane_vec(q_gain), lane_vec(k_gain), group)


def _glaqk_kernel(u_ref, w_ref, up_ref, bias_ref, q_ref, k_ref, la_ref):
    w = w_ref[...]
    up_hi, up_lo = _split_bf16(up_ref[...])
    parts = _row_parts(u_ref.shape[0], GLAQK_PARTS)
    accs = [_dot(u_ref[rows, :], w) for rows in parts]
    for rows, acc in zip(parts, accs):
        q_ref[rows, :] = (acc[:, 0:GLA_QK_WIDTH] * (GLA_KEY_DIM ** -0.5)).astype(BF16)
        k_ref[rows, :] = acc[:, GLA_QK_WIDTH:2 * GLA_QK_WIDTH].astype(BF16)
        low = acc[:, 2 * GLA_QK_WIDTH:]
        low_hi, low_lo = _split_bf16(low)
        z = _dot(low_hi, up_hi) + _dot(low_lo, up_hi) + _dot(low_hi, up_lo) + bias_ref[...]
        log_sig = jnp.minimum(z, 0.0) - jnp.log1p(jnp.exp(-jnp.abs(z)))
        la_ref[rows, :] = log_sig * (1.0 / GLA_GATE_TAU)


def _glaqk(u, w_g, gate_up, gate_bias, seq):
    t, d = u.shape
    tm = min(PROJ_TM, seq)
    up = jnp.zeros((LANES, GLA_QK_WIDTH), F32).at[:GLA_GATE_RANK].set(gate_up)
    row_spec = lambda w: pl.BlockSpec((tm, w), lambda i: (i, 0))
    return pl.pallas_call(
        _glaqk_kernel,
        grid=(t // tm,),
        in_specs=[row_spec(d), _resident(w_g.shape), _resident(up.shape),
                  pl.BlockSpec((1, GLA_QK_WIDTH), lambda i: (0, 0))],
        out_specs=[row_spec(GLA_QK_WIDTH), row_spec(GLA_QK_WIDTH), row_spec(GLA_QK_WIDTH)],
        out_shape=[jax.ShapeDtypeStruct((t, GLA_QK_WIDTH), BF16),
                   jax.ShapeDtypeStruct((t, GLA_QK_WIDTH), BF16),
                   jax.ShapeDtypeStruct((t, GLA_QK_WIDTH), F32)],
        compiler_params=_params("parallel"),
        name="glaqk",
    )(u, w_g, up, gate_bias.reshape(1, GLA_QK_WIDTH))


ACT_SILU_BLOCKS = (GLA_WIDTH // ACT_TN, 2 * GLA_WIDTH // ACT_TN)


def _row_parts(rows, parts):
    step = rows // parts
    return [slice(r * step, (r + 1) * step) for r in range(parts)]


def _act_kernel(u_ref, w_ref, o_ref):
    j = pl.program_id(0)
    w = w_ref[...]
    parts = _row_parts(u_ref.shape[0], ACT_PARTS)
    accs = [_dot(u_ref[rows, :], w) for rows in parts]
    for rows, acc in zip(parts, accs):
        sig = jax.nn.sigmoid(acc)
        out = jnp.where(j < ACT_SILU_BLOCKS[0], acc, jnp.where(j < ACT_SILU_BLOCKS[1], acc * sig, sig))
        o_ref[rows, :] = out.astype(BF16)


def _act(u, w_act, seq):
    t, d = u.shape
    n = w_act.shape[1]
    tm = min(ACT_TM, seq)
    return pl.pallas_call(
        _act_kernel,
        grid=(n // ACT_TN, t // tm),
        in_specs=[pl.BlockSpec((tm, d), lambda j, i: (i, 0)),
                  pl.BlockSpec((d, ACT_TN), lambda j, i: (0, j))],
        out_specs=pl.BlockSpec((tm, ACT_TN), lambda j, i: (i, j)),
        out_shape=jax.ShapeDtypeStruct((t, n), BF16),
        compiler_params=_params("arbitrary", "arbitrary"),
        name="act",
    )(u, w_act)


_FULL, _MASK, _SKIP = "full", "mask", "skip"


def _attn_kernel(bounded_ref, qt_ref, k_ref, vt_ref, lq1_ref, lk1_ref, lq2_ref, lk2_ref, subg_ref, o_ref,
                 s_ref, p_ref, cm_ref, acc_ref, m_ref, l_ref, q8_ref, *, lam_init):
    i = pl.program_id(2)
    tq, tk, qc_w = ATT_TQ, ATT_TK, ATT_QC
    nqc = tq // qc_w
    acc_ref[...] = jnp.zeros_like(acc_ref)
    l_ref[...] = jnp.zeros_like(l_ref)
    m_ref[...] = jnp.full_like(m_ref, -jnp.inf)

    def scores(t, slot, modes):
        kb = k_ref[0, 0, pl.ds(pl.multiple_of(t * tk, tk), tk), :]
        for c in range(2):
            for q in range(nqc):
                if modes[q] == _SKIP:
                    continue
                cols = slice(q * qc_w, (q + 1) * qc_w)
                s = _dot(kb, qt_ref[0, 0, 0, c, :, cols])
                if modes[q] == _MASK:
                    rel = (lax.broadcasted_iota(jnp.int32, (tk, qc_w), 0)
                           - lax.broadcasted_iota(jnp.int32, (tk, qc_w), 1))
                    s = jnp.where(rel <= i * tq + q * qc_w - t * tk, s, -jnp.inf)
                s_ref[slot, c, :, cols] = s
                cm_ref[slot, c, :, cols] = jnp.max(s, axis=0, keepdims=True)

    def accumulate(t, slot, modes):
        vtb = vt_ref[0, 0, t]
        for c in range(2):
            for q in range(nqc):
                if modes[q] == _SKIP:
                    continue
                cols = slice(q * qc_w, (q + 1) * qc_w)
                m_old = m_ref[c, :, cols]
                m_new = jnp.maximum(m_old, cm_ref[slot, c, :, cols])
                alpha = jnp.exp2(m_old - m_new)
                p = jnp.exp2(s_ref[slot, c, :, cols] - m_new)
                l_ref[c, :, cols] = alpha * l_ref[c, :, cols] + jnp.sum(p, axis=0, keepdims=True)
                acc_ref[c, :, cols] = alpha * acc_ref[c, :, cols] + _dot(vtb, p.astype(BF16))
                m_ref[c, :, cols] = m_new

    half = tk // qc_w
    even_modes = (_MASK,) * half + (_FULL,) * (nqc - half)
    full_modes = (_FULL,) * nqc
    last_modes = (_SKIP,) * half + (_MASK,) * (nqc - half)

    def step(probs_of=None, weigh_of=None):
        if probs_of is not None:
            pt, pslot, pmodes = probs_of
            kb = k_ref[0, 0, pl.ds(pl.multiple_of(pt * tk, tk), tk), :].astype(F32).astype(FP8)
        if weigh_of is not None:
            wt, wslot, wmodes = weigh_of
            vtb = vt_ref[0, 0, wt]
        for c in range(2):
            for q in range(nqc):
                cols = slice(q * qc_w, (q + 1) * qc_w)
                if probs_of is not None and pmodes[q] != _SKIP:
                    s = _dot(kb, q8_ref[c, :, cols])
                    if pmodes[q] == _MASK:
                        rel = (lax.broadcasted_iota(jnp.int32, (tk, qc_w), 0)
                               - lax.broadcasted_iota(jnp.int32, (tk, qc_w), 1))
                        s = jnp.where(rel <= i * tq + q * qc_w - pt * tk, s, -jnp.inf)
                    p = jnp.exp2(s)
                    l_ref[c, :, cols] += jnp.sum(p, axis=0, keepdims=True)
                    p_ref[pslot, c, :, cols] = p.astype(BF16)
                if weigh_of is not None and wmodes[q] != _SKIP:
                    acc_ref[c, :, cols] += _dot(vtb, p_ref[wslot, c, :, cols])

    @pl.when(bounded_ref[0] != 0)
    def _():
        q8_ref[...] = qt_ref[0, 0, 0].astype(F32).astype(FP8)
        step(probs_of=(0, 0, even_modes))

        def pair(p, next_modes):
            step(probs_of=(2 * p + 1, 1, full_modes), weigh_of=(2 * p, 0, full_modes))
            step(probs_of=(2 * p + 2, 0, next_modes), weigh_of=(2 * p + 1, 1, full_modes))

        unmasked = jnp.maximum(i - 1, 0)

        def group(first, count):
            for n in range(count):
                pair(first + n, full_modes)

        trips = unmasked // ATT_PAIRS_PER_TRIP
        lax.fori_loop(0, trips, lambda j, carry: (group(j * ATT_PAIRS_PER_TRIP, ATT_PAIRS_PER_TRIP), carry)[1], 0)
        done = trips * ATT_PAIRS_PER_TRIP
        size = ATT_PAIRS_PER_TRIP // 2
        while size >= 1:
            @pl.when((unmasked & size) != 0)
            def _(done=done, size=size):
                group(done, size)

            done = done + (unmasked & size)
            size //= 2

        def diagonal_blocks():
            step(probs_of=(2 * i + 1, 1, last_modes), weigh_of=(2 * i, 0, full_modes))
            step(weigh_of=(2 * i + 1, 1, last_modes))

        @pl.when(i > 0)
        def _():
            pair(i - 1, even_modes)
            diagonal_blocks()

        @pl.when(i == 0)
        def _():
            diagonal_blocks()

    @pl.when(bounded_ref[0] == 0)
    def _():
        scores(0, 0, even_modes)

        def pair(p, carry):
            scores(2 * p + 1, 1, full_modes)
            accumulate(2 * p, 0, full_modes)
            scores(2 * p + 2, 0, even_modes)
            accumulate(2 * p + 1, 1, full_modes)
            return carry

        lax.fori_loop(0, i, pair, 0)
        scores(2 * i + 1, 1, last_modes)
        accumulate(2 * i, 0, full_modes)
        accumulate(2 * i + 1, 1, last_modes)

    s1 = jnp.sum(lq1_ref[...] * lk1_ref[...], axis=1, keepdims=True)
    s2 = jnp.sum(lq2_ref[...] * lk2_ref[...], axis=1, keepdims=True)
    lam = jnp.exp(s1) - jnp.exp(s2) + lam_init
    o = acc_ref[0] / l_ref[0] - lam * (acc_ref[1] / l_ref[1])
    ms = jnp.mean(o * o, axis=0, keepdims=True)
    y = o * lax.rsqrt(ms + EPS) * subg_ref[...] * (1.0 - lam_init)
    o_ref[...] = y.T.astype(BF16)


ATT_DIRECT_LOG2 = 18.0


def _scores_bounded(q_gain, k_gain):
    q_scale = (DA_HEAD_DIM ** -0.5) * math.log2(math.e)
    bound = 1.02 * DA_HEAD_DIM * q_scale * jnp.max(jnp.abs(q_gain)) * jnp.max(jnp.abs(k_gain))
    return (bound <= ATT_DIRECT_LOG2).astype(jnp.int32).reshape(1)


def _attention(qt, k, vt, q_gain, k_gain, lq1, lk1, lq2, lk2, subln_g, layer_idx):
    bsz, heads, nq, _, _, tq = qt.shape
    nkb, vdim, tk = vt.shape[2:]
    seq = nq * tq
    assert (tq, tk, vdim) == (ATT_TQ, ATT_TK, DA_V_DIM) and tq == 2 * tk and nkb * tk == seq
    lam_init = 0.8 - 0.6 * math.exp(-0.3 * layer_idx)
    vec = lambda v: v.astype(F32).reshape(1, DA_HEAD_DIM)
    vec_spec = pl.BlockSpec((1, DA_HEAD_DIM), lambda b, h, i: (0, 0))
    return pl.pallas_call(
        functools.partial(_attn_kernel, lam_init=lam_init),
        grid=(bsz, heads, nq),
        in_specs=[pl.BlockSpec(memory_space=pltpu.SMEM),
                  pl.BlockSpec((1, 1, 1, 2, LANES, tq), lambda b, h, i: (b, h, i, 0, 0, 0)),
                  pl.BlockSpec((1, 1, seq, LANES), lambda b, h, i: (b, h, 0, 0)),
                  pl.BlockSpec((1, 1, nkb, vdim, tk), lambda b, h, i: (b, h, 0, 0, 0)),
                  vec_spec, vec_spec, vec_spec, vec_spec,
                  pl.BlockSpec((DA_V_DIM, 1), lambda b, h, i: (0, 0))],
        out_specs=pl.BlockSpec((tq, LANES), lambda b, h, i: (b * nq + i, h)),
        out_shape=jax.ShapeDtypeStruct((bsz * seq, DA_WIDTH), BF16),
        scratch_shapes=[pltpu.VMEM((2, 2, tk, tq), F32),
                        pltpu.VMEM((2, 2, tk, tq), BF16),
                        pltpu.VMEM((2, 2, 1, tq), F32),
                        pltpu.VMEM((2, vdim, tq), F32),
                        pltpu.VMEM((2, 1, tq), F32),
                        pltpu.VMEM((2, 1, tq), F32),
                        pltpu.VMEM((2, LANES, tq), FP8)],
        compiler_params=_params("parallel", "parallel", "arbitrary"),
        name="attn",
    )(_scores_bounded(q_gain, k_gain), qt, k, vt, vec(lq1), vec(lk1), vec(lq2), vec(lk2),
      subln_g.astype(F32).reshape(DA_V_DIM, 1))


def _gla_decay_matrix():
    c = GLA_CHUNK
    t = np.arange(c)[:, None]
    s = np.arange(c)[None, :]
    blocks = [(s <= t)]
    for h in GLA_LEVELS:
        r = (t // (2 * h)) * (2 * h) + h
        upper = (t >= r) & (s > r) & (s <= t)
        lower = (t < r) & (s > t) & (s <= r)
        blocks.append(upper | lower)
    return jnp.asarray(np.concatenate(blocks, axis=0), dtype=BF16)


def _gla_kernel(q_ref, k_ref, v_ref, la_ref, sr_ref, fm_ref, g_ref, o_ref, st_ref, *, nchunk):
    @pl.when(pl.program_id(1) == 0)
    def _():
        st_ref[...] = jnp.zeros_like(st_ref)

    c = GLA_CHUNK
    ii = lax.broadcasted_iota(jnp.int32, (c, c), 0)
    jj = lax.broadcasted_iota(jnp.int32, (c, c), 1)
    eye = ii == jj
    level_masks = []
    for h in GLA_LEVELS:
        shift = int(math.log2(2 * h))
        same = (ii >> shift) == (jj >> shift)
        level_masks.append(same & ((ii & (2 * h - 1)) >= h) & ((jj & (2 * h - 1)) < h))
    fm = fm_ref[...]
    gain = g_ref[...]

    heads = range(GLA_HEADS)
    ksl = [slice(hd * GLA_KEY_DIM, (hd + 1) * GLA_KEY_DIM) for hd in heads]
    vsl = [slice(hd * GLA_VAL_DIM, (hd + 1) * GLA_VAL_DIM) for hd in heads]

    def local_part(rows):
        la = la_ref[rows, :]
        la_hi, la_lo = _split_bf16(la)
        fa = _dot(fm, la_hi) + _dot(fm, la_lo)
        b = fa[0:c]
        b_last = b[c - 1:c]
        e_b = jnp.exp(b)
        e_rest = jnp.exp(b_last - b)
        e_last = jnp.exp(b_last)
        e_lvl = [jnp.exp(fa[(l + 1) * c:(l + 2) * c]) for l in range(len(GLA_LEVELS))]
        qb = [q_ref[rows, ksl[hd]] for hd in heads]
        kb = [k_ref[rows, ksl[hd]] for hd in heads]
        q = [x.astype(F32) for x in qb]
        k = [x.astype(F32) for x in kb]
        v = [v_ref[rows, vsl[hd]] for hd in heads]
        att = [jnp.where(eye, _dot_nt(qb[hd], kb[hd]), 0.0) for hd in heads]
        for l in range(len(GLA_LEVELS)):
            for hd in heads:
                e = e_lvl[l][:, ksl[hd]]
                lvl = _dot_nt((q[hd] * e).astype(BF16), (k[hd] * e).astype(BF16))
                att[hd] = jnp.where(level_masks[l], lvl, att[hd])
        upd = [_dot_tn(v[hd], (k[hd] * e_rest[:, ksl[hd]]).astype(BF16)) for hd in heads]
        intra = [_dot(att[hd].astype(BF16), v[hd]) for hd in heads]
        qd = [(q[hd] * e_b[:, ksl[hd]]).astype(BF16) for hd in heads]
        return intra, qd, upd, e_last

    def carried_part(rows, local):
        intra, qd, upd, e_last = local
        for hd in heads:
            state = st_ref[hd]
            out = intra[hd] + _dot_nt(qd[hd], state.astype(BF16))
            ms = jnp.mean(out * out, axis=-1, keepdims=True)
            y = out * lax.rsqrt(ms + EPS) * gain * sr_ref[rows, vsl[hd]].astype(F32)
            o_ref[rows, vsl[hd]] = y.astype(BF16)
            st_ref[hd] = state * e_last[:, ksl[hd]] + upd[hd]

    def group(g, carry):
        rows = [pl.ds(pl.multiple_of((g * GLA_GROUP + j) * c, c), c) for j in range(GLA_GROUP)]
        local = [local_part(r) for r in rows]
        for r, loc in zip(rows, local):
            carried_part(r, loc)
        return carry

    lax.fori_loop(0, nchunk // GLA_GROUP, group, 0)


def _gla(gq, gk, act, la, out_gain, bsz, seq):
    t = gq.shape[0]
    tm = min(GLA_TM, seq)
    tpb = seq // tm
    fm = _gla_decay_matrix()
    row = lambda b, n: (b * tpb + n, 0)
    return pl.pallas_call(
        functools.partial(_gla_kernel, nchunk=tm // GLA_CHUNK),
        grid=(bsz, tpb),
        in_specs=[pl.BlockSpec((tm, GLA_QK_WIDTH), row),
                  pl.BlockSpec((tm, GLA_QK_WIDTH), row),
                  pl.BlockSpec((tm, GLA_WIDTH), row),
                  pl.BlockSpec((tm, GLA_QK_WIDTH), row),
                  pl.BlockSpec((tm, GLA_WIDTH), lambda b, n: (b * tpb + n, 1)),
                  pl.BlockSpec(fm.shape, lambda b, n: (0, 0)),
                  pl.BlockSpec((1, GLA_VAL_DIM), lambda b, n: (0, 0))],
        out_specs=pl.BlockSpec((tm, GLA_WIDTH), row),
        out_shape=jax.ShapeDtypeStruct((t, GLA_WIDTH), BF16),
        scratch_shapes=[pltpu.VMEM((GLA_HEADS, GLA_VAL_DIM, GLA_KEY_DIM), F32)],
        compiler_params=_params("parallel", "arbitrary"),
        name="gla",
    )(gq, gk, act, la, act, fm, out_gain.astype(F32).reshape(1, GLA_VAL_DIM))


def _merge_kernel(ya_ref, yb_ref, sa_ref, sb_ref, x_ref, mod_ref, wa_ref, wb_ref, wo_ref, o_ref, mg_ref):
    ya = ya_ref[...]
    yb = yb_ref[...]
    for n in range(mg_ref.shape[1] // MERGE_TN):
        cols = slice(n * MERGE_TN, (n + 1) * MERGE_TN)
        ta = _dot(ya, wa_ref[:, cols])
        tb = _dot(yb, wb_ref[:, cols])
        mg_ref[:, cols] = (sa_ref[:, cols].astype(F32) * ta + sb_ref[:, cols].astype(F32) * tb).astype(BF16)
    o_ref[...] = x_ref[...] + mod_ref[0][2:3] * _dot(mg_ref[...], wo_ref[...])


def _merge(ya, yb, act, x2, mod, w_a, w_b, w_o, seq):
    t, d = x2.shape
    tm = min(MERGE_TM, seq)
    tpb = seq // tm
    gate_blk = 2 * GLA_WIDTH // d
    return pl.pallas_call(
        _merge_kernel,
        grid=(t // tm,),
        in_specs=[pl.BlockSpec((tm, DA_WIDTH), lambda i: (i, 0)),
                  pl.BlockSpec((tm, GLA_WIDTH), lambda i: (i, 0)),
                  pl.BlockSpec((tm, d), lambda i: (i, gate_blk)),
                  pl.BlockSpec((tm, d), lambda i: (i, gate_blk + 1)),
                  pl.BlockSpec((tm, d), lambda i: (i, 0)),
                  pl.BlockSpec((1, N_MOD, d), lambda i: (i // tpb, 0, 0)),
                  _resident(w_a.shape), _resident(w_b.shape), _resident(w_o.shape)],
        out_specs=pl.BlockSpec((tm, d), lambda i: (i, 0)),
        out_shape=jax.ShapeDtypeStruct((t, d), F32),
        scratch_shapes=[pltpu.VMEM((tm, d), BF16)],
        compiler_params=_params("parallel"),
        name="merge",
    )(ya, yb, act, act, x2, mod, w_a, w_b, w_o)


def _mlp_kernel(h_ref, g_ref, mod_ref, w1_ref, w2_ref, o_ref, u_ref):
    @pl.when(pl.program_id(1) == 0)
    def _():
        h = h_ref[...]
        u_ref[...] = _modulated_norm(h, g_ref[...], mod_ref[0], 3, 4).astype(BF16)
        o_ref[...] = h

    hid = jnp.square(jnp.maximum(_dot(u_ref[...], w1_ref[...]), 0.0))
    o_ref[...] += mod_ref[0][5:6] * _dot(hid.astype(BF16), w2_ref[...])


def _mlp(h, gain, mod, w1, w2, seq):
    t, d = h.shape
    tm = min(MLP_TM, seq)
    tpb = seq // tm
    ff = w1.shape[1]
    return pl.pallas_call(
        _mlp_kernel,
        grid=(t // tm, ff // MLP_TF),
        in_specs=[pl.BlockSpec((tm, d), lambda i, kf: (i, 0)),
                  pl.BlockSpec((1, d), lambda i, kf: (0, 0)),
                  pl.BlockSpec((1, N_MOD, d), lambda i, kf: (i // tpb, 0, 0)),
                  pl.BlockSpec((d, MLP_TF), lambda i, kf: (0, kf)),
                  pl.BlockSpec((MLP_TF, d), lambda i, kf: (kf, 0))],
        out_specs=pl.BlockSpec((tm, d), lambda i, kf: (i, 0)),
        out_shape=jax.ShapeDtypeStruct((t, d), F32),
        scratch_shapes=[pltpu.VMEM((tm, d), BF16)],
        compiler_params=_params("parallel", "arbitrary"),
        name="mlp",
    )(h, gain.reshape(1, d), mod, w1, w2)


def _layer(h2, c, positions, bsz, seq, layer_idx, w_ada, b_ada, norm1_g, w_in, da_q_norm_g, da_k_norm_g,
           lq1, lk1, lq2, lk2, da_subln_g, gla_gate_up, gla_gate_bias, gla_out_norm_g,
           w_branch_a, w_branch_b, w_out, norm2_g, w_mlp_in, w_mlp_out):
    d = h2.shape[1]
    assert seq % min(PROJ_TM, seq) == 0 and seq % GLA_CHUNK == 0
    offs = np.concatenate([[0], np.cumsum(IN_SIZES)])
    col = lambda a, b: w_in[:, offs[a]:offs[b]].astype(BF16)
    w_qkv = col(0, 3)
    w_g = jnp.concatenate([col(3, 5), col(7, 8), jnp.zeros((d, LANES - GLA_GATE_RANK), BF16)], axis=1)
    w_act = jnp.concatenate([col(5, 7), col(8, 10)], axis=1)

    mod = _ada(c, w_ada, b_ada)
    u, qt, k, vt = _qkv(h2, norm1_g, mod, w_qkv, positions, da_q_norm_g, da_k_norm_g, bsz, seq)
    gq, gk, la = _glaqk(u, w_g, gla_gate_up, gla_gate_bias, seq)
    act = _act(u, w_act, seq)
    ya = _attention(qt, k, vt, da_q_norm_g, da_k_norm_g, lq1, lk1, lq2, lk2, da_subln_g, layer_idx)
    yb = _gla(gq, gk, act, la, gla_out_norm_g, bsz, seq)
    h1 = _merge(ya, yb, act, h2, mod, w_branch_a.astype(BF16), w_branch_b.astype(BF16),
                w_out.astype(BF16), seq)
    return _mlp(h1, norm2_g, mod, w_mlp_in.astype(BF16), w_mlp_out.astype(BF16), seq)


def kernel(x, c, positions, w_ada, b_ada, norm1_g, w_in, da_q_norm_g, da_k_norm_g, da_lambda_q1,
           da_lambda_k1, da_lambda_q2, da_lambda_k2, da_subln_g, gla_gate_up, gla_gate_bias,
           gla_out_norm_g, w_branch_a, w_branch_b, w_out, norm2_g, w_mlp_in, w_mlp_out):
    bsz, seq, d = x.shape
    h = x.reshape(bsz * seq, d)
    for l in range(w_ada.shape[0]):
        h = _layer(h, c, positions, bsz, seq, l, w_ada[l], b_ada[l], norm1_g[l], w_in[l], da_q_norm_g[l],
                   da_k_norm_g[l], da_lambda_q1[l], da_lambda_k1[l], da_lambda_q2[l], da_lambda_k2[l],
                   da_subln_g[l], gla_gate_up[l], gla_gate_bias[l], gla_out_norm_g[l], w_branch_a[l],
                   w_branch_b[l], w_out[l], norm2_g[l], w_mlp_in[l], w_mlp_out[l])
    return h.reshape(bsz, seq, d)
```

```python
import functools
import math

import jax
import jax.numpy as jnp
import numpy as np
from jax import lax
from jax.experimental import pallas as pl
from jax.experimental.pallas import tpu as pltpu

F32 = jnp.float32
BF16 = jnp.bfloat16

D_MODEL = 2048
DA_HEADS = 8
DA_HEAD_DIM = 64
DA_V_DIM = 2 * DA_HEAD_DIM
DA_QK_WIDTH = DA_HEADS * 2 * DA_HEAD_DIM
DA_WIDTH = DA_HEADS * DA_V_DIM
ROPE_THETA = 500000.0
ROPE_DIM = DA_HEAD_DIM // 4
GLA_HEADS = 4
GLA_KEY_DIM = 128
GLA_VAL_DIM = 256
GLA_QK_WIDTH = GLA_HEADS * GLA_KEY_DIM
GLA_WIDTH = GLA_HEADS * GLA_VAL_DIM
GLA_GATE_RANK = 16
GLA_GATE_TAU = 16.0
GLA_CHUNK = 64
D_FF = 4 * D_MODEL
N_MOD = 6
EPS = 1e-6
IN_SIZES = (DA_QK_WIDTH, DA_QK_WIDTH, DA_WIDTH, GLA_QK_WIDTH, GLA_QK_WIDTH, GLA_WIDTH, GLA_WIDTH,
            GLA_GATE_RANK, D_MODEL, D_MODEL)

LANES = 128
SUBLANES = 8
VMEM_LIMIT_BYTES = 56 * 1024 * 1024

ADA_TN = 1024
PROJ_TM = 512
ACT_TM = 1024
ACT_TN = 1024
ACT_PARTS = 4
GLAQK_PARTS = 2
ATT_TK = PROJ_TM
ATT_TQ = 2 * ATT_TK
ATT_QC = 256
ATT_PAIRS_PER_TRIP = 4
GLA_TM = 512
GLA_GROUP = 4
MERGE_TM = 512
MERGE_TN = 512
MLP_TM = 1024
MLP_TF = 512

GLA_LEVELS = (32, 16, 8, 4, 2, 1)


def _dot(a, b):
    return jnp.dot(a, b, preferred_element_type=F32)


def _dot_nt(a, b):
    return lax.dot_general(a, b, (((1,), (1,)), ((), ())), preferred_element_type=F32)


def _dot_tn(a, b):
    return lax.dot_general(a, b, (((0,), (0,)), ((), ())), preferred_element_type=F32)


def _split_bf16(x):
    hi = x.astype(BF16)
    lo = (x - hi.astype(F32)).astype(BF16)
    return hi, lo


def _params(*semantics):
    return pltpu.CompilerParams(dimension_semantics=semantics, vmem_limit_bytes=VMEM_LIMIT_BYTES)


def _resident(shape):
    nd = len(shape)
    return pl.BlockSpec(shape, lambda *_: (0,) * nd, pipeline_mode=pl.Buffered(1))


def _ada_kernel(c_ref, w_ref, b_ref, o_ref):
    c = c_ref[...]
    sc = c * jax.nn.sigmoid(c)
    o_ref[...] = _dot(sc.astype(BF16), w_ref[...].astype(BF16)) + b_ref[...]


def _ada(c, w_ada, b_ada):
    bsz, d = c.shape
    n = w_ada.shape[1]
    c8 = jnp.zeros((SUBLANES, d), F32).at[:bsz].set(c)
    out = pl.pallas_call(
        _ada_kernel,
        grid=(n // ADA_TN,),
        in_specs=[pl.BlockSpec((SUBLANES, d), lambda j: (0, 0)),
                  pl.BlockSpec((d, ADA_TN), lambda j: (0, j)),
                  pl.BlockSpec((1, ADA_TN), lambda j: (0, j))],
        out_specs=pl.BlockSpec((SUBLANES, ADA_TN), lambda j: (0, j)),
        out_shape=jax.ShapeDtypeStruct((SUBLANES, n), F32),
        compiler_params=_params("arbitrary"),
        name="ada",
    )(c8, w_ada, b_ada.reshape(1, n))
    return out[:bsz].reshape(bsz, N_MOD, d)


def _modulated_norm(x, gain, mod, shift_idx, scale_idx):
    ms = jnp.mean(x * x, axis=-1, keepdims=True)
    y = x * lax.rsqrt(ms + EPS) * gain
    return y * (1.0 + mod[scale_idx:scale_idx + 1]) + mod[shift_idx:shift_idx + 1]


QK_DTYPE = BF16
FP8 = jnp.float8_e4m3fn
GROUP_COLS = 256


def _qkv_kernel(x_ref, g1_ref, mod_ref, w_ref, pos_ref, freq_ref, sa_ref, sb_ref, gq_ref, gk_ref, grp_ref,
                u_ref, qt_ref, k_ref, vt_ref):
    u = _modulated_norm(x_ref[...], g1_ref[...], mod_ref[0], 0, 1).astype(BF16)
    u_ref[...] = u
    ang = pos_ref[...].astype(F32) * freq_ref[...]
    cs = jnp.cos(ang)
    sn = jnp.sin(ang)
    sin_a = sn * sa_ref[...]
    sin_b = sn * sb_ref[...]
    grp = grp_ref[...]
    half = ROPE_DIM // 2

    def group_sums(acc):
        return [_dot(jnp.square(acc[:, j * GROUP_COLS:(j + 1) * GROUP_COLS]).astype(BF16), grp)
                for j in range(acc.shape[1] // GROUP_COLS)]

    def norm_rope(acc, sums, gain):
        slabs = []
        for j, ss in enumerate(sums):
            xn = acc[:, j * GROUP_COLS:(j + 1) * GROUP_COLS] * lax.rsqrt(ss * (1.0 / DA_HEAD_DIM) + EPS)
            for s in range(GROUP_COLS // LANES):
                xs = xn[:, s * LANES:(s + 1) * LANES] * gain
                slabs.append(xs * cs + pltpu.roll(xs, LANES - half, 1) * sin_a + pltpu.roll(xs, half, 1) * sin_b)
        return slabs

    q_scale = (DA_HEAD_DIM ** -0.5) * math.log2(math.e)
    tm = u.shape[0]
    row = lax.broadcasted_iota(jnp.int32, (LANES, tm), 0)
    acc = _dot(u, w_ref[:, 0:DA_QK_WIDTH])
    for h, xq in enumerate(norm_rope(acc, group_sums(acc), gq_ref[...])):
        xt = (xq * q_scale).T
        qt_ref[0, h, 0, 0] = jnp.where(row < DA_HEAD_DIM, xt, 0.0).astype(QK_DTYPE)
        qt_ref[0, h, 0, 1] = jnp.where(row >= DA_HEAD_DIM, xt, 0.0).astype(QK_DTYPE)
    acc = _dot(u, w_ref[:, DA_QK_WIDTH:2 * DA_QK_WIDTH])
    for h, xk in enumerate(norm_rope(acc, group_sums(acc), gk_ref[...])):
        k_ref[0, h] = xk.astype(QK_DTYPE)
    acc = _dot(u, w_ref[:, 2 * DA_QK_WIDTH:2 * DA_QK_WIDTH + DA_WIDTH])
    for h in range(DA_HEADS):
        vt_ref[0, h, 0] = acc[:, h * LANES:(h + 1) * LANES].T.astype(BF16)


def _rope_lane_tables():
    lane = np.arange(LANES) % DA_HEAD_DIM
    inv_freq = ROPE_THETA ** (-jnp.arange(0, ROPE_DIM, 2, dtype=F32) / ROPE_DIM)
    half = ROPE_DIM // 2
    freq = jnp.where(lane < ROPE_DIM, inv_freq[lane % half], 0.0).astype(F32).reshape(1, LANES)
    sign_a = np.where(lane < half, -1.0, 0.0).astype(np.float32).reshape(1, LANES)
    mask_b = np.where((lane >= half) & (lane < ROPE_DIM), 1.0, 0.0).astype(np.float32).reshape(1, LANES)
    col = np.arange(GROUP_COLS)
    group = col[:, None] // DA_HEAD_DIM == col[None, :] // DA_HEAD_DIM
    return freq, jnp.asarray(sign_a), jnp.asarray(mask_b), jnp.asarray(group, dtype=BF16)


def _qkv(x2, gain, mod, w_qkv, positions, q_gain, k_gain, bsz, seq):
    t, d = x2.shape
    tm = min(PROJ_TM, seq)
    tpb = seq // tm
    freq, sign_a, mask_b, group = _rope_lane_tables()
    lane_vec = lambda v: jnp.tile(v.astype(F32), LANES // DA_HEAD_DIM).reshape(1, LANES)
    vec_spec = pl.BlockSpec((1, LANES), lambda i: (0, 0))
    assert tm == ATT_TK and seq % ATT_TQ == 0
    per_q = ATT_TQ // tm
    qt_shape = jax.ShapeDtypeStruct((bsz, DA_HEADS, seq // ATT_TQ, 2, LANES, ATT_TQ), QK_DTYPE)
    qt_spec = pl.BlockSpec((1, DA_HEADS, 1, 2, LANES, tm),
                           lambda i: (i // tpb, 0, (i % tpb) // per_q, 0, 0, (i % tpb) % per_q))
    vt_shape = jax.ShapeDtypeStruct((bsz, DA_HEADS, tpb, DA_V_DIM, tm), BF16)
    vt_spec = pl.BlockSpec((1, DA_HEADS, 1, DA_V_DIM, tm), lambda i: (i // tpb, 0, i % tpb, 0, 0))
    return pl.pallas_call(
        _qkv_kernel,
        grid=(t // tm,),
        in_specs=[pl.BlockSpec((tm, d), lambda i: (i, 0)),
                  pl.BlockSpec((1, d), lambda i: (0, 0)),
                  pl.BlockSpec((1, N_MOD, d), lambda i: (i // tpb, 0, 0)),
                  _resident(w_qkv.shape),
                  pl.BlockSpec((tm, 1), lambda i: (i, 0)),
                  vec_spec, vec_spec, vec_spec, vec_spec, vec_spec,
                  pl.BlockSpec((GROUP_COLS, GROUP_COLS), lambda i: (0, 0))],
        out_specs=[pl.BlockSpec((tm, d), lambda i: (i, 0)),
                   qt_spec,
                   pl.BlockSpec((1, DA_HEADS, tm, LANES), lambda i: (i // tpb, 0, i % tpb, 0)),
                   vt_spec],
        out_shape=[jax.ShapeDtypeStruct((t, d), BF16), qt_shape,
                   jax.ShapeDtypeStruct((bsz, DA_HEADS, seq, LANES), QK_DTYPE), vt_shape],
        compiler_params=_params("parallel"),
        name="qkv",
    )(x2, gain.reshape(1, d), mod, w_qkv, positions.reshape(t, 1), freq, sign_a, mask_b,
      lane_vec(q_gain), lane_vec(k_gain), group)


def _glaqk_kernel(u_ref, w_ref, up_ref, bias_ref, q_ref, k_ref, la_ref):
    w = w_ref[...]
    up_hi, up_lo = _split_bf16(up_ref[...])
    parts = _row_parts(u_ref.shape[0], GLAQK_PARTS)
    accs = [_dot(u_ref[rows, :], w) for rows in parts]
    for rows, acc in zip(parts, accs):
        q_ref[rows, :] = (acc[:, 0:GLA_QK_WIDTH] * (GLA_KEY_DIM ** -0.5)).astype(BF16)
        k_ref[rows, :] = acc[:, GLA_QK_WIDTH:2 * GLA_QK_WIDTH].astype(BF16)
        low = acc[:, 2 * GLA_QK_WIDTH:]
        low_hi, low_lo = _split_bf16(low)
        z = _dot(low_hi, up_hi) + _dot(low_lo, up_hi) + _dot(low_hi, up_lo) + bias_ref[...]
        log_sig = jnp.minimum(z, 0.0) - jnp.log1p(jnp.exp(-jnp.abs(z)))
        la_ref[rows, :] = log_sig * (1.0 / GLA_GATE_TAU)


def _glaqk(u, w_g, gate_up, gate_bias, seq):
    t, d = u.shape
    tm = min(PROJ_TM, seq)
    up = jnp.zeros((LANES, GLA_QK_WIDTH), F32).at[:GLA_GATE_RANK].set(gate_up)
    row_spec = lambda w: pl.BlockSpec((tm, w), lambda i: (i, 0))
    return pl.pallas_call(
        _glaqk_kernel,
        grid=(t // tm,),
        in_specs=[row_spec(d), _resident(w_g.shape), _resident(up.shape),
                  pl.BlockSpec((1, GLA_QK_WIDTH), lambda i: (0, 0))],
        out_specs=[row_spec(GLA_QK_WIDTH), row_spec(GLA_QK_WIDTH), row_spec(GLA_QK_WIDTH)],
        out_shape=[jax.ShapeDtypeStruct((t, GLA_QK_WIDTH), BF16),
                   jax.ShapeDtypeStruct((t, GLA_QK_WIDTH), BF16),
                   jax.ShapeDtypeStruct((t, GLA_QK_WIDTH), F32)],
        compiler_params=_params("parallel"),
        name="glaqk",
    )(u, w_g, up, gate_bias.reshape(1, GLA_QK_WIDTH))


ACT_SILU_BLOCKS = (GLA_WIDTH // ACT_TN, 2 * GLA_WIDTH // ACT_TN)


def _row_parts(rows, parts):
    step = rows // parts
    return [slice(r * step, (r + 1) * step) for r in range(parts)]


def _act_kernel(u_ref, w_ref, o_ref):
    j = pl.program_id(0)
    w = w_ref[...]
    parts = _row_parts(u_ref.shape[0], ACT_PARTS)
    accs = [_dot(u_ref[rows, :], w) for rows in parts]
    for rows, acc in zip(parts, accs):
        sig = jax.nn.sigmoid(acc)
        out = jnp.where(j < ACT_SILU_BLOCKS[0], acc, jnp.where(j < ACT_SILU_BLOCKS[1], acc * sig, sig))
        o_ref[rows, :] = out.astype(BF16)


def _act(u, w_act, seq):
    t, d = u.shape
    n = w_act.shape[1]
    tm = min(ACT_TM, seq)
    return pl.pallas_call(
        _act_kernel,
        grid=(n // ACT_TN, t // tm),
        in_specs=[pl.BlockSpec((tm, d), lambda j, i: (i, 0)),
                  pl.BlockSpec((d, ACT_TN), lambda j, i: (0, j))],
        out_specs=pl.BlockSpec((tm, ACT_TN), lambda j, i: (i, j)),
        out_shape=jax.ShapeDtypeStruct((t, n), BF16),
        compiler_params=_params("arbitrary", "arbitrary"),
        name="act",
    )(u, w_act)


_FULL, _MASK, _SKIP = "full", "mask", "skip"


def _attn_kernel(bounded_ref, qt_ref, k_ref, vt_ref, lq1_ref, lk1_ref, lq2_ref, lk2_ref, subg_ref, o_ref,
                 s_ref, p_ref, cm_ref, acc_ref, m_ref, l_ref, q8_ref, *, lam_init):
    i = pl.program_id(2)
    tq, tk, qc_w = ATT_TQ, ATT_TK, ATT_QC
    nqc = tq // qc_w

    def clear_sums():
        acc_ref[...] = jnp.zeros_like(acc_ref)
        l_ref[...] = jnp.zeros_like(l_ref)

    def scores(t, slot, modes):
        kb = k_ref[0, 0, pl.ds(pl.multiple_of(t * tk, tk), tk), :]
        for c in range(2):
            for q in range(nqc):
                if modes[q] == _SKIP:
                    continue
                cols = slice(q * qc_w, (q + 1) * qc_w)
                s = _dot(kb, qt_ref[0, 0, 0, c, :, cols])
                if modes[q] == _MASK:
                    rel = (lax.broadcasted_iota(jnp.int32, (tk, qc_w), 0)
                           - lax.broadcasted_iota(jnp.int32, (tk, qc_w), 1))
                    s = jnp.where(rel <= i * tq + q * qc_w - t * tk, s, -jnp.inf)
                s_ref[slot, c, :, cols] = s
                cm_ref[slot, c, :, cols] = jnp.max(s, axis=0, keepdims=True)

    def accumulate(t, slot, modes):
        vtb = vt_ref[0, 0, t]
        for c in range(2):
            for q in range(nqc):
                if modes[q] == _SKIP:
                    continue
                cols = slice(q * qc_w, (q + 1) * qc_w)
                m_old = m_ref[c, :, cols]
                m_new = jnp.maximum(m_old, cm_ref[slot, c, :, cols])
                alpha = jnp.exp2(m_old - m_new)
                p = jnp.exp2(s_ref[slot, c, :, cols] - m_new)
                l_ref[c, :, cols] = alpha * l_ref[c, :, cols] + jnp.sum(p, axis=0, keepdims=True)
                acc_ref[c, :, cols] = alpha * acc_ref[c, :, cols] + _dot(vtb, p.astype(BF16))
                m_ref[c, :, cols] = m_new

    half = tk // qc_w
    even_modes = (_MASK,) * half + (_FULL,) * (nqc - half)
    full_modes = (_FULL,) * nqc
    last_modes = (_SKIP,) * half + (_MASK,) * (nqc - half)

    def step(probs_of=None, weigh_of=None):
        if probs_of is not None:
            pt, pslot, pmodes = probs_of
            kb = k_ref[0, 0, pl.ds(pl.multiple_of(pt * tk, tk), tk), :].astype(F32).astype(FP8)
        if weigh_of is not None:
            wt, wslot, wmodes = weigh_of
            vtb = vt_ref[0, 0, wt]
        for c in range(2):
            for q in range(nqc):
                cols = slice(q * qc_w, (q + 1) * qc_w)
                if probs_of is not None and pmodes[q] != _SKIP:
                    s = _dot(kb, q8_ref[c, :, cols])
                    if pmodes[q] == _MASK:
                        rel = (lax.broadcasted_iota(jnp.int32, (tk, qc_w), 0)
                               - lax.broadcasted_iota(jnp.int32, (tk, qc_w), 1))
                        s = jnp.where(rel <= i * tq + q * qc_w - pt * tk, s, -jnp.inf)
                    p = jnp.exp2(s)
                    l_ref[c, :, cols] += jnp.sum(p, axis=0, keepdims=True)
                    p_ref[pslot, c, :, cols] = p.astype(BF16)
                if weigh_of is not None and wmodes[q] != _SKIP:
                    acc_ref[c, :, cols] += _dot(vtb, p_ref[wslot, c, :, cols])

    @pl.when(bounded_ref[0] != 0)
    def _():
        clear_sums()
        q8_ref[...] = qt_ref[0, 0, 0].astype(F32).astype(FP8)
        step(probs_of=(0, 0, even_modes))

        def pair(p, next_modes):
            step(probs_of=(2 * p + 1, 1, full_modes), weigh_of=(2 * p, 0, full_modes))
            step(probs_of=(2 * p + 2, 0, next_modes), weigh_of=(2 * p + 1, 1, full_modes))

        unmasked = jnp.maximum(i - 1, 0)

        def group(first, count):
            for n in range(count):
                pair(first + n, full_modes)

        trips = unmasked // ATT_PAIRS_PER_TRIP
        lax.fori_loop(0, trips, lambda j, carry: (group(j * ATT_PAIRS_PER_TRIP, ATT_PAIRS_PER_TRIP), carry)[1], 0)
        done = trips * ATT_PAIRS_PER_TRIP
        size = ATT_PAIRS_PER_TRIP // 2
        while size >= 1:
            @pl.when((unmasked & size) != 0)
            def _(done=done, size=size):
                group(done, size)

            done = done + (unmasked & size)
            size //= 2

        def diagonal_blocks():
            step(probs_of=(2 * i + 1, 1, last_modes), weigh_of=(2 * i, 0, full_modes))
            step(weigh_of=(2 * i + 1, 1, last_modes))

        @pl.when(i > 0)
        def _():
            pair(i - 1, even_modes)
            diagonal_blocks()

        @pl.when(i == 0)
        def _():
            diagonal_blocks()

    @pl.when(bounded_ref[0] == 0)
    def _():
        clear_sums()
        m_ref[...] = jnp.full_like(m_ref, -jnp.inf)
        scores(0, 0, even_modes)

        def pair(p, carry):
            scores(2 * p + 1, 1, full_modes)
            accumulate(2 * p, 0, full_modes)
            scores(2 * p + 2, 0, even_modes)
            accumulate(2 * p + 1, 1, full_modes)
            return carry

        lax.fori_loop(0, i, pair, 0)
        scores(2 * i + 1, 1, last_modes)
        accumulate(2 * i, 0, full_modes)
        accumulate(2 * i + 1, 1, last_modes)

    s1 = jnp.sum(lq1_ref[...] * lk1_ref[...], axis=1, keepdims=True)
    s2 = jnp.sum(lq2_ref[...] * lk2_ref[...], axis=1, keepdims=True)
    lam = jnp.exp(s1) - jnp.exp(s2) + lam_init
    o = acc_ref[0] / l_ref[0] - lam * (acc_ref[1] / l_ref[1])
    ms = jnp.mean(o * o, axis=0, keepdims=True)
    y = o * lax.rsqrt(ms + EPS) * subg_ref[...] * (1.0 - lam_init)
    o_ref[...] = y.T.astype(BF16)


ATT_DIRECT_LOG2 = 18.0


def _scores_bounded(q_gain, k_gain):
    q_scale = (DA_HEAD_DIM ** -0.5) * math.log2(math.e)
    bound = 1.02 * DA_HEAD_DIM * q_scale * jnp.max(jnp.abs(q_gain)) * jnp.max(jnp.abs(k_gain))
    return (bound <= ATT_DIRECT_LOG2).astype(jnp.int32).reshape(1)


def _attention(qt, k, vt, q_gain, k_gain, lq1, lk1, lq2, lk2, subln_g, layer_idx):
    bsz, heads, nq, _, _, tq = qt.shape
    nkb, vdim, tk = vt.shape[2:]
    seq = nq * tq
    assert (tq, tk, vdim) == (ATT_TQ, ATT_TK, DA_V_DIM) and tq == 2 * tk and nkb * tk == seq
    lam_init = 0.8 - 0.6 * math.exp(-0.3 * layer_idx)
    vec = lambda v: v.astype(F32).reshape(1, DA_HEAD_DIM)
    vec_spec = pl.BlockSpec((1, DA_HEAD_DIM), lambda b, h, i: (0, 0))
    return pl.pallas_call(
        functools.partial(_attn_kernel, lam_init=lam_init),
        grid=(bsz, heads, nq),
        in_specs=[pl.BlockSpec(memory_space=pltpu.SMEM),
                  pl.BlockSpec((1, 1, 1, 2, LANES, tq), lambda b, h, i: (b, h, i, 0, 0, 0)),
                  pl.BlockSpec((1, 1, seq, LANES), lambda b, h, i: (b, h, 0, 0)),
                  pl.BlockSpec((1, 1, nkb, vdim, tk), lambda b, h, i: (b, h, 0, 0, 0)),
                  vec_spec, vec_spec, vec_spec, vec_spec,
                  pl.BlockSpec((DA_V_DIM, 1), lambda b, h, i: (0, 0))],
        out_specs=pl.BlockSpec((tq, LANES), lambda b, h, i: (b * nq + i, h)),
        out_shape=jax.ShapeDtypeStruct((bsz * seq, DA_WIDTH), BF16),
        scratch_shapes=[pltpu.VMEM((2, 2, tk, tq), F32),
                        pltpu.VMEM((2, 2, tk, tq), BF16),
                        pltpu.VMEM((2, 2, 1, tq), F32),
                        pltpu.VMEM((2, vdim, tq), F32),
                        pltpu.VMEM((2, 1, tq), F32),
                        pltpu.VMEM((2, 1, tq), F32),
                        pltpu.VMEM((2, LANES, tq), FP8)],
        compiler_params=_params("parallel", "parallel", "arbitrary"),
        name="attn",
    )(_scores_bounded(q_gain, k_gain), qt, k, vt, vec(lq1), vec(lk1), vec(lq2), vec(lk2),
      subln_g.astype(F32).reshape(DA_V_DIM, 1))


def _gla_decay_matrix():
    c = GLA_CHUNK
    t = np.arange(c)[:, None]
    s = np.arange(c)[None, :]
    blocks = [(s <= t)]
    for h in GLA_LEVELS:
        r = (t // (2 * h)) * (2 * h) + h
        upper = (t >= r) & (s > r) & (s <= t)
        lower = (t < r) & (s > t) & (s <= r)
        blocks.append(upper | lower)
    return jnp.asarray(np.concatenate(blocks, axis=0), dtype=BF16)


def _gla_kernel(q_ref, k_ref, v_ref, la_ref, sr_ref, fm_ref, g_ref, o_ref, st_ref, *, nchunk):
    @pl.when(pl.program_id(1) == 0)
    def _():
        st_ref[...] = jnp.zeros_like(st_ref)

    c = GLA_CHUNK
    ii = lax.broadcasted_iota(jnp.int32, (c, c), 0)
    jj = lax.broadcasted_iota(jnp.int32, (c, c), 1)
    eye = ii == jj
    level_masks = []
    for h in GLA_LEVELS:
        shift = int(math.log2(2 * h))
        same = (ii >> shift) == (jj >> shift)
        level_masks.append(same & ((ii & (2 * h - 1)) >= h) & ((jj & (2 * h - 1)) < h))
    fm = fm_ref[...]
    gain = g_ref[...]

    heads = range(GLA_HEADS)
    ksl = [slice(hd * GLA_KEY_DIM, (hd + 1) * GLA_KEY_DIM) for hd in heads]
    vsl = [slice(hd * GLA_VAL_DIM, (hd + 1) * GLA_VAL_DIM) for hd in heads]

    def local_part(rows):
        la = la_ref[rows, :]
        la_hi, la_lo = _split_bf16(la)
        fa = _dot(fm, la_hi) + _dot(fm, la_lo)
        b = fa[0:c]
        b_last = b[c - 1:c]
        e_b = jnp.exp(b)
        e_rest = jnp.exp(b_last - b)
        e_last = jnp.exp(b_last)
        e_lvl = [jnp.exp(fa[(l + 1) * c:(l + 2) * c]) for l in range(len(GLA_LEVELS))]
        qb = [q_ref[rows, ksl[hd]] for hd in heads]
        kb = [k_ref[rows, ksl[hd]] for hd in heads]
        q = [x.astype(F32) for x in qb]
        k = [x.astype(F32) for x in kb]
        v = [v_ref[rows, vsl[hd]] for hd in heads]
        att = [jnp.where(eye, _dot_nt(qb[hd], kb[hd]), 0.0) for hd in heads]
        for l in range(len(GLA_LEVELS)):
            for hd in heads:
                e = e_lvl[l][:, ksl[hd]]
                lvl = _dot_nt((q[hd] * e).astype(BF16), (k[hd] * e).astype(BF16))
                att[hd] = jnp.where(level_masks[l], lvl, att[hd])
        upd = [_dot_tn(v[hd], (k[hd] * e_rest[:, ksl[hd]]).astype(BF16)) for hd in heads]
        intra = [_dot(att[hd].astype(BF16), v[hd]) for hd in heads]
        qd = [(q[hd] * e_b[:, ksl[hd]]).astype(BF16) for hd in heads]
        return intra, qd, upd, e_last

    def carried_part(rows, local):
        intra, qd, upd, e_last = local
        for hd in heads:
            state = st_ref[hd]
            out = intra[hd] + _dot_nt(qd[hd], state.astype(BF16))
            ms = jnp.mean(out * out, axis=-1, keepdims=True)
            y = out * lax.rsqrt(ms + EPS) * gain * sr_ref[rows, vsl[hd]].astype(F32)
            o_ref[rows, vsl[hd]] = y.astype(BF16)
            st_ref[hd] = state * e_last[:, ksl[hd]] + upd[hd]

    def group(g, carry):
        rows = [pl.ds(pl.multiple_of((g * GLA_GROUP + j) * c, c), c) for j in range(GLA_GROUP)]
        local = [local_part(r) for r in rows]
        for r, loc in zip(rows, local):
            carried_part(r, loc)
        return carry

    lax.fori_loop(0, nchunk // GLA_GROUP, group, 0)


def _gla(gq, gk, act, la, out_gain, bsz, seq):
    t = gq.shape[0]
    tm = min(GLA_TM, seq)
    tpb = seq // tm
    fm = _gla_decay_matrix()
    row = lambda b, n: (b * tpb + n, 0)
    return pl.pallas_call(
        functools.partial(_gla_kernel, nchunk=tm // GLA_CHUNK),
        grid=(bsz, tpb),
        in_specs=[pl.BlockSpec((tm, GLA_QK_WIDTH), row),
                  pl.BlockSpec((tm, GLA_QK_WIDTH), row),
                  pl.BlockSpec((tm, GLA_WIDTH), row),
                  pl.BlockSpec((tm, GLA_QK_WIDTH), row),
                  pl.BlockSpec((tm, GLA_WIDTH), lambda b, n: (b * tpb + n, 1)),
                  pl.BlockSpec(fm.shape, lambda b, n: (0, 0)),
                  pl.BlockSpec((1, GLA_VAL_DIM), lambda b, n: (0, 0))],
        out_specs=pl.BlockSpec((tm, GLA_WIDTH), row),
        out_shape=jax.ShapeDtypeStruct((t, GLA_WIDTH), BF16),
        scratch_shapes=[pltpu.VMEM((GLA_HEADS, GLA_VAL_DIM, GLA_KEY_DIM), F32)],
        compiler_params=_params("parallel", "arbitrary"),
        name="gla",
    )(gq, gk, act, la, act, fm, out_gain.astype(F32).reshape(1, GLA_VAL_DIM))


def _merge_kernel(ya_ref, yb_ref, sa_ref, sb_ref, x_ref, mod_ref, wa_ref, wb_ref, wo_ref, o_ref, mg_ref):
    ya = ya_ref[...]
    yb = yb_ref[...]
    for n in range(mg_ref.shape[1] // MERGE_TN):
        cols = slice(n * MERGE_TN, (n + 1) * MERGE_TN)
        ta = _dot(ya, wa_ref[:, cols])
        tb = _dot(yb, wb_ref[:, cols])
        mg_ref[:, cols] = (sa_ref[:, cols].astype(F32) * ta + sb_ref[:, cols].astype(F32) * tb).astype(BF16)
    o_ref[...] = x_ref[...] + mod_ref[0][2:3] * _dot(mg_ref[...], wo_ref[...])


def _merge(ya, yb, act, x2, mod, w_a, w_b, w_o, seq):
    t, d = x2.shape
    tm = min(MERGE_TM, seq)
    tpb = seq // tm
    gate_blk = 2 * GLA_WIDTH // d
    return pl.pallas_call(
        _merge_kernel,
        grid=(t // tm,),
        in_specs=[pl.BlockSpec((tm, DA_WIDTH), lambda i: (i, 0)),
                  pl.BlockSpec((tm, GLA_WIDTH), lambda i: (i, 0)),
                  pl.BlockSpec((tm, d), lambda i: (i, gate_blk)),
                  pl.BlockSpec((tm, d), lambda i: (i, gate_blk + 1)),
                  pl.BlockSpec((tm, d), lambda i: (i, 0)),
                  pl.BlockSpec((1, N_MOD, d), lambda i: (i // tpb, 0, 0)),
                  _resident(w_a.shape), _resident(w_b.shape), _resident(w_o.shape)],
        out_specs=pl.BlockSpec((tm, d), lambda i: (i, 0)),
        out_shape=jax.ShapeDtypeStruct((t, d), F32),
        scratch_shapes=[pltpu.VMEM((tm, d), BF16)],
        compiler_params=_params("parallel"),
        name="merge",
    )(ya, yb, act, act, x2, mod, w_a, w_b, w_o)


def _mlp_kernel(h_ref, g_ref, mod_ref, w1_ref, w2_ref, o_ref, u_ref):
    @pl.when(pl.program_id(1) == 0)
    def _():
        h = h_ref[...]
        u_ref[...] = _modulated_norm(h, g_ref[...], mod_ref[0], 3, 4).astype(BF16)
        o_ref[...] = h

    hid = jnp.square(jnp.maximum(_dot(u_ref[...], w1_ref[...]), 0.0))
    o_ref[...] += mod_ref[0][5:6] * _dot(hid.astype(BF16), w2_ref[...])


def _mlp(h, gain, mod, w1, w2, seq):
    t, d = h.shape
    tm = min(MLP_TM, seq)
    tpb = seq // tm
    ff = w1.shape[1]
    return pl.pallas_call(
        _mlp_kernel,
        grid=(t // tm, ff // MLP_TF),
        in_specs=[pl.BlockSpec((tm, d), lambda i, kf: (i, 0)),
                  pl.BlockSpec((1, d), lambda i, kf: (0, 0)),
                  pl.BlockSpec((1, N_MOD, d), lambda i, kf: (i // tpb, 0, 0)),
                  pl.BlockSpec((d, MLP_TF), lambda i, kf: (0, kf)),
                  pl.BlockSpec((MLP_TF, d), lambda i, kf: (kf, 0))],
        out_specs=pl.BlockSpec((tm, d), lambda i, kf: (i, 0)),
        out_shape=jax.ShapeDtypeStruct((t, d), F32),
        scratch_shapes=[pltpu.VMEM((tm, d), BF16)],
        compiler_params=_params("parallel", "arbitrary"),
        name="mlp",
    )(h, gain.reshape(1, d), mod, w1, w2)


def _layer(h2, c, positions, bsz, seq, layer_idx, w_ada, b_ada, norm1_g, w_in, da_q_norm_g, da_k_norm_g,
           lq1, lk1, lq2, lk2, da_subln_g, gla_gate_up, gla_gate_bias, gla_out_norm_g,
           w_branch_a, w_branch_b, w_out, norm2_g, w_mlp_in, w_mlp_out):
    d = h2.shape[1]
    assert seq % min(PROJ_TM, seq) == 0 and seq % GLA_CHUNK == 0
    offs = np.concatenate([[0], np.cumsum(IN_SIZES)])
    col = lambda a, b: w_in[:, offs[a]:offs[b]].astype(BF16)
    w_qkv = col(0, 3)
    w_g = jnp.concatenate([col(3, 5), col(7, 8), jnp.zeros((d, LANES - GLA_GATE_RANK), BF16)], axis=1)
    w_act = jnp.concatenate([col(5, 7), col(8, 10)], axis=1)

    mod = _ada(c, w_ada, b_ada)
    u, qt, k, vt = _qkv(h2, norm1_g, mod, w_qkv, positions, da_q_norm_g, da_k_norm_g, bsz, seq)
    gq, gk, la = _glaqk(u, w_g, gla_gate_up, gla_gate_bias, seq)
    act = _act(u, w_act, seq)
    ya = _attention(qt, k, vt, da_q_norm_g, da_k_norm_g, lq1, lk1, lq2, lk2, da_subln_g, layer_idx)
    yb = _gla(gq, gk, act, la, gla_out_norm_g, bsz, seq)
    h1 = _merge(ya, yb, act, h2, mod, w_branch_a.astype(BF16), w_branch_b.astype(BF16),
                w_out.astype(BF16), seq)
    return _mlp(h1, norm2_g, mod, w_mlp_in.astype(BF16), w_mlp_out.astype(BF16), seq)


def kernel(x, c, positions, w_ada, b_ada, norm1_g, w_in, da_q_norm_g, da_k_norm_g, da_lambda_q1,
           da_lambda_k1, da_lambda_q2, da_lambda_k2, da_subln_g, gla_gate_up, gla_gate_bias,
           gla_out_norm_g, w_branch_a, w_branch_b, w_out, norm2_g, w_mlp_in, w_mlp_out):
    bsz, seq, d = x.shape
    h = x.reshape(bsz * seq, d)
    for l in range(w_ada.shape[0]):
        h = _layer(h, c, positions, bsz, seq, l, w_ada[l], b_ada[l], norm1_g[l], w_in[l], da_q_norm_g[l],
                   da_k_norm_g[l], da_lambda_q1[l], da_lambda_k1[l], da_lambda_q2[l], da_lambda_k2[l],
                   da_subln_g[l], gla_gate_up[l], gla_gate_bias[l], gla_out_norm_g[l], w_branch_a[l],
                   w_branch_b[l], w_out[l], norm2_g[l], w_mlp_in[l], w_mlp_out[l])
    return h.reshape(bsz, seq, d)
```
